```python
import math
import jax, jax.numpy as jnp
from jax import lax
import numpy as np

D_MODEL = 1024
BATCH = 16
SEQ = 2048
DEPTH = 2

PLE_DIM = 256
ROPE_THETA = 500000.0
Q_BLOCK = 128

MLA_HEADS = 8
MLA_Q_LORA = 256
MLA_KV_LORA = 128
MLA_NOPE = 64
MLA_ROPE = 32
MLA_V = 64
MOBA_HEADS = 8
MOBA_HEAD_DIM = 64
MOBA_ROT = MOBA_HEAD_DIM // 4
MOBA_BLOCK = 256
MOBA_TOPK = 3
MOBA_WIDTH = MOBA_HEADS * MOBA_HEAD_DIM
ATT_IN_DIM = MLA_Q_LORA + MLA_KV_LORA + MLA_ROPE + 3 * MOBA_WIDTH
ATT_OUT_IN = MLA_HEADS * MLA_V + MOBA_WIDTH
SSD_INNER = 2 * D_MODEL
SSD_HEAD_DIM = 64
SSD_HEADS = SSD_INNER // SSD_HEAD_DIM
SSD_GROUPS = 4
SSD_STATE = 128
SSD_CONV = 4
SSD_CHUNK = 128
SSD_CONV_DIM = SSD_INNER + 2 * SSD_GROUPS * SSD_STATE
SSD_IN_DIM = SSD_INNER + SSD_CONV_DIM + SSD_HEADS
D_FF = 2816
FFN_CONV = 3

LN_EPS = 1e-5
RMS_EPS = 1e-6
DEEPNORM_ALPHA = (2 * DEPTH) ** 0.25
DEEPNORM_BETA = (8 * DEPTH) ** -0.25

kernel_name = "hybrid_mla_moba_ssd_convffn_deepnorm"


def layer_norm(x, g, b):
    xf = x.astype(jnp.float32)
    mu = jnp.mean(xf, axis=-1, keepdims=True)
    var = jnp.mean(jnp.square(xf - mu), axis=-1, keepdims=True)
    return ((xf - mu) * lax.rsqrt(var + LN_EPS) * g.astype(jnp.float32) + b.astype(jnp.float32)).astype(x.dtype)


def rms_norm(x, g):
    xf = x.astype(jnp.float32)
    ms = jnp.mean(jnp.square(xf), axis=-1, keepdims=True)
    return (xf * lax.rsqrt(ms + RMS_EPS) * g.astype(jnp.float32)).astype(x.dtype)


def group_rms_norm(y, g):
    B, S, C = y.shape
    yg = y.reshape(B, S, SSD_GROUPS, C // SSD_GROUPS)
    yg = yg * lax.rsqrt(jnp.mean(jnp.square(yg), axis=-1, keepdims=True) + RMS_EPS)
    return yg.reshape(B, S, C) * g.astype(jnp.float32)


def rope_cos_sin(positions, rot_dim):
    inv_freq = ROPE_THETA ** (-jnp.arange(0, rot_dim, 2, dtype=jnp.float32) / rot_dim)
    ang = positions.astype(jnp.float32)[..., None] * inv_freq
    return jnp.cos(ang)[:, :, None, :], jnp.sin(ang)[:, :, None, :]


def apply_rope(x, cos, sin):
    xf = x.astype(jnp.float32)
    x1, x2 = jnp.split(xf, 2, axis=-1)
    return jnp.concatenate([x1 * cos - x2 * sin, x2 * cos + x1 * sin], axis=-1).astype(x.dtype)


def causal_depthwise_conv(x, w, b):
    k, c = w.shape
    y = lax.conv_general_dilated(x, w[:, None, :], window_strides=(1,), padding=[(k - 1, 0)],
                                 dimension_numbers=('NWC', 'WIO', 'NWC'), feature_group_count=c)
    return y + b


def to_query_blocks(t):
    B, S = t.shape[0], t.shape[1]
    return t.reshape(B, S // Q_BLOCK, Q_BLOCK, *t.shape[2:]).swapaxes(0, 1)


def from_query_blocks(o):
    o = o.swapaxes(0, 1)
    return o.reshape(o.shape[0], o.shape[1] * o.shape[2], *o.shape[3:])


def mla_attention(q, k, v):
    S, dqk = q.shape[1], q.shape[3]
    scale = dqk ** -0.5
    kpos = jnp.arange(S)

    def block(args):
        qb, bi = args
        qpos = bi * Q_BLOCK + jnp.arange(Q_BLOCK)
        s = jnp.einsum('bqhd,bkhd->bhqk', qb, k).astype(jnp.float32) * scale
        s = jnp.where(kpos[None, :] <= qpos[:, None], s, -jnp.inf)
        pr = jax.nn.softmax(s, axis=-1).astype(v.dtype)
        return jnp.einsum('bhqk,bkhd->bqhd', pr, v)

    out = lax.map(block, (to_query_blocks(q), jnp.arange(S // Q_BLOCK)))
    return from_query_blocks(out)


def moba_attention(q, k, v):
    B, S, H, dh = q.shape
    scale = dh ** -0.5
    nb = -(-S // MOBA_BLOCK)
    n_sel = min(MOBA_TOPK, nb)
    pad = nb * MOBA_BLOCK - S
    kp = jnp.pad(k, ((0, 0), (0, pad), (0, 0), (0, 0)))
    vp = jnp.pad(v, ((0, 0), (0, pad), (0, 0), (0, 0)))
    kb = kp.reshape(B, nb, MOBA_BLOCK, H, dh).transpose(0, 3, 1, 2, 4)
    vb = vp.reshape(B, nb, MOBA_BLOCK, H, dh).transpose(0, 3, 1, 2, 4)
    k_mean = jnp.mean(kb.astype(jnp.float32), axis=3)
    b_idx = jnp.arange(B)[:, None, None, None]
    h_idx = jnp.arange(H)[None, :, None, None]
    slots = jnp.arange(n_sel)

    def block(args):
        qb, bi = args
        q0 = bi * Q_BLOCK
        own = q0 // MOBA_BLOCK
        qpos = q0 + jnp.arange(Q_BLOCK)
        gate = jnp.einsum('bqhd,bhnd->bhqn', qb.astype(jnp.float32), k_mean)
        gate = jnp.where(jnp.arange(nb) < own, gate, -jnp.inf)
        _, sel = lax.top_k(gate, n_sel)
        k_sel = kb[b_idx, h_idx, sel]
        v_sel = vb[b_idx, h_idx, sel]
        s_sel = jnp.einsum('bqhd,bhqrkd->bhqrk', qb, k_sel).astype(jnp.float32) * scale
        s_sel = jnp.where((slots < own)[:, None], s_sel, -jnp.inf)
        k_own = lax.dynamic_slice_in_dim(kp, own * MOBA_BLOCK, MOBA_BLOCK, axis=1)
        v_own = lax.dynamic_slice_in_dim(vp, own * MOBA_BLOCK, MOBA_BLOCK, axis=1)
        s_own = jnp.einsum('bqhd,bkhd->bhqk', qb, k_own).astype(jnp.float32) * scale
        kpos = own * MOBA_BLOCK + jnp.arange(MOBA_BLOCK)
        s_own = jnp.where(kpos[None, :] <= qpos[:, None], s_own, -jnp.inf)
        n_g = n_sel * MOBA_BLOCK
        s_all = jnp.concatenate([s_sel.reshape(B, H, Q_BLOCK, n_g), s_own], axis=-1)
        pr = jax.nn.softmax(s_all, axis=-1).astype(v.dtype)
        p_sel = pr[..., :n_g].reshape(B, H, Q_BLOCK, n_sel, MOBA_BLOCK)
        p_own = pr[..., n_g:]
        return (jnp.einsum('bhqrk,bhqrkd->bqhd', p_sel, v_sel)
                + jnp.einsum('bhqk,bkhd->bqhd', p_own, v_own))

    out = lax.map(block, (to_query_blocks(q), jnp.arange(S // Q_BLOCK)))
    return from_query_blocks(out)


def hybrid_attention_mixer(x, cos_m, sin_m, cos_b, sin_b, w_in, q_norm, w_uq, kv_norm, w_ukv, w_out):
    B, S, _ = x.shape
    h = x @ w_in
    o1 = MLA_Q_LORA
    o2 = o1 + MLA_KV_LORA
    o3 = o2 + MLA_ROPE
    c_q, c_kv, k_rope, q_b, k_b, v_b = jnp.split(
        h, [o1, o2, o3, o3 + MOBA_WIDTH, o3 + 2 * MOBA_WIDTH], axis=-1)
    q = (rms_norm(c_q, q_norm) @ w_uq).reshape(B, S, MLA_HEADS, MLA_NOPE + MLA_ROPE)
    q = jnp.concatenate([q[..., :MLA_NOPE], apply_rope(q[..., MLA_NOPE:], cos_m, sin_m)], axis=-1)
    kv = (rms_norm(c_kv, kv_norm) @ w_ukv).reshape(B, S, MLA_HEADS, MLA_NOPE + MLA_V)
    k_r = apply_rope(k_rope[:, :, None, :], cos_m, sin_m)
    k = jnp.concatenate([kv[..., :MLA_NOPE], jnp.broadcast_to(k_r, (B, S, MLA_HEADS, MLA_ROPE))], axis=-1)
    o_mla = mla_attention(q, k, kv[..., MLA_NOPE:])
    def partial_rope(t):
        t = t.reshape(B, S, MOBA_HEADS, MOBA_HEAD_DIM)
        return jnp.concatenate([apply_rope(t[..., :MOBA_ROT], cos_b, sin_b), t[..., MOBA_ROT:]], axis=-1)
    o_moba = moba_attention(partial_rope(q_b), partial_rope(k_b),
                            v_b.reshape(B, S, MOBA_HEADS, MOBA_HEAD_DIM))
    o = jnp.concatenate([o_mla.reshape(B, S, -1), o_moba.reshape(B, S, -1)], axis=-1)
    return o @ w_out


def ssd_chunked_scan(x, dt, a, bm, cm):
    B, S, H, P = x.shape
    G, N = bm.shape[2], bm.shape[3]
    hg = H // G
    L = SSD_CHUNK
    nc = S // L

    def chunks(t):
        return t.reshape(B, nc, L, *t.shape[2:]).swapaxes(0, 1)

    xc = chunks(x.astype(jnp.float32).reshape(B, S, G, hg, P))
    dtc = chunks(dt.reshape(B, S, G, hg))
    bc = chunks(bm.astype(jnp.float32))
    cc = chunks(cm.astype(jnp.float32))
    a_g = a.reshape(G, hg)
    causal = jnp.tril(jnp.ones((L, L), dtype=bool))[None, :, :, None, None]

    def step(state, inp):
        xk, dtk, bk, ck = inp
        la = jnp.cumsum(dtk * a_g, axis=1)
        seg = la[:, :, None] - la[:, None, :]
        decay = jnp.exp(jnp.where(causal, seg, -jnp.inf))
        xdt = xk * dtk[..., None]
        cb = jnp.einsum('bign,bjgn->bijg', ck, bk)
        y = jnp.einsum('bijgh,bjghp->bighp', cb[..., None] * decay, xdt)
        y = y + jnp.einsum('bign,bghpn->bighp', ck, state) * jnp.exp(la)[..., None]
        to_end = jnp.exp(la[:, -1:] - la)
        state = (state * jnp.exp(la[:, -1])[..., None, None]
                 + jnp.einsum('bjgn,bjgh,bjghp->bghpn', bk, to_end, xdt))
        return state, y

    state0 = jnp.zeros((B, G, hg, P, N), jnp.float32)
    _, ys = lax.scan(step, state0, (xc, dtc, bc, cc))
    return ys.swapaxes(0, 1).reshape(B, S, H, P)


def ssd_mixer(x, w_in, conv_w, conv_b, dt_bias, a_log, d_skip, norm_w, w_out):
    B, S, _ = x.shape
    h = x @ w_in
    z, xbc, dt = jnp.split(h, [SSD_INNER, SSD_INNER + SSD_CONV_DIM], axis=-1)
    xbc = jax.nn.silu(causal_depthwise_conv(xbc, conv_w, conv_b))
    xs, bm, cm = jnp.split(xbc, [SSD_INNER, SSD_INNER + SSD_GROUPS * SSD_STATE], axis=-1)
    xs = xs.reshape(B, S, SSD_HEADS, SSD_HEAD_DIM)
    bm = bm.reshape(B, S, SSD_GROUPS, SSD_STATE)
    cm = cm.reshape(B, S, SSD_GROUPS, SSD_STATE)
    dt = jax.nn.softplus(dt.astype(jnp.float32) + dt_bias.astype(jnp.float32))
    a = -jnp.exp(a_log.astype(jnp.float32))
    y = ssd_chunked_scan(xs, dt, a, bm, cm)
    y = y + d_skip.astype(jnp.float32)[:, None] * xs.astype(jnp.float32)
    y = y.reshape(B, S, SSD_INNER) * jax.nn.silu(z.astype(jnp.float32))
    y = group_rms_norm(y, norm_w).astype(x.dtype)
    return y @ w_out


def conv_ffn(x, w_up, conv_w, conv_b, w_down):
    g, u = jnp.split(x @ w_up, 2, axis=-1)
    g = causal_depthwise_conv(g, conv_w, conv_b)
    return (jax.nn.gelu(g, approximate=False) * u) @ w_down


def setup_inputs(seed: int = 0) -> dict:
    key = jax.random.key(seed)
    ks = iter(jax.random.split(key, 40))
    ne = (DEPTH + 1) // 2
    no = DEPTH // 2

    def nrm(shape, scale):
        return jax.random.normal(next(ks), shape, jnp.float32) * scale

    def gain(shape):
        return 1.0 + nrm(shape, 0.02)

    dt0 = jnp.exp(jax.random.uniform(next(ks), (no, SSD_HEADS), jnp.float32)
                  * (math.log(0.1) - math.log(0.001)) + math.log(0.001))
    return {
        "x": nrm((BATCH, SEQ, D_MODEL), 1.0),
        "p": nrm((DEPTH, BATCH, SEQ, PLE_DIM), 1.0),
        "positions": jnp.broadcast_to(jnp.arange(SEQ, dtype=jnp.int32), (BATCH, SEQ)),
        "att_w_in": nrm((ne, D_MODEL, ATT_IN_DIM), D_MODEL ** -0.5),
        "mla_q_norm": gain((ne, MLA_Q_LORA)),
        "mla_w_uq": nrm((ne, MLA_Q_LORA, MLA_HEADS * (MLA_NOPE + MLA_ROPE)), MLA_Q_LORA ** -0.5),
        "mla_kv_norm": gain((ne, MLA_KV_LORA)),
        "mla_w_ukv": nrm((ne, MLA_KV_LORA, MLA_HEADS * (MLA_NOPE + MLA_V)), MLA_KV_LORA ** -0.5),
        "att_w_out": nrm((ne, ATT_OUT_IN, D_MODEL), ATT_OUT_IN ** -0.5 * DEEPNORM_BETA),
        "ssd_w_in": nrm((no, D_MODEL, SSD_IN_DIM), D_MODEL ** -0.5),
        "ssd_conv_w": nrm((no, SSD_CONV, SSD_CONV_DIM), SSD_CONV ** -0.5),
        "ssd_conv_b": nrm((no, SSD_CONV_DIM), 0.02),
        "ssd_dt_bias": dt0 + jnp.log(-jnp.expm1(-dt0)),
        "ssd_a_log": jnp.log(jax.random.uniform(next(ks), (no, SSD_HEADS), jnp.float32, 1.0, 16.0)),
        "ssd_d": gain((no, SSD_HEADS)),
        "ssd_norm": gain((no, SSD_INNER)),
        "ssd_w_out": nrm((no, SSD_INNER, D_MODEL), SSD_INNER ** -0.5 * DEEPNORM_BETA),
        "ln_mix_g": gain((DEPTH, D_MODEL)),
        "ln_mix_b": nrm((DEPTH, D_MODEL), 0.02),
        "ffn_w_up": nrm((DEPTH, D_MODEL, 2 * D_FF), D_MODEL ** -0.5),
        "ffn_conv_w": nrm((DEPTH, FFN_CONV, D_FF), FFN_CONV ** -0.5),
        "ffn_conv_b": nrm((DEPTH, D_FF), 0.02),
        "ffn_w_down": nrm((DEPTH, D_FF, D_MODEL), D_FF ** -0.5 * DEEPNORM_BETA),
        "ln_ffn_g": gain((DEPTH, D_MODEL)),
        "ln_ffn_b": nrm((DEPTH, D_MODEL), 0.02),
        "ple_w_gate": nrm((DEPTH, D_MODEL, D_MODEL), D_MODEL ** -0.5),
        "ple_w_proj": nrm((DEPTH, PLE_DIM, D_MODEL), PLE_DIM ** -0.5),
    }


def reference(x, p, positions, att_w_in, mla_q_norm, mla_w_uq, mla_kv_norm, mla_w_ukv, att_w_out,
              ssd_w_in, ssd_conv_w, ssd_conv_b, ssd_dt_bias, ssd_a_log, ssd_d, ssd_norm, ssd_w_out,
              ln_mix_g, ln_mix_b, ffn_w_up, ffn_conv_w, ffn_conv_b, ffn_w_down, ln_ffn_g, ln_ffn_b,
              ple_w_gate, ple_w_proj):
    cos_m, sin_m = rope_cos_sin(positions, MLA_ROPE)
    cos_b, sin_b = rope_cos_sin(positions, MOBA_ROT)
    for i in range(DEPTH):
        j = i // 2
        if i % 2 == 0:
            m = hybrid_attention_mixer(x, cos_m, sin_m, cos_b, sin_b, att_w_in[j], mla_q_norm[j],
                                       mla_w_uq[j], mla_kv_norm[j], mla_w_ukv[j], att_w_out[j])
        else:
            m = ssd_mixer(x, ssd_w_in[j], ssd_conv_w[j], ssd_conv_b[j], ssd_dt_bias[j],
                          ssd_a_log[j], ssd_d[j], ssd_norm[j], ssd_w_out[j])
        x = layer_norm(DEEPNORM_ALPHA * x + m, ln_mix_g[i], ln_mix_b[i])
        f = conv_ffn(x, ffn_w_up[i], ffn_conv_w[i], ffn_conv_b[i], ffn_w_down[i])
        x = layer_norm(DEEPNORM_ALPHA * x + f, ln_ffn_g[i], ln_ffn_b[i])
        x = x + jax.nn.sigmoid(x @ ple_w_gate[i]) * (p[i] @ ple_w_proj[i])
    return x
```

```python
import functools
import math

import jax
import jax.numpy as jnp
from jax import lax
from jax.experimental import pallas as pl
from jax.experimental.pallas import tpu as pltpu

D_MODEL = 1024
PLE_DIM = 256
ROPE_THETA = 500000.0
MLA_HEADS = 8
MLA_Q_LORA = 256
MLA_KV_LORA = 128
MLA_NOPE = 64
MLA_ROPE = 32
MLA_V = 64
MOBA_HEADS = 8
MOBA_HEAD_DIM = 64
MOBA_ROT = MOBA_HEAD_DIM // 4
MOBA_BLOCK = 256
MOBA_TOPK = 3
MOBA_WIDTH = MOBA_HEADS * MOBA_HEAD_DIM
SSD_INNER = 2 * D_MODEL
SSD_HEAD_DIM = 64
SSD_HEADS = SSD_INNER // SSD_HEAD_DIM
SSD_GROUPS = 4
SSD_STATE = 128
SSD_CONV = 4
SSD_CHUNK = 128
SSD_CONV_DIM = SSD_INNER + 2 * SSD_GROUPS * SSD_STATE
D_FF = 2816
FFN_CONV = 3
LN_EPS = 1e-5
RMS_EPS = 1e-6
DEPTH = 2
DEEPNORM_ALPHA = (2 * DEPTH) ** 0.25

LANES = 128
SUBLANES = 8
HEAD_PAD = LANES
VMEM_LIMIT = 56 * 1024 * 1024
NEG_BIG = -1e30
INV_SQRT2 = 0.7071067811865476

F32 = jnp.float32
BF16 = jnp.bfloat16


def _dot(a, b):
    return jnp.dot(a, b, preferred_element_type=F32)


def _dot_nt(a, b):
    return lax.dot_general(a, b, (((1,), (1,)), ((), ())), preferred_element_type=F32)


def _const_spec(shape):
    return pl.BlockSpec(shape, lambda *_: (0,) * len(shape), pipeline_mode=pl.Buffered(1))


def _params(semantics):
    return pltpu.CompilerParams(dimension_semantics=semantics, vmem_limit_bytes=VMEM_LIMIT)


def _rope_table_kernel(pos_ref, f_ref, m1_ref, m2_ref, c_ref, s1_ref, s2_ref):
    pos = pos_ref[...].astype(F32)
    for r in range(2):
        ang = pos * f_ref[r:r + 1, :]
        s = jnp.sin(ang)
        c_ref[r] = jnp.cos(ang)
        s1_ref[r] = s * m1_ref[r:r + 1, :]
        s2_ref[r] = s * m2_ref[r:r + 1, :]


def _rope_tables(positions):
    t = positions.size
    inv_m = ROPE_THETA ** (-jnp.arange(0, MLA_ROPE, 2, dtype=F32) / MLA_ROPE)
    inv_b = ROPE_THETA ** (-jnp.arange(0, MOBA_ROT, 2, dtype=F32) / MOBA_ROT)
    hm, hb = MLA_ROPE // 2, MOBA_ROT // 2
    z = lambda n: jnp.zeros((n,), F32)
    o = lambda n: jnp.ones((n,), F32)
    f_m = jnp.concatenate([z(MLA_NOPE), inv_m, inv_m, z(LANES - MLA_NOPE - MLA_ROPE)])
    m1_m = jnp.concatenate([z(MLA_NOPE), -o(hm), z(hm), z(LANES - MLA_NOPE - MLA_ROPE)])
    m2_m = jnp.concatenate([z(MLA_NOPE), z(hm), o(hm), z(LANES - MLA_NOPE - MLA_ROPE)])
    rest = MOBA_HEAD_DIM - MOBA_ROT
    f_b = jnp.tile(jnp.concatenate([inv_b, inv_b, z(rest)]), LANES // MOBA_HEAD_DIM)
    m1_b = jnp.tile(jnp.concatenate([-o(hb), z(hb), z(rest)]), LANES // MOBA_HEAD_DIM)
    m2_b = jnp.tile(jnp.concatenate([z(hb), o(hb), z(rest)]), LANES // MOBA_HEAD_DIM)
    f = jnp.stack([f_m, f_b])
    m1 = jnp.stack([m1_m, m1_b])
    m2 = jnp.stack([m2_m, m2_b])
    tm = min(t, 2048)
    out = jax.ShapeDtypeStruct((2, t, LANES), F32)
    row_spec = _const_spec((2, LANES))
    tab_spec = pl.BlockSpec((2, tm, LANES), lambda i: (0, i, 0))
    return pl.pallas_call(
        _rope_table_kernel,
        grid=(t // tm,),
        in_specs=[pl.BlockSpec((tm, 1), lambda i: (i, 0)), row_spec, row_spec, row_spec],
        out_specs=[tab_spec, tab_spec, tab_spec],
        out_shape=[out, out, out],
        compiler_params=_params(("parallel",)),
        name="rope_tables",
    )(positions.reshape(t, 1), f, m1, m2)


def _rope(t, c, s1, s2, half):
    w = t.shape[-1]
    return t * c + pltpu.roll(t, w - half, 1) * s1 + pltpu.roll(t, half, 1) * s2


def _rms(x, g):
    ms = jnp.mean(jnp.square(x), axis=-1, keepdims=True)
    return x * lax.rsqrt(ms + RMS_EPS) * g


def _layer_norm(x, g, b):
    mu = jnp.mean(x, axis=-1, keepdims=True)
    xc = x - mu
    var = jnp.mean(jnp.square(xc), axis=-1, keepdims=True)
    return xc * lax.rsqrt(var + LN_EPS) * g + b


ATT_PROJ_TM = 256


def _att_proj_kernel(x_ref, win_ref, wuq_ref, wk_ref, wv_ref, qn_ref, kvn_ref, c_ref, s1_ref, s2_ref,
                     qm_ref, km_ref, vm_ref, qb_ref, kb_ref, vb_ref, kmean_ref):
    xb = x_ref[...].astype(BF16)
    cm, s1m, s2m = c_ref[0], s1_ref[0], s2_ref[0]
    cb, s1b, s2b = c_ref[1], s1_ref[1], s2_ref[1]
    hm, hb = MLA_ROPE // 2, MOBA_ROT // 2
    mla_scale = (MLA_NOPE + MLA_ROPE) ** -0.5
    moba_scale = MOBA_HEAD_DIM ** -0.5

    h_lat = _dot(xb, win_ref[:, 0:512])
    c_q = h_lat[:, 0:MLA_Q_LORA]
    c_kv = h_lat[:, MLA_Q_LORA:MLA_Q_LORA + MLA_KV_LORA]
    k_rope = _rope(h_lat[:, 384:512], cm, s1m, s2m, hm)

    q = _dot(_rms(c_q, qn_ref[...]).astype(BF16), wuq_ref[...])
    for h in range(MLA_HEADS):
        sl = slice(h * HEAD_PAD, (h + 1) * HEAD_PAD)
        qm_ref[:, sl] = (_rope(q[:, sl], cm, s1m, s2m, hm) * mla_scale).astype(BF16)

    ckv = _rms(c_kv, kvn_ref[...]).astype(BF16)
    k = _dot(ckv, wk_ref[...])
    for h in range(MLA_HEADS):
        sl = slice(h * HEAD_PAD, (h + 1) * HEAD_PAD)
        km_ref[:, sl] = (k[:, sl] + k_rope).astype(BF16)
    vm_ref[...] = _dot(ckv, wv_ref[...]).astype(BF16)

    hq = _dot(xb, win_ref[:, 512:1024])
    hk = _dot(xb, win_ref[:, 1024:1536])
    for g in range(MOBA_WIDTH // LANES):
        sl = slice(g * LANES, (g + 1) * LANES)
        qb_ref[:, sl] = (_rope(hq[:, sl], cb, s1b, s2b, hb) * moba_scale).astype(BF16)
        kr = _rope(hk[:, sl], cb, s1b, s2b, hb)
        kb_ref[:, sl] = kr.astype(BF16)
        for r in range(ATT_PROJ_TM // MOBA_BLOCK):
            rows = slice(r * MOBA_BLOCK, (r + 1) * MOBA_BLOCK)
            kmean_ref[r, :, sl] = jnp.mean(kr[rows], axis=0, keepdims=True)
    vb_ref[...] = _dot(xb, win_ref[:, 1536:2048]).astype(BF16)


def _att_proj(x2, tabs, w_in, w_uq, w_k, w_v, q_norm, kv_norm):
    t = x2.shape[0]
    tm = ATT_PROJ_TM
    c, s1, s2 = tabs
    row = lambda w: pl.BlockSpec((tm, w), lambda i: (i, 0))
    tab = pl.BlockSpec((2, tm, LANES), lambda i: (0, i, 0))
    bf = lambda w: jax.ShapeDtypeStruct((t, w), BF16)
    nblk = tm // MOBA_BLOCK
    return pl.pallas_call(
        _att_proj_kernel,
        grid=(t // tm,),
        in_specs=[row(D_MODEL), _const_spec(w_in.shape), _const_spec(w_uq.shape), _const_spec(w_k.shape),
                  _const_spec(w_v.shape), _const_spec(q_norm.shape), _const_spec(kv_norm.shape), tab, tab, tab],
        out_specs=[row(1024), row(1024), row(512), row(512), row(512), row(512),
                   pl.BlockSpec((nblk, 1, MOBA_WIDTH), lambda i: (i, 0, 0))],
        out_shape=[bf(1024), bf(1024), bf(512), bf(512), bf(512), bf(512),
                   jax.ShapeDtypeStruct((t // MOBA_BLOCK, 1, MOBA_WIDTH), F32)],
        compiler_params=_params(("parallel",)),
        name="att_proj",
    )(x2, w_in, w_uq, w_k, w_v, q_norm, kv_norm, c, s1, s2)


def _moba_select_kernel(q_ref, kab_ref, r_ref, lc_ref, qa_ref, *, nb):
    own = pl.program_id(1)
    lane = lax.broadcasted_iota(jnp.int32, (1, LANES), 1)
    pair_valid = jnp.where(lc_ref[0:1, :] < own.astype(F32), 1.0, 0.0)
    tie_first = lc_ref[1:2, :]
    low = lane < (LANES // 2)
    lt_own = lane < own
    rest_pen = jnp.where((lane > own) & (lane < nb), NEG_BIG, 0.0)
    for h in range(MOBA_HEADS):
        qp = q_ref[:, (h // 2) * LANES:(h // 2 + 1) * LANES]
        qm = jnp.where(low if h % 2 == 0 else jnp.logical_not(low), qp, jnp.zeros_like(qp))
        ab = _dot(qm, kab_ref[0, h])
        a, b = ab[:, 0:LANES], ab[:, LANES:2 * LANES]
        beats = jnp.where(b > a, 1.0, jnp.where(b == a, tie_first, 0.0)) * pair_valid
        cnt = _dot(beats.astype(BF16), r_ref[...])
        pen = jnp.where(lt_own, jnp.where(cnt < MOBA_TOPK, 0.0, NEG_BIG), rest_pen)
        qa_ref[:, 2 * h * LANES:(2 * h + 1) * LANES] = qm
        qa_ref[:, (2 * h + 1) * LANES:(2 * h + 2) * LANES] = pen.astype(BF16)


def _moba_select(qb, kmean, batch, seq):
    t = qb.shape[0]
    nb = seq // MOBA_BLOCK
    assert nb * nb <= LANES
    km = kmean.reshape(batch, nb, MOBA_HEADS, MOBA_HEAD_DIM).transpose(0, 2, 3, 1)
    a = jnp.repeat(km, nb, axis=-1)
    b = jnp.tile(km, (1, 1, 1, nb))
    zc = jnp.zeros(km.shape[:3] + (LANES - nb * nb,), F32)
    ab = jnp.concatenate([a, zc, b, zc], axis=-1)
    zr = jnp.zeros_like(ab)
    odd = (jnp.arange(MOBA_HEADS) % 2 == 1)[None, :, None, None]
    kab = jnp.where(odd, jnp.concatenate([zr, ab], axis=2), jnp.concatenate([ab, zr], axis=2)).astype(BF16)
    cidx = jnp.arange(LANES)
    rmat = ((cidx[:, None] // nb == cidx[None, :]) & (cidx[:, None] < nb * nb)).astype(BF16)
    used = cidx < nb * nb
    lane_consts = jnp.zeros((SUBLANES, LANES), F32)
    lane_consts = lane_consts.at[0].set(jnp.where(used, cidx % nb, nb).astype(F32))
    lane_consts = lane_consts.at[1].set((used & (cidx % nb < cidx // nb)).astype(F32))
    tq = MOBA_BLOCK
    return pl.pallas_call(
        functools.partial(_moba_select_kernel, nb=nb),
        grid=(batch, seq // tq),
        in_specs=[pl.BlockSpec((tq, MOBA_WIDTH), lambda bi, qi: (bi * (seq // tq) + qi, 0)),
                  pl.BlockSpec((1, MOBA_HEADS, LANES, 2 * LANES), lambda bi, qi: (bi, 0, 0, 0)),
                  _const_spec((LANES, LANES)), _const_spec((SUBLANES, LANES))],
        out_specs=pl.BlockSpec((tq, MOBA_HEADS * 2 * LANES), lambda bi, qi: (bi * (seq // tq) + qi, 0)),
        out_shape=jax.ShapeDtypeStruct((t, MOBA_HEADS * 2 * LANES), BF16),
        compiler_params=_params(("parallel", "parallel")),
        name="moba_select",
    )(qb, kab, rmat, lane_consts)


ATT_TILE = 256


def _attn_kernel(*refs, moba):
    if moba:
        q_ref, k_ref, oh_ref, v_ref, o_ref, va_ref, vb_ref = refs
    else:
        q_ref, k_ref, v_ref, o_ref, va_ref, vb_ref = refs
        oh_ref = None
    seq = q_ref.shape[0]
    tq = ATT_TILE
    qw = q_ref.shape[1] // 2
    lane = lax.broadcasted_iota(jnp.int32, (1, LANES), 1)
    v = v_ref[...]
    va_ref[...] = jnp.where(lane < LANES // 2, v, jnp.zeros_like(v))
    vb_ref[...] = jnp.where(lane >= LANES // 2, v, jnp.zeros_like(v))
    row = lax.broadcasted_iota(jnp.int32, (tq, tq), 0)
    col = lax.broadcasted_iota(jnp.int32, (tq, tq), 1)
    causal = col <= row

    def load_k(c0, hh):
        if moba:
            return jnp.concatenate([k_ref[pl.ds(c0, tq), :], oh_ref[pl.ds(c0, tq), :]], axis=1)
        return k_ref[pl.ds(c0, tq), hh * HEAD_PAD:(hh + 1) * HEAD_PAD]

    def update(s, vt, carry):
        m, l, acc = carry
        m_new = jnp.maximum(m, jnp.max(s, axis=-1, keepdims=True))
        alpha = jnp.exp(m - m_new)
        p = jnp.exp(s - m_new)
        l = alpha * l + jnp.sum(p, axis=-1, keepdims=True)
        acc = alpha * acc + _dot(p.astype(BF16), vt)
        return m_new, l, acc

    def q_tile(qi, _):
        r0 = pl.multiple_of(qi * tq, tq)
        out = jnp.zeros((tq, LANES), F32)
        for hh in range(2):
            q = q_ref[pl.ds(r0, tq), hh * qw:(hh + 1) * qw]
            vs_ref = va_ref if hh == 0 else vb_ref

            def body(j, carry):
                c0 = pl.multiple_of(j * tq, tq)
                return update(_dot_nt(q, load_k(c0, hh)), vs_ref[pl.ds(c0, tq), :], carry)

            init = (jnp.full((tq, 1), -jnp.inf, F32), jnp.zeros((tq, 1), F32), jnp.zeros((tq, LANES), F32))
            carry = lax.fori_loop(0, qi, body, init)
            s = jnp.where(causal, _dot_nt(q, load_k(r0, hh)), -jnp.inf)
            _, l, acc = update(s, vs_ref[pl.ds(r0, tq), :], carry)
            out = out + acc * (1.0 / l)
        o_ref[pl.ds(r0, tq), :] = out.astype(o_ref.dtype)
        return 0

    lax.fori_loop(0, seq // tq, q_tile, 0)


def _attention(q, k, v, batch, seq, onehot=None):
    t = q.shape[0]
    moba = onehot is not None
    pairs = v.shape[1] // LANES
    qw2 = q.shape[1] // pairs
    kw2 = k.shape[1] // pairs
    blk = lambda w: pl.BlockSpec((seq, w), lambda bi, p: (bi, p))
    in_specs = [blk(qw2), blk(kw2)]
    args = [q, k]
    if moba:
        in_specs.append(_const_spec((seq, LANES)))
        args.append(onehot)
    in_specs.append(blk(LANES))
    args.append(v)
    return pl.pallas_call(
        functools.partial(_attn_kernel, moba=moba),
        grid=(batch, pairs),
        in_specs=in_specs,
        out_specs=blk(LANES),
        out_shape=jax.ShapeDtypeStruct((t, pairs * LANES), BF16),
        scratch_shapes=[pltpu.VMEM((seq, LANES), BF16), pltpu.VMEM((seq, LANES), BF16)],
        compiler_params=_params(("parallel", "parallel")),
        name="moba_attn" if moba else "mla_attn",
    )(*args)


OUT_PROJ_TM = 512


def _out_proj_kernel(*refs, n_act):
    acts = refs[:n_act]
    ws = refs[n_act:2 * n_act]
    x_ref, g_ref, b_ref, o_ref = refs[2 * n_act:]
    m = _dot(acts[0][...], ws[0][...])
    for a_ref, w_ref in zip(acts[1:], ws[1:]):
        m = m + _dot(a_ref[...], w_ref[...])
    o_ref[...] = _layer_norm(DEEPNORM_ALPHA * x_ref[...] + m, g_ref[...], b_ref[...])


def _out_proj_ln(acts, ws, x2, g, b):
    t = x2.shape[0]
    tm = OUT_PROJ_TM
    row = lambda w: pl.BlockSpec((tm, w), lambda i: (i, 0))
    return pl.pallas_call(
        functools.partial(_out_proj_kernel, n_act=len(acts)),
        grid=(t // tm,),
        in_specs=[row(a.shape[1]) for a in acts] + [_const_spec(w.shape) for w in ws]
        + [row(D_MODEL), _const_spec(g.shape), _const_spec(b.shape)],
        out_specs=row(D_MODEL),
        out_shape=jax.ShapeDtypeStruct((t, D_MODEL), F32),
        compiler_params=_params(("parallel",)),
        name="out_proj_ln",
    )(*acts, *ws, x2, g, b)


FFN_TM = 256
FFN_CHUNK = 256


def _ffn_kernel(x_ref, p_ref, wup_ref, cw_ref, cb_ref, wdn_ref, g_ref, b_ref, wg_ref, wp_ref, o_ref, gbuf_ref,
                *, tiles_per_seq):
    tm = FFN_TM
    hist = SUBLANES

    @pl.when(pl.program_id(0) % tiles_per_seq == 0)
    def _():
        gbuf_ref[0:hist, :] = jnp.zeros((hist, D_FF), F32)

    x = x_ref[...]
    xb = x.astype(BF16)
    f = jnp.zeros((tm, D_MODEL), F32)
    for c in range(D_FF // FFN_CHUNK):
        cs = slice(c * FFN_CHUNK, (c + 1) * FFN_CHUNK)
        gate = _dot(xb, wup_ref[:, cs])
        up = _dot(xb, wup_ref[:, D_FF + c * FFN_CHUNK:D_FF + (c + 1) * FFN_CHUNK])
        gbuf_ref[hist:hist + tm, cs] = gate
        conv = cw_ref[FFN_CONV - 1:FFN_CONV, cs] * gate + cb_ref[:, cs]
        for k in range(FFN_CONV - 1):
            d = FFN_CONV - 1 - k
            conv = conv + cw_ref[k:k + 1, cs] * gbuf_ref[hist - d:hist - d + tm, cs]
        hid = 0.5 * conv * (1.0 + lax.erf(conv * INV_SQRT2)) * up
        f = f + _dot(hid.astype(BF16), wdn_ref[cs, :])
    gbuf_ref[0:hist, :] = gbuf_ref[tm:tm + hist, :]
    y = _layer_norm(DEEPNORM_ALPHA * x + f, g_ref[...], b_ref[...])
    gate = jax.nn.sigmoid(_dot(y.astype(BF16), wg_ref[...]))
    o_ref[...] = y + gate * _dot(p_ref[...].astype(BF16), wp_ref[...])


def _ffn_ln_ple(x2, p2, w_up, conv_w, conv_b, w_down, g, b, w_gate, w_proj, seq):
    t = x2.shape[0]
    tm = FFN_TM
    row = lambda w: pl.BlockSpec((tm, w), lambda i: (i, 0))
    consts = [w_up, conv_w, conv_b, w_down, g, b, w_gate, w_proj]
    return pl.pallas_call(
        functools.partial(_ffn_kernel, tiles_per_seq=seq // tm),
        grid=(t // tm,),
        in_specs=[row(D_MODEL), row(PLE_DIM)] + [_const_spec(w.shape) for w in consts],
        out_specs=row(D_MODEL),
        out_shape=jax.ShapeDtypeStruct((t, D_MODEL), F32),
        scratch_shapes=[pltpu.VMEM((tm + SUBLANES, D_FF), F32)],
        compiler_params=_params(("arbitrary",)),
        name="ffn_ln_ple",
    )(x2, p2, *consts)


SSD_PROJ_TM = 256
SSD_PROJ_CHUNK = 512
SSD_BC = SSD_GROUPS * SSD_STATE


def _ssd_proj_kernel(x_ref, win_ref, cw_ref, cb_ref, dtb_ref, z_ref, xs_ref, b_ref, c_ref, dt_ref, cbuf_ref,
                     *, tiles_per_seq):
    tm = SSD_PROJ_TM
    hist = SUBLANES
    ck = SSD_PROJ_CHUNK

    @pl.when(pl.program_id(0) % tiles_per_seq == 0)
    def _():
        cbuf_ref[0:hist, :] = jnp.zeros((hist, SSD_CONV_DIM), F32)

    xb = x_ref[...].astype(BF16)
    for c in range(SSD_INNER // ck):
        z_ref[:, c * ck:(c + 1) * ck] = _dot(xb, win_ref[:, c * ck:(c + 1) * ck])
    for c in range(SSD_CONV_DIM // ck):
        cs = slice(c * ck, (c + 1) * ck)
        h = _dot(xb, win_ref[:, SSD_INNER + c * ck:SSD_INNER + (c + 1) * ck])
        cbuf_ref[hist:hist + tm, cs] = h
        conv = cw_ref[SSD_CONV - 1:SSD_CONV, cs] * h + cb_ref[:, cs]
        for k in range(SSD_CONV - 1):
            d = SSD_CONV - 1 - k
            conv = conv + cw_ref[k:k + 1, cs] * cbuf_ref[hist - d:hist - d + tm, cs]
        act = conv * jax.nn.sigmoid(conv)
        if c < SSD_INNER // ck:
            xs_ref[:, cs] = act
        elif c == SSD_INNER // ck:
            b_ref[...] = act
        else:
            c_ref[...] = act
    cbuf_ref[0:hist, :] = cbuf_ref[tm:tm + hist, :]
    hd = _dot(xb, win_ref[:, SSD_INNER + SSD_CONV_DIM:]) + dtb_ref[...]
    dt_ref[...] = jnp.maximum(hd, 0.0) + jnp.log1p(jnp.exp(-jnp.abs(hd)))


def _ssd_proj(x2, w_in, conv_w, conv_b, dt_bias, seq):
    t = x2.shape[0]
    tm = SSD_PROJ_TM
    assert SSD_BC == SSD_PROJ_CHUNK
    row = lambda w: pl.BlockSpec((tm, w), lambda i: (i, 0))
    f32 = lambda w: jax.ShapeDtypeStruct((t, w), F32)
    consts = [w_in, conv_w, conv_b, dt_bias]
    return pl.pallas_call(
        functools.partial(_ssd_proj_kernel, tiles_per_seq=seq // tm),
        grid=(t // tm,),
        in_specs=[row(D_MODEL)] + [_const_spec(w.shape) for w in consts],
        out_specs=[row(SSD_INNER), row(SSD_INNER), row(SSD_BC), row(SSD_BC), row(LANES)],
        out_shape=[f32(SSD_INNER), f32(SSD_INNER), f32(SSD_BC), f32(SSD_BC), f32(LANES)],
        scratch_shapes=[pltpu.VMEM((tm + SUBLANES, SSD_CONV_DIM), F32)],
        compiler_params=_params(("arbitrary",)),
        name="ssd_proj",
    )(x2, *consts)


def _ssd_scan_kernel(xs_ref, z_ref, b_ref, c_ref, dt_ref, a_ref, d_ref, nw_ref, tri_ref, o_ref, state_ref):
    L = SSD_CHUNK
    hg = SSD_HEADS // SSD_GROUPS
    gw = hg * SSD_HEAD_DIM

    @pl.when(pl.program_id(1) == 0)
    def _():
        state_ref[...] = jnp.zeros_like(state_ref)

    lane = lax.broadcasted_iota(jnp.int32, (1, LANES), 1)
    low = lane < LANES // 2
    row = lax.broadcasted_iota(jnp.int32, (L, L), 0)
    col = lax.broadcasted_iota(jnp.int32, (L, L), 1)
    causal = col <= row

    dt = dt_ref[...]
    la = jnp.dot(tri_ref[...], dt * a_ref[...], preferred_element_type=F32,
                 precision=lax.Precision.HIGHEST)
    la_t = la.T

    for g in range(SSD_GROUPS):
        bg = b_ref[:, g * SSD_STATE:(g + 1) * SSD_STATE]
        cg = c_ref[:, g * SSD_STATE:(g + 1) * SSD_STATE].astype(BF16)
        cb = _dot_nt(cg, bg.astype(BF16))
        bg_t = bg.T.astype(BF16)
        y_parts = []
        for pp in range(hg // 2):
            h0 = g * hg + 2 * pp
            ps = slice(h0 * SSD_HEAD_DIM, (h0 + 2) * SSD_HEAD_DIM)
            xs = xs_ref[:, ps]
            dt_pair = jnp.where(low, dt[:, h0:h0 + 1], dt[:, h0 + 1:h0 + 2])
            la_pair = jnp.where(low, la[:, h0:h0 + 1], la[:, h0 + 1:h0 + 2])
            xdt = xs * dt_pair
            xdt_b = xdt.astype(BF16)
            y = jnp.zeros((L, LANES), F32)
            for hh in range(2):
                seg = la[:, h0 + hh:h0 + hh + 1] - la_t[h0 + hh:h0 + hh + 1, :]
                w = (cb * jnp.exp(jnp.where(causal, seg, -jnp.inf))).astype(BF16)
                keep = low if hh == 0 else jnp.logical_not(low)
                y = y + _dot(w, jnp.where(keep, xdt_b, jnp.zeros_like(xdt_b)))
            st = state_ref[:, ps]
            y = y + _dot(cg, st.astype(BF16)) * jnp.exp(la_pair)
            la_end = la_pair[L - 1:L, :]
            to_end = jnp.exp(la_end - la_pair)
            state_ref[:, ps] = st * jnp.exp(la_end) + _dot(bg_t, (xdt * to_end).astype(BF16))
            y = y + d_ref[:, ps] * xs
            zz = z_ref[:, ps]
            y_parts.append(y * (zz * jax.nn.sigmoid(zz)))
        yg = jnp.concatenate(y_parts, axis=1)
        ms = jnp.mean(jnp.square(yg), axis=-1, keepdims=True)
        gs = slice(g * gw, (g + 1) * gw)
        o_ref[:, gs] = (yg * lax.rsqrt(ms + RMS_EPS) * nw_ref[:, gs]).astype(o_ref.dtype)


def _ssd_scan(xs, z, bm, cm, dt, a_row, d_row, norm_w, batch, seq):
    t = xs.shape[0]
    L = SSD_CHUNK
    nc = seq // L
    assert SSD_INNER // SSD_GROUPS == SSD_BC
    row = lambda w: pl.BlockSpec((L, w), lambda bi, ci: (bi * nc + ci, 0))
    tri = jnp.tril(jnp.ones((L, L), F32))
    consts = [a_row, d_row, norm_w, tri]
    return pl.pallas_call(
        _ssd_scan_kernel,
        grid=(batch, nc),
        in_specs=[row(SSD_INNER), row(SSD_INNER), row(SSD_BC), row(SSD_BC), row(LANES)]
        + [_const_spec(w.shape) for w in consts],
        out_specs=row(SSD_INNER),
        out_shape=jax.ShapeDtypeStruct((t, SSD_INNER), BF16),
        scratch_shapes=[pltpu.VMEM((SSD_STATE, SSD_INNER), F32)],
        compiler_params=_params(("parallel", "arbitrary")),
        name="ssd_scan",
    )(xs, z, bm, cm, dt, *consts)


def _pad_cols(w, n):
    return jnp.concatenate([w, jnp.zeros((w.shape[0], n), w.dtype)], axis=1)


def _attention_layer(x2, tabs, batch, seq, w_in, q_norm, w_uq, kv_norm, w_ukv, w_out):
    o1 = MLA_Q_LORA + MLA_KV_LORA
    zc = lambda n: jnp.zeros((D_MODEL, n), w_in.dtype)
    w_in2 = jnp.concatenate([w_in[:, :o1], zc(MLA_NOPE), w_in[:, o1:o1 + MLA_ROPE],
                             zc(LANES - MLA_NOPE - MLA_ROPE), w_in[:, o1 + MLA_ROPE:]], axis=1).astype(BF16)
    dqk = MLA_NOPE + MLA_ROPE
    w_uq2 = jnp.pad(w_uq.reshape(MLA_Q_LORA, MLA_HEADS, dqk), ((0, 0), (0, 0), (0, HEAD_PAD - dqk)))
    w_uq2 = w_uq2.reshape(MLA_Q_LORA, MLA_HEADS * HEAD_PAD).astype(BF16)
    w_kv3 = w_ukv.reshape(MLA_KV_LORA, MLA_HEADS, MLA_NOPE + MLA_V)
    w_k2 = jnp.pad(w_kv3[:, :, :MLA_NOPE], ((0, 0), (0, 0), (0, HEAD_PAD - MLA_NOPE)))
    w_k2 = w_k2.reshape(MLA_KV_LORA, MLA_HEADS * HEAD_PAD).astype(BF16)
    w_v2 = w_kv3[:, :, MLA_NOPE:].reshape(MLA_KV_LORA, MLA_HEADS * MLA_V).astype(BF16)

    qm, km, vm, qb, kb, vb, kmean = _att_proj(x2, tabs, w_in2, w_uq2, w_k2, w_v2,
                                              q_norm.reshape(1, -1), kv_norm.reshape(1, -1))
    o_mla = _attention(qm, km, vm, batch, seq)
    qa = _moba_select(qb, kmean, batch, seq)
    pos = jnp.arange(seq)
    onehot = (pos[:, None] // MOBA_BLOCK == jnp.arange(LANES)[None, :]).astype(BF16)
    o_moba = _attention(qa, kb, vb, batch, seq, onehot=onehot)
    n_mla = MLA_HEADS * MLA_V
    w_o = w_out.astype(BF16)
    return [o_mla, o_moba], [w_o[:n_mla], w_o[n_mla:]]


def _ssd_layer(x2, batch, seq, w_in, conv_w, conv_b, dt_bias, a_log, d_skip, norm_w, w_out):
    w_in2 = _pad_cols(w_in, LANES - SSD_HEADS).astype(BF16)
    dtb = _pad_cols(dt_bias.reshape(1, -1), LANES - SSD_HEADS)
    z, xs, bm, cm, dt = _ssd_proj(x2, w_in2, conv_w, conv_b.reshape(1, -1), dtb, seq)
    a_row = _pad_cols((-jnp.exp(a_log.astype(F32))).reshape(1, -1), LANES - SSD_HEADS)
    d_row = jnp.repeat(d_skip.astype(F32), SSD_HEAD_DIM).reshape(1, -1)
    y = _ssd_scan(xs, z, bm, cm, dt, a_row, d_row, norm_w.reshape(1, -1), batch, seq)
    return [y], [w_out.astype(BF16)]


def kernel(x, p, positions, att_w_in, mla_q_norm, mla_w_uq, mla_kv_norm, mla_w_ukv, att_w_out, ssd_w_in, ssd_conv_w, ssd_conv_b, ssd_dt_bias, ssd_a_log, ssd_d, ssd_norm, ssd_w_out, ln_mix_g, ln_mix_b, ffn_w_up, ffn_conv_w, ffn_conv_b, ffn_w_down, ln_ffn_g, ln_ffn_b, ple_w_gate, ple_w_proj):
    batch, seq, _ = x.shape
    depth = p.shape[0]
    t = batch * seq
    x2 = x.reshape(t, D_MODEL)
    tabs = _rope_tables(positions)
    for i in range(depth):
        j = i // 2
        if i % 2 == 0:
            acts, ws = _attention_layer(x2, tabs, batch, seq, att_w_in[j], mla_q_norm[j], mla_w_uq[j],
                                        mla_kv_norm[j], mla_w_ukv[j], att_w_out[j])
        else:
            acts, ws = _ssd_layer(x2, batch, seq, ssd_w_in[j], ssd_conv_w[j], ssd_conv_b[j], ssd_dt_bias[j],
                                  ssd_a_log[j], ssd_d[j], ssd_norm[j], ssd_w_out[j])
        x2 = _out_proj_ln(acts, ws, x2, ln_mix_g[i].reshape(1, -1), ln_mix_b[i].reshape(1, -1))
        x2 = _ffn_ln_ple(x2, p[i].reshape(t, PLE_DIM), ffn_w_up[i].astype(BF16), ffn_conv_w[i],
                         ffn_conv_b[i].reshape(1, -1), ffn_w_down[i].astype(BF16),
                         ln_ffn_g[i].reshape(1, -1), ln_ffn_b[i].reshape(1, -1),
                         ple_w_gate[i].astype(BF16), ple_w_proj[i].astype(BF16), seq)
    return x2.reshape(batch, seq, D_MODEL)
```

```python
import functools
import math

import jax
import jax.numpy as jnp
from jax import lax
from jax.experimental import pallas as pl
from jax.experimental.pallas import tpu as pltpu

D_MODEL = 1024
PLE_DIM = 256
ROPE_THETA = 500000.0
MLA_HEADS = 8
MLA_Q_LORA = 256
MLA_KV_LORA = 128
MLA_NOPE = 64
MLA_ROPE = 32
MLA_V = 64
MOBA_HEADS = 8
MOBA_HEAD_DIM = 64
MOBA_ROT = MOBA_HEAD_DIM // 4
MOBA_BLOCK = 256
MOBA_TOPK = 3
MOBA_WIDTH = MOBA_HEADS * MOBA_HEAD_DIM
SSD_INNER = 2 * D_MODEL
SSD_HEAD_DIM = 64
SSD_HEADS = SSD_INNER // SSD_HEAD_DIM
SSD_GROUPS = 4
SSD_STATE = 128
SSD_CONV = 4
SSD_CHUNK = 128
SSD_CONV_DIM = SSD_INNER + 2 * SSD_GROUPS * SSD_STATE
D_FF = 2816
FFN_CONV = 3
LN_EPS = 1e-5
RMS_EPS = 1e-6
DEPTH = 2
DEEPNORM_ALPHA = (2 * DEPTH) ** 0.25

LANES = 128
SUBLANES = 8
HEAD_PAD = LANES
VMEM_LIMIT = 56 * 1024 * 1024
NEG_BIG = -1e30
INV_SQRT2 = 0.7071067811865476
LOG2_E = 1.4426950408889634

F32 = jnp.float32
BF16 = jnp.bfloat16


def _dot(a, b):
    return jnp.dot(a, b, preferred_element_type=F32)


def _dot_nt(a, b):
    return lax.dot_general(a, b, (((1,), (1,)), ((), ())), preferred_element_type=F32)


def _const_spec(shape):
    return pl.BlockSpec(shape, lambda *_: (0,) * len(shape), pipeline_mode=pl.Buffered(1))


def _params(semantics):
    return pltpu.CompilerParams(dimension_semantics=semantics, vmem_limit_bytes=VMEM_LIMIT)


def _rope_table_kernel(pos_ref, f_ref, m1_ref, m2_ref, c_ref, s1_ref, s2_ref):
    pos = pos_ref[...].astype(F32)
    for r in range(2):
        ang = pos * f_ref[r:r + 1, :]
        s = jnp.sin(ang)
        c_ref[r] = jnp.cos(ang)
        s1_ref[r] = s * m1_ref[r:r + 1, :]
        s2_ref[r] = s * m2_ref[r:r + 1, :]


def _rope_tables(positions):
    t = positions.size
    inv_m = ROPE_THETA ** (-jnp.arange(0, MLA_ROPE, 2, dtype=F32) / MLA_ROPE)
    inv_b = ROPE_THETA ** (-jnp.arange(0, MOBA_ROT, 2, dtype=F32) / MOBA_ROT)
    hm, hb = MLA_ROPE // 2, MOBA_ROT // 2
    z = lambda n: jnp.zeros((n,), F32)
    o = lambda n: jnp.ones((n,), F32)
    f_m = jnp.concatenate([z(MLA_NOPE), inv_m, inv_m, z(LANES - MLA_NOPE - MLA_ROPE)])
    m1_m = jnp.concatenate([z(MLA_NOPE), -o(hm), z(hm), z(LANES - MLA_NOPE - MLA_ROPE)])
    m2_m = jnp.concatenate([z(MLA_NOPE), z(hm), o(hm), z(LANES - MLA_NOPE - MLA_ROPE)])
    rest = MOBA_HEAD_DIM - MOBA_ROT
    f_b = jnp.tile(jnp.concatenate([inv_b, inv_b, z(rest)]), LANES // MOBA_HEAD_DIM)
    m1_b = jnp.tile(jnp.concatenate([-o(hb), z(hb), z(rest)]), LANES // MOBA_HEAD_DIM)
    m2_b = jnp.tile(jnp.concatenate([z(hb), o(hb), z(rest)]), LANES // MOBA_HEAD_DIM)
    f = jnp.stack([f_m, f_b])
    m1 = jnp.stack([m1_m, m1_b])
    m2 = jnp.stack([m2_m, m2_b])
    tm = min(t, 2048)
    out = jax.ShapeDtypeStruct((2, t, LANES), F32)
    row_spec = _const_spec((2, LANES))
    tab_spec = pl.BlockSpec((2, tm, LANES), lambda i: (0, i, 0))
    return pl.pallas_call(
        _rope_table_kernel,
        grid=(t // tm,),
        in_specs=[pl.BlockSpec((tm, 1), lambda i: (i, 0)), row_spec, row_spec, row_spec],
        out_specs=[tab_spec, tab_spec, tab_spec],
        out_shape=[out, out, out],
        compiler_params=_params(("parallel",)),
        name="rope_tables",
    )(positions.reshape(t, 1), f, m1, m2)


def _rope(t, c, s1, s2, half):
    w = t.shape[-1]
    return t * c + pltpu.roll(t, w - half, 1) * s1 + pltpu.roll(t, half, 1) * s2


def _rms(x, g):
    ms = jnp.mean(jnp.square(x), axis=-1, keepdims=True)
    return x * lax.rsqrt(ms + RMS_EPS) * g


def _layer_norm(x, g, b):
    mu = jnp.mean(x, axis=-1, keepdims=True)
    xc = x - mu
    var = jnp.mean(jnp.square(xc), axis=-1, keepdims=True)
    return xc * lax.rsqrt(var + LN_EPS) * g + b


ATT_PROJ_TM = 256


def _att_proj_kernel(x_ref, win_ref, wuq_ref, wk_ref, wv_ref, wvb_ref, qn_ref, kvn_ref, c_ref, s1_ref, s2_ref,
                     qm_ref, km_ref, vm_ref, qb_ref, kb_ref, vb_ref, kmean_ref):
    xb = x_ref[...].astype(BF16)
    cm, s1m, s2m = c_ref[0], s1_ref[0], s2_ref[0]
    cb, s1b, s2b = c_ref[1], s1_ref[1], s2_ref[1]
    hm, hb = MLA_ROPE // 2, MOBA_ROT // 2
    mla_scale = (MLA_NOPE + MLA_ROPE) ** -0.5 * LOG2_E
    moba_scale = MOBA_HEAD_DIM ** -0.5 * LOG2_E

    h_lat = _dot(xb, win_ref[:, 0:512])
    c_q = h_lat[:, 0:MLA_Q_LORA]
    c_kv = h_lat[:, MLA_Q_LORA:MLA_Q_LORA + MLA_KV_LORA]
    k_rope = _rope(h_lat[:, 384:512], cm, s1m, s2m, hm)

    q = _dot(_rms(c_q, qn_ref[...]).astype(BF16), wuq_ref[...])
    for h in range(MLA_HEADS):
        sl = slice(h * HEAD_PAD, (h + 1) * HEAD_PAD)
        qm_ref[:, sl] = (_rope(q[:, sl], cm, s1m, s2m, hm) * mla_scale).astype(BF16)

    ckv = _rms(c_kv, kvn_ref[...]).astype(BF16)
    k = _dot(ckv, wk_ref[...])
    for h in range(MLA_HEADS):
        sl = slice(h * HEAD_PAD, (h + 1) * HEAD_PAD)
        km_ref[:, sl] = (k[:, sl] + k_rope).astype(BF16)
    vm_ref[...] = _dot_nt(wv_ref[...], ckv).astype(BF16)

    hq = _dot(xb, win_ref[:, 512:1024])
    hk = _dot(xb, win_ref[:, 1024:1536])
    for g in range(MOBA_WIDTH // LANES):
        sl = slice(g * LANES, (g + 1) * LANES)
        qb_ref[:, sl] = (_rope(hq[:, sl], cb, s1b, s2b, hb) * moba_scale).astype(BF16)
        kr = _rope(hk[:, sl], cb, s1b, s2b, hb)
        kb_ref[:, sl] = kr.astype(BF16)
        for r in range(ATT_PROJ_TM // MOBA_BLOCK):
            rows = slice(r * MOBA_BLOCK, (r + 1) * MOBA_BLOCK)
            kmean_ref[r, :, sl] = jnp.mean(kr[rows], axis=0, keepdims=True)
    vb_ref[...] = _dot_nt(wvb_ref[...], xb).astype(BF16)


def _att_proj(x2, tabs, w_in, w_uq, w_k, w_v_t, w_vb_t, q_norm, kv_norm):
    t = x2.shape[0]
    tm = ATT_PROJ_TM
    c, s1, s2 = tabs
    row = lambda w: pl.BlockSpec((tm, w), lambda i: (i, 0))
    col = lambda h: pl.BlockSpec((h, tm), lambda i: (0, i))
    tab = pl.BlockSpec((2, tm, LANES), lambda i: (0, i, 0))
    bf = lambda w: jax.ShapeDtypeStruct((t, w), BF16)
    bf_t = lambda h: jax.ShapeDtypeStruct((h, t), BF16)
    nblk = tm // MOBA_BLOCK
    consts = [w_in, w_uq, w_k, w_v_t, w_vb_t, q_norm, kv_norm]
    return pl.pallas_call(
        _att_proj_kernel,
        grid=(t // tm,),
        in_specs=[row(D_MODEL)] + [_const_spec(w.shape) for w in consts] + [tab, tab, tab],
        out_specs=[row(1024), row(1024), col(512), row(512), row(512), col(512),
                   pl.BlockSpec((nblk, 1, MOBA_WIDTH), lambda i: (i, 0, 0))],
        out_shape=[bf(1024), bf(1024), bf_t(512), bf(512), bf(512), bf_t(512),
                   jax.ShapeDtypeStruct((t // MOBA_BLOCK, 1, MOBA_WIDTH), F32)],
        compiler_params=_params(("parallel",)),
        name="att_proj",
    )(x2, *consts, c, s1, s2)


def _moba_select_kernel(q_ref, kab_ref, r_ref, lc_ref, qa_ref, *, nb):
    own = pl.program_id(1)
    lane = lax.broadcasted_iota(jnp.int32, (1, LANES), 1)
    pair_valid = jnp.where(lc_ref[0:1, :] < own.astype(F32), 1.0, 0.0)
    tie_first = lc_ref[1:2, :]
    low = lane < (LANES // 2)
    lt_own = lane < own
    rest_pen = jnp.where((lane > own) & (lane < nb), NEG_BIG, 0.0)
    for h in range(MOBA_HEADS):
        qp = q_ref[:, (h // 2) * LANES:(h // 2 + 1) * LANES]
        qm = jnp.where(low if h % 2 == 0 else jnp.logical_not(low), qp, jnp.zeros_like(qp))
        ab = _dot(qm, kab_ref[0, h])
        a, b = ab[:, 0:LANES], ab[:, LANES:2 * LANES]
        beats = jnp.where(b > a, 1.0, jnp.where(b == a, tie_first, 0.0)) * pair_valid
        cnt = _dot(beats.astype(BF16), r_ref[...])
        pen = jnp.where(lt_own, jnp.where(cnt < MOBA_TOPK, 0.0, NEG_BIG), rest_pen)
        qa_ref[:, 2 * h * LANES:(2 * h + 1) * LANES] = qm
        qa_ref[:, (2 * h + 1) * LANES:(2 * h + 2) * LANES] = pen.astype(BF16)


def _moba_select(qb, kmean, batch, seq):
    t = qb.shape[0]
    nb = seq // MOBA_BLOCK
    assert nb * nb <= LANES
    km = kmean.reshape(batch, nb, MOBA_HEADS, MOBA_HEAD_DIM).transpose(0, 2, 3, 1)
    a = jnp.repeat(km, nb, axis=-1)
    b = jnp.tile(km, (1, 1, 1, nb))
    zc = jnp.zeros(km.shape[:3] + (LANES - nb * nb,), F32)
    ab = jnp.concatenate([a, zc, b, zc], axis=-1)
    zr = jnp.zeros_like(ab)
    odd = (jnp.arange(MOBA_HEADS) % 2 == 1)[None, :, None, None]
    kab = jnp.where(odd, jnp.concatenate([zr, ab], axis=2), jnp.concatenate([ab, zr], axis=2)).astype(BF16)
    cidx = jnp.arange(LANES)
    rmat = ((cidx[:, None] // nb == cidx[None, :]) & (cidx[:, None] < nb * nb)).astype(BF16)
    used = cidx < nb * nb
    lane_consts = jnp.zeros((SUBLANES, LANES), F32)
    lane_consts = lane_consts.at[0].set(jnp.where(used, cidx % nb, nb).astype(F32))
    lane_consts = lane_consts.at[1].set((used & (cidx % nb < cidx // nb)).astype(F32))
    tq = MOBA_BLOCK
    return pl.pallas_call(
        functools.partial(_moba_select_kernel, nb=nb),
        grid=(batch, seq // tq),
        in_specs=[pl.BlockSpec((tq, MOBA_WIDTH), lambda bi, qi: (bi * (seq // tq) + qi, 0)),
                  pl.BlockSpec((1, MOBA_HEADS, LANES, 2 * LANES), lambda bi, qi: (bi, 0, 0, 0)),
                  _const_spec((LANES, LANES)), _const_spec((SUBLANES, LANES))],
        out_specs=pl.BlockSpec((tq, MOBA_HEADS * 2 * LANES), lambda bi, qi: (bi * (seq // tq) + qi, 0)),
        out_shape=jax.ShapeDtypeStruct((t, MOBA_HEADS * 2 * LANES), BF16),
        compiler_params=_params(("parallel", "parallel")),
        name="moba_select",
    )(qb, kab, rmat, lane_consts)


ATT_TILE = 256
ATT_PAIRS = 2


def _attn_kernel(*refs, moba):
    if moba:
        q_ref, k_ref, oh_ref, v_ref, o_ref, va_ref, vb_ref = refs
    else:
        q_ref, k_ref, v_ref, o_ref, va_ref, vb_ref = refs
        oh_ref = None
    seq = q_ref.shape[0]
    tq = ATT_TILE
    pairs = v_ref.shape[0] // LANES
    heads = 2 * pairs
    qw = q_ref.shape[1] // heads
    sub = lax.broadcasted_iota(jnp.int32, (LANES, 1), 0)
    for p in range(pairs):
        ps = slice(p * LANES, (p + 1) * LANES)
        v = v_ref[ps, :]
        va_ref[ps, :] = jnp.where(sub < LANES // 2, v, jnp.zeros_like(v))
        vb_ref[ps, :] = jnp.where(sub >= LANES // 2, v, jnp.zeros_like(v))
    key = lax.broadcasted_iota(jnp.int32, (tq, tq), 0)
    qry = lax.broadcasted_iota(jnp.int32, (tq, tq), 1)
    causal = key <= qry

    def load_q(r0, h):
        return q_ref[pl.ds(r0, tq), h * qw:(h + 1) * qw]

    def load_k(c0, h):
        if moba:
            ps = slice((h // 2) * LANES, (h // 2 + 1) * LANES)
            return jnp.concatenate([k_ref[pl.ds(c0, tq), ps], oh_ref[pl.ds(c0, tq), :]], axis=1)
        return k_ref[pl.ds(c0, tq), h * HEAD_PAD:(h + 1) * HEAD_PAD]

    def load_v(c0, h):
        ps = slice((h // 2) * LANES, (h // 2 + 1) * LANES)
        return (va_ref if h % 2 == 0 else vb_ref)[ps, pl.ds(c0, tq)]

    def scores_of(r0, c0):
        return [_dot_nt(load_k(c0, h), load_q(r0, h)) for h in range(heads)]

    def value_update(c0, accs, alphas, ps):
        return tuple(alphas[h] * accs[h] + _dot(load_v(c0, h), ps[h]) for h in range(heads))

    def softmax_update(scores, ms, ls):
        out = []
        for h in range(heads):
            m_new = jnp.maximum(ms[h], jnp.max(scores[h], axis=0, keepdims=True))
            alpha = jnp.exp2(ms[h] - m_new)
            p = jnp.exp2(scores[h] - m_new)
            out.append((m_new, alpha * ls[h] + jnp.sum(p, axis=0, keepdims=True), alpha, p.astype(BF16)))
        return tuple(zip(*out))

    def q_tile(qi, _):
        r0 = pl.multiple_of(qi * tq, tq)

        def body(j, carry):
            ms, ls, accs, alphas, ps = carry
            scores = scores_of(r0, pl.multiple_of(j * tq, tq))
            accs = value_update(pl.multiple_of(jnp.maximum(j - 1, 0) * tq, tq), accs, alphas, ps)
            ms, ls, alphas, ps = softmax_update(scores, ms, ls)
            return ms, ls, accs, alphas, ps

        rep = lambda a: (a,) * heads
        init = (rep(jnp.full((1, tq), -jnp.inf, F32)), rep(jnp.zeros((1, tq), F32)),
                rep(jnp.zeros((LANES, tq), F32)), rep(jnp.ones((1, tq), F32)), rep(jnp.zeros((tq, tq), BF16)))
        ms, ls, accs, alphas, ps = lax.fori_loop(0, qi, body, init)
        scores = [jnp.where(causal, s_t, -jnp.inf) for s_t in scores_of(r0, r0)]
        accs = value_update(pl.multiple_of(jnp.maximum(qi - 1, 0) * tq, tq), accs, alphas, ps)
        ms, ls, alphas, ps = softmax_update(scores, ms, ls)
        accs = value_update(r0, accs, alphas, ps)
        outs = [acc * (1.0 / l) for l, acc in zip(ls, accs)]
        for p in range(pairs):
            o_t = outs[2 * p] + outs[2 * p + 1]
            o_ref[pl.ds(r0, tq), p * LANES:(p + 1) * LANES] = o_t.T.astype(o_ref.dtype)
        return 0

    lax.fori_loop(0, seq // tq, q_tile, 0)


def _attention(q, k, v_t, batch, seq, onehot=None):
    t = q.shape[0]
    moba = onehot is not None
    pairs = v_t.shape[0] // LANES
    steps = pairs // ATT_PAIRS
    blk = lambda a: pl.BlockSpec((seq, a.shape[1] // steps), lambda bi, p: (bi, p))
    vw = ATT_PAIRS * LANES
    in_specs = [blk(q), blk(k)]
    args = [q, k]
    if moba:
        in_specs.append(_const_spec((seq, LANES)))
        args.append(onehot)
    in_specs.append(pl.BlockSpec((vw, seq), lambda bi, p: (p, bi)))
    args.append(v_t)
    return pl.pallas_call(
        functools.partial(_attn_kernel, moba=moba),
        grid=(batch, steps),
        in_specs=in_specs,
        out_specs=pl.BlockSpec((seq, vw), lambda bi, p: (bi, p)),
        out_shape=jax.ShapeDtypeStruct((t, pairs * LANES), BF16),
        scratch_shapes=[pltpu.VMEM((vw, seq), BF16), pltpu.VMEM((vw, seq), BF16)],
        compiler_params=_params(("parallel", "parallel")),
        name="moba_attn" if moba else "mla_attn",
    )(*args)


OUT_PROJ_TM = 512


def _out_proj_kernel(*refs, n_act):
    acts = refs[:n_act]
    ws = refs[n_act:2 * n_act]
    x_ref, g_ref, b_ref, o_ref = refs[2 * n_act:]
    m = _dot(acts[0][...], ws[0][...])
    for a_ref, w_ref in zip(acts[1:], ws[1:]):
        m = m + _dot(a_ref[...], w_ref[...])
    o_ref[...] = _layer_norm(DEEPNORM_ALPHA * x_ref[...] + m, g_ref[...], b_ref[...])


def _out_proj_ln(acts, ws, x2, g, b):
    t = x2.shape[0]
    tm = OUT_PROJ_TM
    row = lambda w: pl.BlockSpec((tm, w), lambda i: (i, 0))
    return pl.pallas_call(
        functools.partial(_out_proj_kernel, n_act=len(acts)),
        grid=(t // tm,),
        in_specs=[row(a.shape[1]) for a in acts] + [_const_spec(w.shape) for w in ws]
        + [row(D_MODEL), _const_spec(g.shape), _const_spec(b.shape)],
        out_specs=row(D_MODEL),
        out_shape=jax.ShapeDtypeStruct((t, D_MODEL), F32),
        compiler_params=_params(("parallel",)),
        name="out_proj_ln",
    )(*acts, *ws, x2, g, b)


FFN_TM = 256
FFN_CHUNK = 256


def _ffn_kernel(x_ref, p_ref, wup_ref, cw_ref, cb_ref, wdn_ref, g_ref, b_ref, wg_ref, wp_ref, o_ref, gbuf_ref,
                *, tiles_per_seq):
    tm = FFN_TM
    hist = SUBLANES

    @pl.when(pl.program_id(0) % tiles_per_seq == 0)
    def _():
        gbuf_ref[0:hist, :] = jnp.zeros((hist, D_FF), F32)

    x = x_ref[...]
    xb = x.astype(BF16)
    f = jnp.zeros((tm, D_MODEL), F32)
    for c in range(D_FF // FFN_CHUNK):
        cs = slice(c * FFN_CHUNK, (c + 1) * FFN_CHUNK)
        gate = _dot(xb, wup_ref[:, cs])
        up = _dot(xb, wup_ref[:, D_FF + c * FFN_CHUNK:D_FF + (c + 1) * FFN_CHUNK])
        gbuf_ref[hist:hist + tm, cs] = gate
        conv = cw_ref[FFN_CONV - 1:FFN_CONV, cs] * gate + cb_ref[:, cs]
        for k in range(FFN_CONV - 1):
            d = FFN_CONV - 1 - k
            conv = conv + cw_ref[k:k + 1, cs] * gbuf_ref[hist - d:hist - d + tm, cs]
        hid = 0.5 * conv * (1.0 + lax.erf(conv * INV_SQRT2)) * up
        f = f + _dot(hid.astype(BF16), wdn_ref[cs, :])
    gbuf_ref[0:hist, :] = gbuf_ref[tm:tm + hist, :]
    y = _layer_norm(DEEPNORM_ALPHA * x + f, g_ref[...], b_ref[...])
    gate = jax.nn.sigmoid(_dot(y.astype(BF16), wg_ref[...]))
    o_ref[...] = y + gate * _dot(p_ref[...].astype(BF16), wp_ref[...])


def _ffn_ln_ple(x2, p2, w_up, conv_w, conv_b, w_down, g, b, w_gate, w_proj, seq):
    t = x2.shape[0]
    tm = FFN_TM
    row = lambda w: pl.BlockSpec((tm, w), lambda i: (i, 0))
    consts = [w_up, conv_w, conv_b, w_down, g, b, w_gate, w_proj]
    return pl.pallas_call(
        functools.partial(_ffn_kernel, tiles_per_seq=seq // tm),
        grid=(t // tm,),
        in_specs=[row(D_MODEL), row(PLE_DIM)] + [_const_spec(w.shape) for w in consts],
        out_specs=row(D_MODEL),
        out_shape=jax.ShapeDtypeStruct((t, D_MODEL), F32),
        scratch_shapes=[pltpu.VMEM((tm + SUBLANES, D_FF), F32)],
        compiler_params=_params(("arbitrary",)),
        name="ffn_ln_ple",
    )(x2, p2, *consts)


SSD_PROJ_TM = 256
SSD_PROJ_CHUNK = 512
SSD_BC = SSD_GROUPS * SSD_STATE


def _ssd_proj_kernel(x_ref, win_ref, cw_ref, cb_ref, dtb_ref, z_ref, xs_ref, b_ref, c_ref, dt_ref, cbuf_ref,
                     *, tiles_per_seq):
    tm = SSD_PROJ_TM
    hist = SUBLANES
    ck = SSD_PROJ_CHUNK

    @pl.when(pl.program_id(0) % tiles_per_seq == 0)
    def _():
        cbuf_ref[0:hist, :] = jnp.zeros((hist, SSD_CONV_DIM), F32)

    xb = x_ref[...].astype(BF16)
    for c in range(SSD_INNER // ck):
        z_ref[:, c * ck:(c + 1) * ck] = _dot(xb, win_ref[:, c * ck:(c + 1) * ck])
    for c in range(SSD_CONV_DIM // ck):
        cs = slice(c * ck, (c + 1) * ck)
        h = _dot(xb, win_ref[:, SSD_INNER + c * ck:SSD_INNER + (c + 1) * ck])
        cbuf_ref[hist:hist + tm, cs] = h
        conv = cw_ref[SSD_CONV - 1:SSD_CONV, cs] * h + cb_ref[:, cs]
        for k in range(SSD_CONV - 1):
            d = SSD_CONV - 1 - k
            conv = conv + cw_ref[k:k + 1, cs] * cbuf_ref[hist - d:hist - d + tm, cs]
        act = conv * jax.nn.sigmoid(conv)
        if c < SSD_INNER // ck:
            xs_ref[:, cs] = act
        elif c == SSD_INNER // ck:
            b_ref[...] = act
        else:
            c_ref[...] = act
    cbuf_ref[0:hist, :] = cbuf_ref[tm:tm + hist, :]
    hd = _dot(xb, win_ref[:, SSD_INNER + SSD_CONV_DIM:]) + dtb_ref[...]
    dt_ref[...] = jnp.maximum(hd, 0.0) + jnp.log1p(jnp.exp(-jnp.abs(hd)))


def _ssd_proj(x2, w_in, conv_w, conv_b, dt_bias, seq):
    t = x2.shape[0]
    tm = SSD_PROJ_TM
    assert SSD_BC == SSD_PROJ_CHUNK
    row = lambda w: pl.BlockSpec((tm, w), lambda i: (i, 0))
    f32 = lambda w: jax.ShapeDtypeStruct((t, w), F32)
    consts = [w_in, conv_w, conv_b, dt_bias]
    return pl.pallas_call(
        functools.partial(_ssd_proj_kernel, tiles_per_seq=seq // tm),
        grid=(t // tm,),
        in_specs=[row(D_MODEL)] + [_const_spec(w.shape) for w in consts],
        out_specs=[row(SSD_INNER), row(SSD_INNER), row(SSD_BC), row(SSD_BC), row(LANES)],
        out_shape=[f32(SSD_INNER), f32(SSD_INNER), f32(SSD_BC), f32(SSD_BC), f32(LANES)],
        scratch_shapes=[pltpu.VMEM((tm + SUBLANES, SSD_CONV_DIM), F32)],
        compiler_params=_params(("arbitrary",)),
        name="ssd_proj",
    )(x2, *consts)


def _ssd_scan_kernel(xs_ref, z_ref, b_ref, c_ref, dt_ref, a_ref, d_ref, nw_ref, tri_ref, o_ref, state_ref):
    L = SSD_CHUNK
    hg = SSD_HEADS // SSD_GROUPS
    gw = hg * SSD_HEAD_DIM

    @pl.when(pl.program_id(1) == 0)
    def _():
        state_ref[...] = jnp.zeros_like(state_ref)

    lane = lax.broadcasted_iota(jnp.int32, (1, LANES), 1)
    low = lane < LANES // 2
    row = lax.broadcasted_iota(jnp.int32, (L, L), 0)
    col = lax.broadcasted_iota(jnp.int32, (L, L), 1)
    causal = col <= row

    dt = dt_ref[...]
    la = jnp.dot(tri_ref[...], dt * a_ref[...], preferred_element_type=F32,
                 precision=lax.Precision.HIGHEST)
    la_t = la.T

    for g in range(SSD_GROUPS):
        bg = b_ref[:, g * SSD_STATE:(g + 1) * SSD_STATE]
        cg = c_ref[:, g * SSD_STATE:(g + 1) * SSD_STATE].astype(BF16)
        cb = _dot_nt(cg, bg.astype(BF16))
        bg_t = bg.T.astype(BF16)
        y_parts = []
        for pp in range(hg // 2):
            h0 = g * hg + 2 * pp
            ps = slice(h0 * SSD_HEAD_DIM, (h0 + 2) * SSD_HEAD_DIM)
            xs = xs_ref[:, ps]
            dt_pair = jnp.where(low, dt[:, h0:h0 + 1], dt[:, h0 + 1:h0 + 2])
            la_pair = jnp.where(low, la[:, h0:h0 + 1], la[:, h0 + 1:h0 + 2])
            xdt = xs * dt_pair
            xdt_b = xdt.astype(BF16)
            y = jnp.zeros((L, LANES), F32)
            for hh in range(2):
                seg = la[:, h0 + hh:h0 + hh + 1] - la_t[h0 + hh:h0 + hh + 1, :]
                w = (cb * jnp.exp(jnp.where(causal, seg, -jnp.inf))).astype(BF16)
                keep = low if hh == 0 else jnp.logical_not(low)
                y = y + _dot(w, jnp.where(keep, xdt_b, jnp.zeros_like(xdt_b)))
            st = state_ref[:, ps]
            y = y + _dot(cg, st.astype(BF16)) * jnp.exp(la_pair)
            la_end = la_pair[L - 1:L, :]
            to_end = jnp.exp(la_end - la_pair)
            state_ref[:, ps] = st * jnp.exp(la_end) + _dot(bg_t, (xdt * to_end).astype(BF16))
            y = y + d_ref[:, ps] * xs
            zz = z_ref[:, ps]
            y_parts.append(y * (zz * jax.nn.sigmoid(zz)))
        yg = jnp.concatenate(y_parts, axis=1)
        ms = jnp.mean(jnp.square(yg), axis=-1, keepdims=True)
        gs = slice(g * gw, (g + 1) * gw)
        o_ref[:, gs] = (yg * lax.rsqrt(ms + RMS_EPS) * nw_ref[:, gs]).astype(o_ref.dtype)


def _ssd_scan(xs, z, bm, cm, dt, a_row, d_row, norm_w, batch, seq):
    t = xs.shape[0]
    L = SSD_CHUNK
    nc = seq // L
    assert SSD_INNER // SSD_GROUPS == SSD_BC
    row = lambda w: pl.BlockSpec((L, w), lambda bi, ci: (bi * nc + ci, 0))
    tri = jnp.tril(jnp.ones((L, L), F32))
    consts = [a_row, d_row, norm_w, tri]
    return pl.pallas_call(
        _ssd_scan_kernel,
        grid=(batch, nc),
        in_specs=[row(SSD_INNER), row(SSD_INNER), row(SSD_BC), row(SSD_BC), row(LANES)]
        + [_const_spec(w.shape) for w in consts],
        out_specs=row(SSD_INNER),
        out_shape=jax.ShapeDtypeStruct((t, SSD_INNER), BF16),
        scratch_shapes=[pltpu.VMEM((SSD_STATE, SSD_INNER), F32)],
        compiler_params=_params(("parallel", "arbitrary")),
        name="ssd_scan",
    )(xs, z, bm, cm, dt, *consts)


def _pad_cols(w, n):
    return jnp.concatenate([w, jnp.zeros((w.shape[0], n), w.dtype)], axis=1)


def _attention_layer(x2, tabs, batch, seq, w_in, q_norm, w_uq, kv_norm, w_ukv, w_out):
    o1 = MLA_Q_LORA + MLA_KV_LORA
    zc = lambda n: jnp.zeros((D_MODEL, n), w_in.dtype)
    o2 = o1 + MLA_ROPE + 2 * MOBA_WIDTH
    w_in2 = jnp.concatenate([w_in[:, :o1], zc(MLA_NOPE), w_in[:, o1:o1 + MLA_ROPE],
                             zc(LANES - MLA_NOPE - MLA_ROPE), w_in[:, o1 + MLA_ROPE:o2]], axis=1).astype(BF16)
    w_vb_t = w_in[:, o2:].T.astype(BF16)
    dqk = MLA_NOPE + MLA_ROPE
    w_uq2 = jnp.pad(w_uq.reshape(MLA_Q_LORA, MLA_HEADS, dqk), ((0, 0), (0, 0), (0, HEAD_PAD - dqk)))
    w_uq2 = w_uq2.reshape(MLA_Q_LORA, MLA_HEADS * HEAD_PAD).astype(BF16)
    w_kv3 = w_ukv.reshape(MLA_KV_LORA, MLA_HEADS, MLA_NOPE + MLA_V)
    w_k2 = jnp.pad(w_kv3[:, :, :MLA_NOPE], ((0, 0), (0, 0), (0, HEAD_PAD - MLA_NOPE)))
    w_k2 = w_k2.reshape(MLA_KV_LORA, MLA_HEADS * HEAD_PAD).astype(BF16)
    w_v_t = w_kv3[:, :, MLA_NOPE:].reshape(MLA_KV_LORA, MLA_HEADS * MLA_V).T.astype(BF16)

    qm, km, vm, qb, kb, vb, kmean = _att_proj(x2, tabs, w_in2, w_uq2, w_k2, w_v_t, w_vb_t,
                                              q_norm.reshape(1, -1), kv_norm.reshape(1, -1))
    o_mla = _attention(qm, km, vm, batch, seq)
    qa = _moba_select(qb, kmean, batch, seq)
    pos = jnp.arange(seq)
    onehot = (pos[:, None] // MOBA_BLOCK == jnp.arange(LANES)[None, :]).astype(BF16)
    o_moba = _attention(qa, kb, vb, batch, seq, onehot=onehot)
    n_mla = MLA_HEADS * MLA_V
    w_o = w_out.astype(BF16)
    return [o_mla, o_moba], [w_o[:n_mla], w_o[n_mla:]]


def _ssd_layer(x2, batch, seq, w_in, conv_w, conv_b, dt_bias, a_log, d_skip, norm_w, w_out):
    w_in2 = _pad_cols(w_in, LANES - SSD_HEADS).astype(BF16)
    dtb = _pad_cols(dt_bias.reshape(1, -1), LANES - SSD_HEADS)
    z, xs, bm, cm, dt = _ssd_proj(x2, w_in2, conv_w, conv_b.reshape(1, -1), dtb, seq)
    a_row = _pad_cols((-jnp.exp(a_log.astype(F32))).reshape(1, -1), LANES - SSD_HEADS)
    d_row = jnp.repeat(d_skip.astype(F32), SSD_HEAD_DIM).reshape(1, -1)
    y = _ssd_scan(xs, z, bm, cm, dt, a_row, d_row, norm_w.reshape(1, -1), batch, seq)
    return [y], [w_out.astype(BF16)]


def kernel(x, p, positions, att_w_in, mla_q_norm, mla_w_uq, mla_kv_norm, mla_w_ukv, att_w_out, ssd_w_in, ssd_conv_w, ssd_conv_b, ssd_dt_bias, ssd_a_log, ssd_d, ssd_norm, ssd_w_out, ln_mix_g, ln_mix_b, ffn_w_up, ffn_conv_w, ffn_conv_b, ffn_w_down, ln_ffn_g, ln_ffn_b, ple_w_gate, ple_w_proj):
    batch, seq, _ = x.shape
    depth = p.shape[0]
    t = batch * seq
    x2 = x.reshape(t, D_MODEL)
    tabs = _rope_tables(positions)
    for i in range(depth):
        j = i // 2
        if i % 2 == 0:
            acts, ws = _attention_layer(x2, tabs, batch, seq, att_w_in[j], mla_q_norm[j], mla_w_uq[j],
                                        mla_kv_norm[j], mla_w_ukv[j], att_w_out[j])
        else:
            acts, ws = _ssd_layer(x2, batch, seq, ssd_w_in[j], ssd_conv_w[j], ssd_conv_b[j], ssd_dt_bias[j],
                                  ssd_a_log[j], ssd_d[j], ssd_norm[j], ssd_w_out[j])
        x2 = _out_proj_ln(acts, ws, x2, ln_mix_g[i].reshape(1, -1), ln_mix_b[i].reshape(1, -1))
        x2 = _ffn_ln_ple(x2, p[i].reshape(t, PLE_DIM), ffn_w_up[i].astype(BF16), ffn_conv_w[i],
                         ffn_conv_b[i].reshape(1, -1), ffn_w_down[i].astype(BF16),
                         ln_ffn_g[i].reshape(1, -1), ln_ffn_b[i].reshape(1, -1),
                         ple_w_gate[i].astype(BF16), ple_w_proj[i].astype(BF16), seq)
    return x2.reshape(batch, seq, D_MODEL)
```

```python
import functools
import math

import jax
import jax.numpy as jnp
from jax import lax
from jax.experimental import pallas as pl
from jax.experimental.pallas import tpu as pltpu

D_MODEL = 1024
PLE_DIM = 256
ROPE_THETA = 500000.0
MLA_HEADS = 8
MLA_Q_LORA = 256
MLA_KV_LORA = 128
MLA_NOPE = 64
MLA_ROPE = 32
MLA_V = 64
MOBA_HEADS = 8
MOBA_HEAD_DIM = 64
MOBA_ROT = MOBA_HEAD_DIM // 4
MOBA_BLOCK = 256
MOBA_TOPK = 3
MOBA_WIDTH = MOBA_HEADS * MOBA_HEAD_DIM
SSD_INNER = 2 * D_MODEL
SSD_HEAD_DIM = 64
SSD_HEADS = SSD_INNER // SSD_HEAD_DIM
SSD_GROUPS = 4
SSD_STATE = 128
SSD_CONV = 4
SSD_CHUNK = 128
SSD_CONV_DIM = SSD_INNER + 2 * SSD_GROUPS * SSD_STATE
D_FF = 2816
FFN_CONV = 3
LN_EPS = 1e-5
RMS_EPS = 1e-6
DEPTH = 2
DEEPNORM_ALPHA = (2 * DEPTH) ** 0.25

LANES = 128
SUBLANES = 8
HEAD_PAD = LANES
VMEM_LIMIT = 56 * 1024 * 1024
NEG_BIG = -1e30
INV_SQRT2 = 0.7071067811865476
LOG2_E = 1.4426950408889634

F32 = jnp.float32
BF16 = jnp.bfloat16


def _dot(a, b):
    return jnp.dot(a, b, preferred_element_type=F32)


def _dot_nt(a, b):
    return lax.dot_general(a, b, (((1,), (1,)), ((), ())), preferred_element_type=F32)


def _const_spec(shape):
    return pl.BlockSpec(shape, lambda *_: (0,) * len(shape), pipeline_mode=pl.Buffered(1))


def _params(semantics):
    return pltpu.CompilerParams(dimension_semantics=semantics, vmem_limit_bytes=VMEM_LIMIT)


def _rope_table_kernel(pos_ref, f_ref, m1_ref, m2_ref, c_ref, s1_ref, s2_ref):
    pos = pos_ref[...].astype(F32)
    for r in range(2):
        ang = pos * f_ref[r:r + 1, :]
        s = jnp.sin(ang)
        c_ref[r] = jnp.cos(ang)
        s1_ref[r] = s * m1_ref[r:r + 1, :]
        s2_ref[r] = s * m2_ref[r:r + 1, :]


def _rope_tables(positions):
    t = positions.size
    inv_m = ROPE_THETA ** (-jnp.arange(0, MLA_ROPE, 2, dtype=F32) / MLA_ROPE)
    inv_b = ROPE_THETA ** (-jnp.arange(0, MOBA_ROT, 2, dtype=F32) / MOBA_ROT)
    hm, hb = MLA_ROPE // 2, MOBA_ROT // 2
    z = lambda n: jnp.zeros((n,), F32)
    o = lambda n: jnp.ones((n,), F32)
    f_m = jnp.concatenate([z(MLA_NOPE), inv_m, inv_m, z(LANES - MLA_NOPE - MLA_ROPE)])
    m1_m = jnp.concatenate([z(MLA_NOPE), -o(hm), z(hm), z(LANES - MLA_NOPE - MLA_ROPE)])
    m2_m = jnp.concatenate([z(MLA_NOPE), z(hm), o(hm), z(LANES - MLA_NOPE - MLA_ROPE)])
    rest = MOBA_HEAD_DIM - MOBA_ROT
    f_b = jnp.tile(jnp.concatenate([inv_b, inv_b, z(rest)]), LANES // MOBA_HEAD_DIM)
    m1_b = jnp.tile(jnp.concatenate([-o(hb), z(hb), z(rest)]), LANES // MOBA_HEAD_DIM)
    m2_b = jnp.tile(jnp.concatenate([z(hb), o(hb), z(rest)]), LANES // MOBA_HEAD_DIM)
    f = jnp.stack([f_m, f_b])
    m1 = jnp.stack([m1_m, m1_b])
    m2 = jnp.stack([m2_m, m2_b])
    tm = min(t, 2048)
    out = jax.ShapeDtypeStruct((2, t, LANES), F32)
    row_spec = _const_spec((2, LANES))
    tab_spec = pl.BlockSpec((2, tm, LANES), lambda i: (0, i, 0))
    return pl.pallas_call(
        _rope_table_kernel,
        grid=(t // tm,),
        in_specs=[pl.BlockSpec((tm, 1), lambda i: (i, 0)), row_spec, row_spec, row_spec],
        out_specs=[tab_spec, tab_spec, tab_spec],
        out_shape=[out, out, out],
        compiler_params=_params(("parallel",)),
        name="rope_tables",
    )(positions.reshape(t, 1), f, m1, m2)


def _rope(t, c, s1, s2, half):
    w = t.shape[-1]
    return t * c + pltpu.roll(t, w - half, 1) * s1 + pltpu.roll(t, half, 1) * s2


def _rms(x, g):
    ms = jnp.mean(jnp.square(x), axis=-1, keepdims=True)
    return x * lax.rsqrt(ms + RMS_EPS) * g


def _layer_norm(x, g, b):
    mu = jnp.mean(x, axis=-1, keepdims=True)
    xc = x - mu
    var = jnp.mean(jnp.square(xc), axis=-1, keepdims=True)
    return xc * lax.rsqrt(var + LN_EPS) * g + b


ATT_PROJ_TM = 256


def _att_proj_kernel(x_ref, win_ref, wuq_ref, wk_ref, wv_ref, wvb_ref, qn_ref, kvn_ref, c_ref, s1_ref, s2_ref,
                     qm_ref, km_ref, vm_ref, qb_ref, kb_ref, vb_ref, kmean_ref):
    xb = x_ref[...].astype(BF16)
    cm, s1m, s2m = c_ref[0], s1_ref[0], s2_ref[0]
    cb, s1b, s2b = c_ref[1], s1_ref[1], s2_ref[1]
    hm, hb = MLA_ROPE // 2, MOBA_ROT // 2
    mla_scale = (MLA_NOPE + MLA_ROPE) ** -0.5 * LOG2_E
    moba_scale = MOBA_HEAD_DIM ** -0.5 * LOG2_E

    h_lat = _dot(xb, win_ref[:, 0:512])
    c_q = h_lat[:, 0:MLA_Q_LORA]
    c_kv = h_lat[:, MLA_Q_LORA:MLA_Q_LORA + MLA_KV_LORA]
    hq = _dot(xb, win_ref[:, 512:1024])
    hk = _dot(xb, win_ref[:, 1024:1536])
    vb_ref[...] = _dot_nt(wvb_ref[...], xb).astype(BF16)
    k_rope = _rope(h_lat[:, 384:512], cm, s1m, s2m, hm)

    q = _dot(_rms(c_q, qn_ref[...]).astype(BF16), wuq_ref[...])
    for h in range(MLA_HEADS):
        sl = slice(h * HEAD_PAD, (h + 1) * HEAD_PAD)
        qm_ref[:, sl] = (_rope(q[:, sl], cm, s1m, s2m, hm) * mla_scale).astype(BF16)

    ckv = _rms(c_kv, kvn_ref[...]).astype(BF16)
    k = _dot(ckv, wk_ref[...])
    for h in range(MLA_HEADS):
        sl = slice(h * HEAD_PAD, (h + 1) * HEAD_PAD)
        km_ref[:, sl] = (k[:, sl] + k_rope).astype(BF16)
    vm_ref[...] = _dot_nt(wv_ref[...], ckv).astype(BF16)

    for g in range(MOBA_WIDTH // LANES):
        sl = slice(g * LANES, (g + 1) * LANES)
        qb_ref[:, sl] = (_rope(hq[:, sl], cb, s1b, s2b, hb) * moba_scale).astype(BF16)
        kr = _rope(hk[:, sl], cb, s1b, s2b, hb)
        kb_ref[:, sl] = kr.astype(BF16)
        for r in range(ATT_PROJ_TM // MOBA_BLOCK):
            rows = slice(r * MOBA_BLOCK, (r + 1) * MOBA_BLOCK)
            kmean_ref[r, :, sl] = jnp.mean(kr[rows], axis=0, keepdims=True)


def _att_proj(x2, tabs, w_in, w_uq, w_k, w_v_t, w_vb_t, q_norm, kv_norm):
    t = x2.shape[0]
    tm = ATT_PROJ_TM
    c, s1, s2 = tabs
    row = lambda w: pl.BlockSpec((tm, w), lambda i: (i, 0))
    col = lambda h: pl.BlockSpec((h, tm), lambda i: (0, i))
    tab = pl.BlockSpec((2, tm, LANES), lambda i: (0, i, 0))
    bf = lambda w: jax.ShapeDtypeStruct((t, w), BF16)
    bf_t = lambda h: jax.ShapeDtypeStruct((h, t), BF16)
    nblk = tm // MOBA_BLOCK
    consts = [w_in, w_uq, w_k, w_v_t, w_vb_t, q_norm, kv_norm]
    return pl.pallas_call(
        _att_proj_kernel,
        grid=(t // tm,),
        in_specs=[row(D_MODEL)] + [_const_spec(w.shape) for w in consts] + [tab, tab, tab],
        out_specs=[row(1024), row(1024), col(512), row(512), row(512), col(512),
                   pl.BlockSpec((nblk, 1, MOBA_WIDTH), lambda i: (i, 0, 0))],
        out_shape=[bf(1024), bf(1024), bf_t(512), bf(512), bf(512), bf_t(512),
                   jax.ShapeDtypeStruct((t // MOBA_BLOCK, 1, MOBA_WIDTH), F32)],
        compiler_params=_params(("parallel",)),
        name="att_proj",
    )(x2, *consts, c, s1, s2)


def _moba_select_kernel(q_ref, kab_ref, r_ref, lc_ref, qa_ref, *, nb):
    own = pl.program_id(1)
    lane = lax.broadcasted_iota(jnp.int32, (1, LANES), 1)
    pair_valid = jnp.where(lc_ref[0:1, :] < own.astype(F32), 1.0, 0.0)
    tie_first = lc_ref[1:2, :]
    low = lane < (LANES // 2)
    lt_own = lane < own
    rest_pen = jnp.where((lane > own) & (lane < nb), NEG_BIG, 0.0)
    gates = []
    for h in range(MOBA_HEADS):
        qp = q_ref[:, (h // 2) * LANES:(h // 2 + 1) * LANES]
        qm = jnp.where(low if h % 2 == 0 else jnp.logical_not(low), qp, jnp.zeros_like(qp))
        qa_ref[:, 2 * h * LANES:(2 * h + 1) * LANES] = qm
        gates.append(_dot(qm, kab_ref[0, h]))
    beats = []
    for ab in gates:
        a, b = ab[:, 0:LANES], ab[:, LANES:2 * LANES]
        beats.append((jnp.where(b > a, 1.0, jnp.where(b == a, tie_first, 0.0)) * pair_valid).astype(BF16))
    counts = [_dot(bt, r_ref[...]) for bt in beats]
    for h, cnt in enumerate(counts):
        pen = jnp.where(lt_own, jnp.where(cnt < MOBA_TOPK, 0.0, NEG_BIG), rest_pen)
        qa_ref[:, (2 * h + 1) * LANES:(2 * h + 2) * LANES] = pen.astype(BF16)


def _moba_select(qb, kmean, batch, seq):
    t = qb.shape[0]
    nb = seq // MOBA_BLOCK
    assert nb * nb <= LANES
    km = kmean.reshape(batch, nb, MOBA_HEADS, MOBA_HEAD_DIM).transpose(0, 2, 3, 1)
    a = jnp.repeat(km, nb, axis=-1)
    b = jnp.tile(km, (1, 1, 1, nb))
    zc = jnp.zeros(km.shape[:3] + (LANES - nb * nb,), F32)
    ab = jnp.concatenate([a, zc, b, zc], axis=-1)
    zr = jnp.zeros_like(ab)
    odd = (jnp.arange(MOBA_HEADS) % 2 == 1)[None, :, None, None]
    kab = jnp.where(odd, jnp.concatenate([zr, ab], axis=2), jnp.concatenate([ab, zr], axis=2)).astype(BF16)
    cidx = jnp.arange(LANES)
    rmat = ((cidx[:, None] // nb == cidx[None, :]) & (cidx[:, None] < nb * nb)).astype(BF16)
    used = cidx < nb * nb
    lane_consts = jnp.zeros((SUBLANES, LANES), F32)
    lane_consts = lane_consts.at[0].set(jnp.where(used, cidx % nb, nb).astype(F32))
    lane_consts = lane_consts.at[1].set((used & (cidx % nb < cidx // nb)).astype(F32))
    tq = MOBA_BLOCK
    return pl.pallas_call(
        functools.partial(_moba_select_kernel, nb=nb),
        grid=(batch, seq // tq),
        in_specs=[pl.BlockSpec((tq, MOBA_WIDTH), lambda bi, qi: (bi * (seq // tq) + qi, 0)),
                  pl.BlockSpec((1, MOBA_HEADS, LANES, 2 * LANES), lambda bi, qi: (bi, 0, 0, 0)),
                  _const_spec((LANES, LANES)), _const_spec((SUBLANES, LANES))],
        out_specs=pl.BlockSpec((tq, MOBA_HEADS * 2 * LANES), lambda bi, qi: (bi * (seq // tq) + qi, 0)),
        out_shape=jax.ShapeDtypeStruct((t, MOBA_HEADS * 2 * LANES), BF16),
        compiler_params=_params(("parallel", "parallel")),
        name="moba_select",
    )(qb, kab, rmat, lane_consts)


ATT_TILE = 256
ATT_PAIRS = 2


def _attn_kernel(*refs, moba):
    if moba:
        q_ref, k_ref, oh_ref, v_ref, o_ref, va_ref, vb_ref = refs
    else:
        q_ref, k_ref, v_ref, o_ref, va_ref, vb_ref = refs
        oh_ref = None
    seq = q_ref.shape[0]
    tq = ATT_TILE
    pairs = v_ref.shape[0] // LANES
    heads = 2 * pairs
    qw = q_ref.shape[1] // heads
    sub = lax.broadcasted_iota(jnp.int32, (LANES, 1), 0)
    for p in range(pairs):
        ps = slice(p * LANES, (p + 1) * LANES)
        v = v_ref[ps, :]
        va_ref[ps, :] = jnp.where(sub < LANES // 2, v, jnp.zeros_like(v))
        vb_ref[ps, :] = jnp.where(sub >= LANES // 2, v, jnp.zeros_like(v))
    key = lax.broadcasted_iota(jnp.int32, (tq, tq), 0)
    qry = lax.broadcasted_iota(jnp.int32, (tq, tq), 1)
    causal = key <= qry

    def load_q(r0, h):
        return q_ref[pl.ds(r0, tq), h * qw:(h + 1) * qw]

    def load_k(c0, h):
        if moba:
            ps = slice((h // 2) * LANES, (h // 2 + 1) * LANES)
            return jnp.concatenate([k_ref[pl.ds(c0, tq), ps], oh_ref[pl.ds(c0, tq), :]], axis=1)
        return k_ref[pl.ds(c0, tq), h * HEAD_PAD:(h + 1) * HEAD_PAD]

    def load_v(c0, h):
        ps = slice((h // 2) * LANES, (h // 2 + 1) * LANES)
        return (va_ref if h % 2 == 0 else vb_ref)[ps, pl.ds(c0, tq)]

    def scores_of(r0, c0):
        return [_dot_nt(load_k(c0, h), load_q(r0, h)) for h in range(heads)]

    def value_update(c0, accs, alphas, ps):
        return tuple(alphas[h] * accs[h] + _dot(load_v(c0, h), ps[h]) for h in range(heads))

    def softmax_update(scores, ms, ls):
        out = []
        for h in range(heads):
            m_new = jnp.maximum(ms[h], jnp.max(scores[h], axis=0, keepdims=True))
            alpha = jnp.exp2(ms[h] - m_new)
            p = jnp.exp2(scores[h] - m_new)
            out.append((m_new, alpha * ls[h] + jnp.sum(p, axis=0, keepdims=True), alpha, p.astype(BF16)))
        return tuple(zip(*out))

    def q_tile(qi, _):
        r0 = pl.multiple_of(qi * tq, tq)

        def body(j, carry):
            ms, ls, accs, alphas, ps = carry
            scores = scores_of(r0, pl.multiple_of(j * tq, tq))
            accs = value_update(pl.multiple_of(jnp.maximum(j - 1, 0) * tq, tq), accs, alphas, ps)
            ms, ls, alphas, ps = softmax_update(scores, ms, ls)
            return ms, ls, accs, alphas, ps

        rep = lambda a: (a,) * heads
        init = (rep(jnp.full((1, tq), -jnp.inf, F32)), rep(jnp.zeros((1, tq), F32)),
                rep(jnp.zeros((LANES, tq), F32)), rep(jnp.ones((1, tq), F32)), rep(jnp.zeros((tq, tq), BF16)))
        ms, ls, accs, alphas, ps = lax.fori_loop(0, qi, body, init)
        scores = [jnp.where(causal, s_t, -jnp.inf) for s_t in scores_of(r0, r0)]
        accs = value_update(pl.multiple_of(jnp.maximum(qi - 1, 0) * tq, tq), accs, alphas, ps)
        ms, ls, alphas, ps = softmax_update(scores, ms, ls)
        accs = value_update(r0, accs, alphas, ps)
        outs = [acc * (1.0 / l) for l, acc in zip(ls, accs)]
        for p in range(pairs):
            o_t = outs[2 * p] + outs[2 * p + 1]
            o_ref[pl.ds(r0, tq), p * LANES:(p + 1) * LANES] = o_t.T.astype(o_ref.dtype)
        return 0

    lax.fori_loop(0, seq // tq, q_tile, 0)


def _attention(q, k, v_t, batch, seq, onehot=None):
    t = q.shape[0]
    moba = onehot is not None
    pairs = v_t.shape[0] // LANES
    steps = pairs // ATT_PAIRS
    blk = lambda a: pl.BlockSpec((seq, a.shape[1] // steps), lambda bi, p: (bi, p))
    vw = ATT_PAIRS * LANES
    in_specs = [blk(q), blk(k)]
    args = [q, k]
    if moba:
        in_specs.append(_const_spec((seq, LANES)))
        args.append(onehot)
    in_specs.append(pl.BlockSpec((vw, seq), lambda bi, p: (p, bi)))
    args.append(v_t)
    return pl.pallas_call(
        functools.partial(_attn_kernel, moba=moba),
        grid=(batch, steps),
        in_specs=in_specs,
        out_specs=pl.BlockSpec((seq, vw), lambda bi, p: (bi, p)),
        out_shape=jax.ShapeDtypeStruct((t, pairs * LANES), BF16),
        scratch_shapes=[pltpu.VMEM((vw, seq), BF16), pltpu.VMEM((vw, seq), BF16)],
        compiler_params=_params(("parallel", "parallel")),
        name="moba_attn" if moba else "mla_attn",
    )(*args)


OUT_PROJ_TM = 512
OUT_PROJ_SPLIT = 4


def _out_proj_kernel(*refs, n_act):
    acts = refs[:n_act]
    ws = refs[n_act:2 * n_act]
    x_ref, g_ref, b_ref, o_ref = refs[2 * n_act:]
    sub = OUT_PROJ_TM // OUT_PROJ_SPLIT
    mixes = []
    for r in range(OUT_PROJ_SPLIT):
        rows = slice(r * sub, (r + 1) * sub)
        m = _dot(acts[0][rows, :], ws[0][...])
        for a_ref, w_ref in zip(acts[1:], ws[1:]):
            m = m + _dot(a_ref[rows, :], w_ref[...])
        mixes.append(m)
    for r in range(OUT_PROJ_SPLIT):
        rows = slice(r * sub, (r + 1) * sub)
        o_ref[rows, :] = _layer_norm(DEEPNORM_ALPHA * x_ref[rows, :] + mixes[r], g_ref[...], b_ref[...])


def _out_proj_ln(acts, ws, x2, g, b):
    t = x2.shape[0]
    tm = OUT_PROJ_TM
    row = lambda w: pl.BlockSpec((tm, w), lambda i: (i, 0))
    return pl.pallas_call(
        functools.partial(_out_proj_kernel, n_act=len(acts)),
        grid=(t // tm,),
        in_specs=[row(a.shape[1]) for a in acts] + [_const_spec(w.shape) for w in ws]
        + [row(D_MODEL), _const_spec(g.shape), _const_spec(b.shape)],
        out_specs=row(D_MODEL),
        out_shape=jax.ShapeDtypeStruct((t, D_MODEL), F32),
        compiler_params=_params(("parallel",)),
        name="out_proj_ln",
    )(*acts, *ws, x2, g, b)


FFN_TM = 256
FFN_CHUNK = 256


def _ffn_kernel(x_ref, p_ref, wup_ref, cw_ref, cb_ref, wdn_ref, g_ref, b_ref, wg_ref, wp_ref, o_ref, gbuf_ref,
                *, tiles_per_seq):
    tm = FFN_TM
    hist = SUBLANES

    @pl.when(pl.program_id(0) % tiles_per_seq == 0)
    def _():
        gbuf_ref[0:hist, :] = jnp.zeros((hist, D_FF), F32)

    x = x_ref[...]
    xb = x.astype(BF16)
    n_chunks = D_FF // FFN_CHUNK

    def up_proj(c):
        lo = c * FFN_CHUNK
        return (_dot(xb, wup_ref[:, lo:lo + FFN_CHUNK]), _dot(xb, wup_ref[:, D_FF + lo:D_FF + lo + FFN_CHUNK]))

    f = jnp.zeros((tm, D_MODEL), F32)
    nxt = up_proj(0)
    for c in range(n_chunks):
        cs = slice(c * FFN_CHUNK, (c + 1) * FFN_CHUNK)
        gate, up = nxt
        if c + 1 < n_chunks:
            nxt = up_proj(c + 1)
        gbuf_ref[hist:hist + tm, cs] = gate
        conv = cw_ref[FFN_CONV - 1:FFN_CONV, cs] * gate + cb_ref[:, cs]
        for k in range(FFN_CONV - 1):
            d = FFN_CONV - 1 - k
            conv = conv + cw_ref[k:k + 1, cs] * gbuf_ref[hist - d:hist - d + tm, cs]
        hid = 0.5 * conv * (1.0 + lax.erf(conv * INV_SQRT2)) * up
        f = f + _dot(hid.astype(BF16), wdn_ref[cs, :])
    proj = _dot(p_ref[...].astype(BF16), wp_ref[...])
    gbuf_ref[0:hist, :] = gbuf_ref[tm:tm + hist, :]
    y = _layer_norm(DEEPNORM_ALPHA * x + f, g_ref[...], b_ref[...])
    gate = jax.nn.sigmoid(_dot(y.astype(BF16), wg_ref[...]))
    o_ref[...] = y + gate * proj


def _ffn_ln_ple(x2, p2, w_up, conv_w, conv_b, w_down, g, b, w_gate, w_proj, seq):
    t = x2.shape[0]
    tm = FFN_TM
    row = lambda w: pl.BlockSpec((tm, w), lambda i: (i, 0))
    consts = [w_up, conv_w, conv_b, w_down, g, b, w_gate, w_proj]
    return pl.pallas_call(
        functools.partial(_ffn_kernel, tiles_per_seq=seq // tm),
        grid=(t // tm,),
        in_specs=[row(D_MODEL), row(PLE_DIM)] + [_const_spec(w.shape) for w in consts],
        out_specs=row(D_MODEL),
        out_shape=jax.ShapeDtypeStruct((t, D_MODEL), F32),
        scratch_shapes=[pltpu.VMEM((tm + SUBLANES, D_FF), F32)],
        compiler_params=_params(("arbitrary",)),
        name="ffn_ln_ple",
    )(x2, p2, *consts)


SSD_PROJ_TM = 256
SSD_PROJ_CHUNK = 256
SSD_BC = SSD_GROUPS * SSD_STATE


def _ssd_proj_kernel(x_ref, win_ref, cw_ref, cb_ref, dtb_ref, z_ref, xs_ref, b_ref, c_ref, dt_ref, cbuf_ref, xb_ref,
                     *, tiles_per_seq):
    tm = SSD_PROJ_TM
    hist = SUBLANES
    ck = SSD_PROJ_CHUNK

    @pl.when(pl.program_id(0) % tiles_per_seq == 0)
    def _():
        cbuf_ref[0:hist, :] = jnp.zeros((hist, SSD_CONV_DIM), F32)

    xb_ref[...] = x_ref[...].astype(BF16)
    for c in range(SSD_INNER // ck):
        z_ref[:, c * ck:(c + 1) * ck] = _dot(xb_ref[...], win_ref[:, c * ck:(c + 1) * ck]).astype(z_ref.dtype)
    for c in range(SSD_CONV_DIM // ck):
        cs = slice(c * ck, (c + 1) * ck)
        h = _dot(xb_ref[...], win_ref[:, SSD_INNER + c * ck:SSD_INNER + (c + 1) * ck])
        cbuf_ref[hist:hist + tm, cs] = h
        conv = cw_ref[SSD_CONV - 1:SSD_CONV, cs] * h + cb_ref[:, cs]
        for k in range(SSD_CONV - 1):
            d = SSD_CONV - 1 - k
            conv = conv + cw_ref[k:k + 1, cs] * cbuf_ref[hist - d:hist - d + tm, cs]
        act = (conv * jax.nn.sigmoid(conv)).astype(xs_ref.dtype)
        lo = c * ck
        if lo < SSD_INNER:
            xs_ref[:, cs] = act
        elif lo < SSD_INNER + SSD_BC:
            b_ref[:, lo - SSD_INNER:lo - SSD_INNER + ck] = act
        else:
            c_ref[:, lo - SSD_INNER - SSD_BC:lo - SSD_INNER - SSD_BC + ck] = act
    cbuf_ref[0:hist, :] = cbuf_ref[tm:tm + hist, :]
    hd = _dot(xb_ref[...], win_ref[:, SSD_INNER + SSD_CONV_DIM:]) + dtb_ref[...]
    dt_ref[...] = jnp.maximum(hd, 0.0) + jnp.log1p(jnp.exp(-jnp.abs(hd)))


def _ssd_proj(x2, w_in, conv_w, conv_b, dt_bias, seq):
    t = x2.shape[0]
    tm = SSD_PROJ_TM
    assert SSD_BC % SSD_PROJ_CHUNK == 0 and SSD_INNER % SSD_PROJ_CHUNK == 0
    row = lambda w: pl.BlockSpec((tm, w), lambda i: (i, 0))
    out = lambda w, dt: jax.ShapeDtypeStruct((t, w), dt)
    consts = [w_in, conv_w, conv_b, dt_bias]
    return pl.pallas_call(
        functools.partial(_ssd_proj_kernel, tiles_per_seq=seq // tm),
        grid=(t // tm,),
        in_specs=[row(D_MODEL)] + [_const_spec(w.shape) for w in consts],
        out_specs=[row(SSD_INNER), row(SSD_INNER), row(SSD_BC), row(SSD_BC), row(LANES)],
        out_shape=[out(SSD_INNER, BF16), out(SSD_INNER, BF16), out(SSD_BC, BF16), out(SSD_BC, BF16),
                   out(LANES, F32)],
        scratch_shapes=[pltpu.VMEM((tm + SUBLANES, SSD_CONV_DIM), F32), pltpu.VMEM((tm, D_MODEL), BF16)],
        compiler_params=_params(("arbitrary",)),
        name="ssd_proj",
    )(x2, *consts)


def _ssd_scan_kernel(xs_ref, z_ref, b_ref, c_ref, dt_ref, a_ref, d_ref, nw_ref, tri_ref, o_ref, state_ref):
    L = SSD_CHUNK
    hg = SSD_HEADS // SSD_GROUPS
    gw = hg * SSD_HEAD_DIM

    @pl.when(pl.program_id(1) == 0)
    def _():
        state_ref[...] = jnp.zeros_like(state_ref)

    lane = lax.broadcasted_iota(jnp.int32, (1, LANES), 1)
    low = lane < LANES // 2
    row = lax.broadcasted_iota(jnp.int32, (L, L), 0)
    col = lax.broadcasted_iota(jnp.int32, (L, L), 1)
    causal = col <= row

    dt = dt_ref[...]
    la = jnp.dot(tri_ref[...], dt * a_ref[...], preferred_element_type=F32,
                 precision=lax.Precision.HIGHEST)
    la_t = la.T
    la_bc = [jnp.broadcast_to(la[:, h:h + 1], (L, LANES)) for h in range(SSD_HEADS)]
    dt_bc = [jnp.broadcast_to(dt[:, h:h + 1], (L, LANES)) for h in range(SSD_HEADS)]

    for g in range(SSD_GROUPS):
        bg = b_ref[:, g * SSD_STATE:(g + 1) * SSD_STATE]
        cg = c_ref[:, g * SSD_STATE:(g + 1) * SSD_STATE]
        cb = _dot_nt(cg, bg)
        bg_t = bg.astype(F32).T.astype(BF16)
        y_parts = []
        for pp in range(hg // 2):
            h0 = g * hg + 2 * pp
            ps = slice(h0 * SSD_HEAD_DIM, (h0 + 2) * SSD_HEAD_DIM)
            xs = xs_ref[:, ps].astype(F32)
            dt_pair = jnp.where(low, dt_bc[h0], dt_bc[h0 + 1])
            la_pair = jnp.where(low, la_bc[h0], la_bc[h0 + 1])
            xdt = xs * dt_pair
            xdt_b = xdt.astype(BF16)
            y = jnp.zeros((L, LANES), F32)
            for hh in range(2):
                seg = la_bc[h0 + hh] - la_t[h0 + hh:h0 + hh + 1, :]
                w = (cb * jnp.exp(jnp.where(causal, seg, -jnp.inf))).astype(BF16)
                keep = low if hh == 0 else jnp.logical_not(low)
                y = y + _dot(w, jnp.where(keep, xdt_b, jnp.zeros_like(xdt_b)))
            st = state_ref[:, ps]
            y = y + _dot(cg, st.astype(BF16)) * jnp.exp(la_pair)
            la_end = la_pair[L - 1:L, :]
            to_end = jnp.exp(la_end - la_pair)
            state_ref[:, ps] = st * jnp.exp(la_end) + _dot(bg_t, (xdt * to_end).astype(BF16))
            y = y + d_ref[:, ps] * xs
            zz = z_ref[:, ps].astype(F32)
            y_parts.append(y * (zz * jax.nn.sigmoid(zz)))
        yg = jnp.concatenate(y_parts, axis=1)
        ms = jnp.mean(jnp.square(yg), axis=-1, keepdims=True)
        gs = slice(g * gw, (g + 1) * gw)
        o_ref[:, gs] = (yg * lax.rsqrt(ms + RMS_EPS) * nw_ref[:, gs]).astype(o_ref.dtype)


def _ssd_scan(xs, z, bm, cm, dt, a_row, d_row, norm_w, batch, seq):
    t = xs.shape[0]
    L = SSD_CHUNK
    nc = seq // L
    assert SSD_INNER // SSD_GROUPS == SSD_BC
    row = lambda w: pl.BlockSpec((L, w), lambda bi, ci: (bi * nc + ci, 0))
    tri = jnp.tril(jnp.ones((L, L), F32))
    consts = [a_row, d_row, norm_w, tri]
    return pl.pallas_call(
        _ssd_scan_kernel,
        grid=(batch, nc),
        in_specs=[row(SSD_INNER), row(SSD_INNER), row(SSD_BC), row(SSD_BC), row(LANES)]
        + [_const_spec(w.shape) for w in consts],
        out_specs=row(SSD_INNER),
        out_shape=jax.ShapeDtypeStruct((t, SSD_INNER), BF16),
        scratch_shapes=[pltpu.VMEM((SSD_STATE, SSD_INNER), F32)],
        compiler_params=_params(("parallel", "arbitrary")),
        name="ssd_scan",
    )(xs, z, bm, cm, dt, *consts)


def _pad_cols(w, n):
    return jnp.concatenate([w, jnp.zeros((w.shape[0], n), w.dtype)], axis=1)


def _attention_layer(x2, tabs, batch, seq, w_in, q_norm, w_uq, kv_norm, w_ukv, w_out):
    o1 = MLA_Q_LORA + MLA_KV_LORA
    zc = lambda n: jnp.zeros((D_MODEL, n), w_in.dtype)
    o2 = o1 + MLA_ROPE + 2 * MOBA_WIDTH
    w_in2 = jnp.concatenate([w_in[:, :o1], zc(MLA_NOPE), w_in[:, o1:o1 + MLA_ROPE],
                             zc(LANES - MLA_NOPE - MLA_ROPE), w_in[:, o1 + MLA_ROPE:o2]], axis=1).astype(BF16)
    w_vb_t = w_in[:, o2:].T.astype(BF16)
    dqk = MLA_NOPE + MLA_ROPE
    w_uq2 = jnp.pad(w_uq.reshape(MLA_Q_LORA, MLA_HEADS, dqk), ((0, 0), (0, 0), (0, HEAD_PAD - dqk)))
    w_uq2 = w_uq2.reshape(MLA_Q_LORA, MLA_HEADS * HEAD_PAD).astype(BF16)
    w_kv3 = w_ukv.reshape(MLA_KV_LORA, MLA_HEADS, MLA_NOPE + MLA_V)
    w_k2 = jnp.pad(w_kv3[:, :, :MLA_NOPE], ((0, 0), (0, 0), (0, HEAD_PAD - MLA_NOPE)))
    w_k2 = w_k2.reshape(MLA_KV_LORA, MLA_HEADS * HEAD_PAD).astype(BF16)
    w_v_t = w_kv3[:, :, MLA_NOPE:].reshape(MLA_KV_LORA, MLA_HEADS * MLA_V).T.astype(BF16)

    qm, km, vm, qb, kb, vb, kmean = _att_proj(x2, tabs, w_in2, w_uq2, w_k2, w_v_t, w_vb_t,
                                              q_norm.reshape(1, -1), kv_norm.reshape(1, -1))
    o_mla = _attention(qm, km, vm, batch, seq)
    qa = _moba_select(qb, kmean, batch, seq)
    pos = jnp.arange(seq)
    onehot = (pos[:, None] // MOBA_BLOCK == jnp.arange(LANES)[None, :]).astype(BF16)
    o_moba = _attention(qa, kb, vb, batch, seq, onehot=onehot)
    n_mla = MLA_HEADS * MLA_V
    w_o = w_out.astype(BF16)
    return [o_mla, o_moba], [w_o[:n_mla], w_o[n_mla:]]


def _ssd_layer(x2, batch, seq, w_in, conv_w, conv_b, dt_bias, a_log, d_skip, norm_w, w_out):
    w_in2 = _pad_cols(w_in, LANES - SSD_HEADS).astype(BF16)
    dtb = _pad_cols(dt_bias.reshape(1, -1), LANES - SSD_HEADS)
    z, xs, bm, cm, dt = _ssd_proj(x2, w_in2, conv_w, conv_b.reshape(1, -1), dtb, seq)
    a_row = _pad_cols((-jnp.exp(a_log.astype(F32))).reshape(1, -1), LANES - SSD_HEADS)
    d_row = jnp.repeat(d_skip.astype(F32), SSD_HEAD_DIM).reshape(1, -1)
    y = _ssd_scan(xs, z, bm, cm, dt, a_row, d_row, norm_w.reshape(1, -1), batch, seq)
    return [y], [w_out.astype(BF16)]


def kernel(x, p, positions, att_w_in, mla_q_norm, mla_w_uq, mla_kv_norm, mla_w_ukv, att_w_out, ssd_w_in, ssd_conv_w, ssd_conv_b, ssd_dt_bias, ssd_a_log, ssd_d, ssd_norm, ssd_w_out, ln_mix_g, ln_mix_b, ffn_w_up, ffn_conv_w, ffn_conv_b, ffn_w_down, ln_ffn_g, ln_ffn_b, ple_w_gate, ple_w_proj):
    batch, seq, _ = x.shape
    depth = p.shape[0]
    t = batch * seq
    x2 = x.reshape(t, D_MODEL)
    tabs = _rope_tables(positions)
    for i in range(depth):
        j = i // 2
        if i % 2 == 0:
            acts, ws = _attention_layer(x2, tabs, batch, seq, att_w_in[j], mla_q_norm[j], mla_w_uq[j],
                                        mla_kv_norm[j], mla_w_ukv[j], att_w_out[j])
        else:
            acts, ws = _ssd_layer(x2, batch, seq, ssd_w_in[j], ssd_conv_w[j], ssd_conv_b[j], ssd_dt_bias[j],
                                  ssd_a_log[j], ssd_d[j], ssd_norm[j], ssd_w_out[j])
        x2 = _out_proj_ln(acts, ws, x2, ln_mix_g[i].reshape(1, -1), ln_mix_b[i].reshape(1, -1))
        x2 = _ffn_ln_ple(x2, p[i].reshape(t, PLE_DIM), ffn_w_up[i].astype(BF16), ffn_conv_w[i],
                         ffn_conv_b[i].reshape(1, -1), ffn_w_down[i].astype(BF16),
                         ln_ffn_g[i].reshape(1, -1), ln_ffn_b[i].reshape(1, -1),
                         ple_w_gate[i].astype(BF16), ple_w_proj[i].astype(BF16), seq)
    return x2.reshape(batch, seq, D_MODEL)
```

```python
import functools
import math

import jax
import jax.numpy as jnp
from jax import lax
from jax.experimental import pallas as pl
from jax.experimental.pallas import tpu as pltpu

D_MODEL = 1024
PLE_DIM = 256
ROPE_THETA = 500000.0
MLA_HEADS = 8
MLA_Q_LORA = 256
MLA_KV_LORA = 128
MLA_NOPE = 64
MLA_ROPE = 32
MLA_V = 64
MOBA_HEADS = 8
MOBA_HEAD_DIM = 64
MOBA_ROT = MOBA_HEAD_DIM // 4
MOBA_BLOCK = 256
MOBA_TOPK = 3
MOBA_WIDTH = MOBA_HEADS * MOBA_HEAD_DIM
SSD_INNER = 2 * D_MODEL
SSD_HEAD_DIM = 64
SSD_HEADS = SSD_INNER // SSD_HEAD_DIM
SSD_GROUPS = 4
SSD_STATE = 128
SSD_CONV = 4
SSD_CHUNK = 128
SSD_CONV_DIM = SSD_INNER + 2 * SSD_GROUPS * SSD_STATE
D_FF = 2816
FFN_CONV = 3
LN_EPS = 1e-5
RMS_EPS = 1e-6
DEPTH = 2
DEEPNORM_ALPHA = (2 * DEPTH) ** 0.25

LANES = 128
SUBLANES = 8
HEAD_PAD = LANES
VMEM_LIMIT = 56 * 1024 * 1024
NEG_BIG = -1e30
INV_SQRT2 = 0.7071067811865476
LOG2_E = 1.4426950408889634

F32 = jnp.float32
BF16 = jnp.bfloat16


def _dot(a, b):
    return jnp.dot(a, b, preferred_element_type=F32)


def _dot_nt(a, b):
    return lax.dot_general(a, b, (((1,), (1,)), ((), ())), preferred_element_type=F32)


def _const_spec(shape):
    return pl.BlockSpec(shape, lambda *_: (0,) * len(shape), pipeline_mode=pl.Buffered(1))


def _params(semantics):
    return pltpu.CompilerParams(dimension_semantics=semantics, vmem_limit_bytes=VMEM_LIMIT)


def _rope_table_kernel(pos_ref, f_ref, m1_ref, m2_ref, c_ref, s1_ref, s2_ref):
    pos = pos_ref[...].astype(F32)
    for r in range(2):
        ang = pos * f_ref[r:r + 1, :]
        s = jnp.sin(ang)
        c_ref[r] = jnp.cos(ang)
        s1_ref[r] = s * m1_ref[r:r + 1, :]
        s2_ref[r] = s * m2_ref[r:r + 1, :]


def _rope_tables(positions):
    t = positions.size
    inv_m = ROPE_THETA ** (-jnp.arange(0, MLA_ROPE, 2, dtype=F32) / MLA_ROPE)
    inv_b = ROPE_THETA ** (-jnp.arange(0, MOBA_ROT, 2, dtype=F32) / MOBA_ROT)
    hm, hb = MLA_ROPE // 2, MOBA_ROT // 2
    z = lambda n: jnp.zeros((n,), F32)
    o = lambda n: jnp.ones((n,), F32)
    f_m = jnp.concatenate([z(MLA_NOPE), inv_m, inv_m, z(LANES - MLA_NOPE - MLA_ROPE)])
    m1_m = jnp.concatenate([z(MLA_NOPE), -o(hm), z(hm), z(LANES - MLA_NOPE - MLA_ROPE)])
    m2_m = jnp.concatenate([z(MLA_NOPE), z(hm), o(hm), z(LANES - MLA_NOPE - MLA_ROPE)])
    rest = MOBA_HEAD_DIM - MOBA_ROT
    f_b = jnp.tile(jnp.concatenate([inv_b, inv_b, z(rest)]), LANES // MOBA_HEAD_DIM)
    m1_b = jnp.tile(jnp.concatenate([-o(hb), z(hb), z(rest)]), LANES // MOBA_HEAD_DIM)
    m2_b = jnp.tile(jnp.concatenate([z(hb), o(hb), z(rest)]), LANES // MOBA_HEAD_DIM)
    f = jnp.stack([f_m, f_b])
    m1 = jnp.stack([m1_m, m1_b])
    m2 = jnp.stack([m2_m, m2_b])
    tm = min(t, 2048)
    out = jax.ShapeDtypeStruct((2, t, LANES), F32)
    row_spec = _const_spec((2, LANES))
    tab_spec = pl.BlockSpec((2, tm, LANES), lambda i: (0, i, 0))
    return pl.pallas_call(
        _rope_table_kernel,
        grid=(t // tm,),
        in_specs=[pl.BlockSpec((tm, 1), lambda i: (i, 0)), row_spec, row_spec, row_spec],
        out_specs=[tab_spec, tab_spec, tab_spec],
        out_shape=[out, out, out],
        compiler_params=_params(("parallel",)),
        name="rope_tables",
    )(positions.reshape(t, 1), f, m1, m2)


def _rope(t, c, s1, s2, half):
    w = t.shape[-1]
    return t * c + pltpu.roll(t, w - half, 1) * s1 + pltpu.roll(t, half, 1) * s2


def _rms(x, g):
    ms = jnp.mean(jnp.square(x), axis=-1, keepdims=True)
    return x * lax.rsqrt(ms + RMS_EPS) * g


def _layer_norm(x, g, b):
    mu = jnp.mean(x, axis=-1, keepdims=True)
    xc = x - mu
    var = jnp.mean(jnp.square(xc), axis=-1, keepdims=True)
    return xc * lax.rsqrt(var + LN_EPS) * g + b


ATT_PROJ_TM = 256


def _att_proj_kernel(x_ref, win_ref, wuq_ref, wk_ref, wv_ref, wvb_ref, qn_ref, kvn_ref, c_ref, s1_ref, s2_ref,
                     qm_ref, km_ref, vm_ref, qb_ref, kb_ref, vb_ref, kmean_ref):
    xb = x_ref[...].astype(BF16)
    cm, s1m, s2m = c_ref[0], s1_ref[0], s2_ref[0]
    cb, s1b, s2b = c_ref[1], s1_ref[1], s2_ref[1]
    hm, hb = MLA_ROPE // 2, MOBA_ROT // 2
    mla_scale = (MLA_NOPE + MLA_ROPE) ** -0.5 * LOG2_E
    moba_scale = MOBA_HEAD_DIM ** -0.5 * LOG2_E

    h_lat = _dot(xb, win_ref[:, 0:512])
    c_q = h_lat[:, 0:MLA_Q_LORA]
    c_kv = h_lat[:, MLA_Q_LORA:MLA_Q_LORA + MLA_KV_LORA]
    hq = _dot(xb, win_ref[:, 512:1024])
    hk = _dot(xb, win_ref[:, 1024:1536])
    vb_ref[...] = _dot_nt(wvb_ref[...], xb).astype(BF16)
    k_rope = _rope(h_lat[:, 384:512], cm, s1m, s2m, hm)

    q = _dot(_rms(c_q, qn_ref[...]).astype(BF16), wuq_ref[...])
    for h in range(MLA_HEADS):
        sl = slice(h * HEAD_PAD, (h + 1) * HEAD_PAD)
        qm_ref[:, sl] = (_rope(q[:, sl], cm, s1m, s2m, hm) * mla_scale).astype(BF16)

    ckv = _rms(c_kv, kvn_ref[...]).astype(BF16)
    k = _dot(ckv, wk_ref[...])
    for h in range(MLA_HEADS):
        sl = slice(h * HEAD_PAD, (h + 1) * HEAD_PAD)
        km_ref[:, sl] = (k[:, sl] + k_rope).astype(BF16)
    vm_ref[...] = _dot_nt(wv_ref[...], ckv).astype(BF16)

    for g in range(MOBA_WIDTH // LANES):
        sl = slice(g * LANES, (g + 1) * LANES)
        qb_ref[:, sl] = (_rope(hq[:, sl], cb, s1b, s2b, hb) * moba_scale).astype(BF16)
        kr = _rope(hk[:, sl], cb, s1b, s2b, hb)
        kb_ref[:, sl] = kr.astype(BF16)
        for r in range(ATT_PROJ_TM // MOBA_BLOCK):
            rows = slice(r * MOBA_BLOCK, (r + 1) * MOBA_BLOCK)
            kmean_ref[r, :, sl] = jnp.mean(kr[rows], axis=0, keepdims=True)


def _att_proj(x2, tabs, w_in, w_uq, w_k, w_v_t, w_vb_t, q_norm, kv_norm):
    t = x2.shape[0]
    tm = ATT_PROJ_TM
    c, s1, s2 = tabs
    row = lambda w: pl.BlockSpec((tm, w), lambda i: (i, 0))
    col = lambda h: pl.BlockSpec((h, tm), lambda i: (0, i))
    tab = pl.BlockSpec((2, tm, LANES), lambda i: (0, i, 0))
    bf = lambda w: jax.ShapeDtypeStruct((t, w), BF16)
    bf_t = lambda h: jax.ShapeDtypeStruct((h, t), BF16)
    nblk = tm // MOBA_BLOCK
    consts = [w_in, w_uq, w_k, w_v_t, w_vb_t, q_norm, kv_norm]
    return pl.pallas_call(
        _att_proj_kernel,
        grid=(t // tm,),
        in_specs=[row(D_MODEL)] + [_const_spec(w.shape) for w in consts] + [tab, tab, tab],
        out_specs=[row(1024), row(1024), col(512), row(512), row(512), col(512),
                   pl.BlockSpec((nblk, 1, MOBA_WIDTH), lambda i: (i, 0, 0))],
        out_shape=[bf(1024), bf(1024), bf_t(512), bf(512), bf(512), bf_t(512),
                   jax.ShapeDtypeStruct((t // MOBA_BLOCK, 1, MOBA_WIDTH), F32)],
        compiler_params=_params(("parallel",)),
        name="att_proj",
    )(x2, *consts, c, s1, s2)


def _moba_select_kernel(q_ref, kab_ref, r_ref, lc_ref, qa_ref, *, nb):
    own = pl.program_id(1)
    lane = lax.broadcasted_iota(jnp.int32, (1, LANES), 1)
    pair_valid = jnp.where(lc_ref[0:1, :] < own.astype(F32), 1.0, 0.0)
    tie_first = lc_ref[1:2, :]
    low = lane < (LANES // 2)
    lt_own = lane < own
    rest_pen = jnp.where((lane > own) & (lane < nb), NEG_BIG, 0.0)
    gates = []
    for h in range(MOBA_HEADS):
        qp = q_ref[:, (h // 2) * LANES:(h // 2 + 1) * LANES]
        qm = jnp.where(low if h % 2 == 0 else jnp.logical_not(low), qp, jnp.zeros_like(qp))
        qa_ref[:, 2 * h * LANES:(2 * h + 1) * LANES] = qm
        gates.append(_dot(qm, kab_ref[0, h]))
    beats = []
    for ab in gates:
        a, b = ab[:, 0:LANES], ab[:, LANES:2 * LANES]
        beats.append((jnp.where(b > a, 1.0, jnp.where(b == a, tie_first, 0.0)) * pair_valid).astype(BF16))
    counts = [_dot(bt, r_ref[...]) for bt in beats]
    for h, cnt in enumerate(counts):
        pen = jnp.where(lt_own, jnp.where(cnt < MOBA_TOPK, 0.0, NEG_BIG), rest_pen)
        qa_ref[:, (2 * h + 1) * LANES:(2 * h + 2) * LANES] = pen.astype(BF16)


def _moba_select(qb, kmean, batch, seq):
    t = qb.shape[0]
    nb = seq // MOBA_BLOCK
    assert nb * nb <= LANES
    km = kmean.reshape(batch, nb, MOBA_HEADS, MOBA_HEAD_DIM).transpose(0, 2, 3, 1)
    a = jnp.repeat(km, nb, axis=-1)
    b = jnp.tile(km, (1, 1, 1, nb))
    zc = jnp.zeros(km.shape[:3] + (LANES - nb * nb,), F32)
    ab = jnp.concatenate([a, zc, b, zc], axis=-1)
    zr = jnp.zeros_like(ab)
    odd = (jnp.arange(MOBA_HEADS) % 2 == 1)[None, :, None, None]
    kab = jnp.where(odd, jnp.concatenate([zr, ab], axis=2), jnp.concatenate([ab, zr], axis=2)).astype(BF16)
    cidx = jnp.arange(LANES)
    rmat = ((cidx[:, None] // nb == cidx[None, :]) & (cidx[:, None] < nb * nb)).astype(BF16)
    used = cidx < nb * nb
    lane_consts = jnp.zeros((SUBLANES, LANES), F32)
    lane_consts = lane_consts.at[0].set(jnp.where(used, cidx % nb, nb).astype(F32))
    lane_consts = lane_consts.at[1].set((used & (cidx % nb < cidx // nb)).astype(F32))
    tq = MOBA_BLOCK
    return pl.pallas_call(
        functools.partial(_moba_select_kernel, nb=nb),
        grid=(batch, seq // tq),
        in_specs=[pl.BlockSpec((tq, MOBA_WIDTH), lambda bi, qi: (bi * (seq // tq) + qi, 0)),
                  pl.BlockSpec((1, MOBA_HEADS, LANES, 2 * LANES), lambda bi, qi: (bi, 0, 0, 0)),
                  _const_spec((LANES, LANES)), _const_spec((SUBLANES, LANES))],
        out_specs=pl.BlockSpec((tq, MOBA_HEADS * 2 * LANES), lambda bi, qi: (bi * (seq // tq) + qi, 0)),
        out_shape=jax.ShapeDtypeStruct((t, MOBA_HEADS * 2 * LANES), BF16),
        compiler_params=_params(("parallel", "parallel")),
        name="moba_select",
    )(qb, kab, rmat, lane_consts)


ATT_TILE = 256
ATT_PAIRS = 4


def _attn_kernel(*refs, moba):
    if moba:
        q_ref, k_ref, oh_ref, v_ref, o_ref, va_ref, vb_ref = refs
    else:
        q_ref, k_ref, v_ref, o_ref, va_ref, vb_ref = refs
        oh_ref = None
    seq = q_ref.shape[0]
    tq = ATT_TILE
    pairs = v_ref.shape[0] // LANES
    heads = 2 * pairs
    qw = q_ref.shape[1] // heads
    sub = lax.broadcasted_iota(jnp.int32, (LANES, 1), 0)
    half = LANES // 2
    for p in range(pairs):
        ps = slice(p * LANES, (p + 1) * LANES)
        v = v_ref[ps, :].astype(F32)
        va_ref[ps, :] = jnp.where(sub < half, v, jnp.where(sub == half, 1.0, 0.0)).astype(BF16)
        vb_ref[ps, :] = jnp.where(sub >= half, v, jnp.where(sub == 0, 1.0, 0.0)).astype(BF16)
    key = lax.broadcasted_iota(jnp.int32, (tq, tq), 0)
    qry = lax.broadcasted_iota(jnp.int32, (tq, tq), 1)
    causal = key <= qry

    def load_q(r0, h):
        return q_ref[pl.ds(r0, tq), h * qw:(h + 1) * qw]

    def load_k(c0, h):
        if moba:
            ps = slice((h // 2) * LANES, (h // 2 + 1) * LANES)
            return jnp.concatenate([k_ref[pl.ds(c0, tq), ps], oh_ref[pl.ds(c0, tq), :]], axis=1)
        return k_ref[pl.ds(c0, tq), h * HEAD_PAD:(h + 1) * HEAD_PAD]

    def load_v(c0, h):
        ps = slice((h // 2) * LANES, (h // 2 + 1) * LANES)
        return (va_ref if h % 2 == 0 else vb_ref)[ps, pl.ds(c0, tq)]

    def scores_of(r0, c0):
        return [_dot_nt(load_k(c0, h), load_q(r0, h)) for h in range(heads)]

    def value_update(c0, accs, alphas, ps):
        return tuple(alphas[h] * accs[h] + _dot(load_v(c0, h), ps[h]) for h in range(heads))

    def softmax_update(scores, ms):
        out = []
        for h in range(heads):
            m_new = jnp.maximum(ms[h], jnp.max(scores[h], axis=0, keepdims=True))
            alpha = jnp.exp2(ms[h] - m_new)
            p = jnp.exp2((scores[h] - m_new).astype(BF16))
            out.append((m_new, alpha, p))
        return tuple(zip(*out))

    def q_tile(qi, _):
        r0 = pl.multiple_of(qi * tq, tq)

        def body(j, carry):
            ms, accs, alphas, ps = carry
            scores = scores_of(r0, pl.multiple_of(j * tq, tq))
            accs = value_update(pl.multiple_of(jnp.maximum(j - 1, 0) * tq, tq), accs, alphas, ps)
            ms, alphas, ps = softmax_update(scores, ms)
            return ms, accs, alphas, ps

        rep = lambda a: (a,) * heads
        init = (rep(jnp.full((1, tq), -jnp.inf, F32)), rep(jnp.zeros((LANES, tq), F32)),
                rep(jnp.ones((1, tq), F32)), rep(jnp.zeros((tq, tq), BF16)))
        ms, accs, alphas, ps = lax.fori_loop(0, qi, body, init)
        scores = [jnp.where(causal, s_t, -jnp.inf) for s_t in scores_of(r0, r0)]
        accs = value_update(pl.multiple_of(jnp.maximum(qi - 1, 0) * tq, tq), accs, alphas, ps)
        ms, alphas, ps = softmax_update(scores, ms)
        accs = value_update(r0, accs, alphas, ps)
        for p in range(pairs):
            a0, a1 = accs[2 * p], accs[2 * p + 1]
            o_t = jnp.where(sub < half, a0 * (1.0 / a0[half:half + 1, :]), a1 * (1.0 / a1[0:1, :]))
            o_ref[pl.ds(r0, tq), p * LANES:(p + 1) * LANES] = o_t.T.astype(o_ref.dtype)
        return 0

    lax.fori_loop(0, seq // tq, q_tile, 0)


def _attention(q, k, v_t, batch, seq, onehot=None):
    t = q.shape[0]
    moba = onehot is not None
    pairs = v_t.shape[0] // LANES
    steps = pairs // ATT_PAIRS
    blk = lambda a: pl.BlockSpec((seq, a.shape[1] // steps), lambda bi, p: (bi, p))
    vw = ATT_PAIRS * LANES
    in_specs = [blk(q), blk(k)]
    args = [q, k]
    if moba:
        in_specs.append(_const_spec((seq, LANES)))
        args.append(onehot)
    in_specs.append(pl.BlockSpec((vw, seq), lambda bi, p: (p, bi)))
    args.append(v_t)
    return pl.pallas_call(
        functools.partial(_attn_kernel, moba=moba),
        grid=(batch, steps),
        in_specs=in_specs,
        out_specs=pl.BlockSpec((seq, vw), lambda bi, p: (bi, p)),
        out_shape=jax.ShapeDtypeStruct((t, pairs * LANES), BF16),
        scratch_shapes=[pltpu.VMEM((vw, seq), BF16), pltpu.VMEM((vw, seq), BF16)],
        compiler_params=_params(("parallel", "parallel")),
        name="moba_attn" if moba else "mla_attn",
    )(*args)


OUT_PROJ_TM = 512
OUT_PROJ_SPLIT = 4


def _out_proj_kernel(*refs, n_act):
    acts = refs[:n_act]
    ws = refs[n_act:2 * n_act]
    x_ref, g_ref, b_ref, o_ref = refs[2 * n_act:]
    sub = OUT_PROJ_TM // OUT_PROJ_SPLIT
    mixes = []
    for r in range(OUT_PROJ_SPLIT):
        rows = slice(r * sub, (r + 1) * sub)
        m = _dot(acts[0][rows, :], ws[0][...])
        for a_ref, w_ref in zip(acts[1:], ws[1:]):
            m = m + _dot(a_ref[rows, :], w_ref[...])
        mixes.append(m)
    for r in range(OUT_PROJ_SPLIT):
        rows = slice(r * sub, (r + 1) * sub)
        o_ref[rows, :] = _layer_norm(DEEPNORM_ALPHA * x_ref[rows, :] + mixes[r], g_ref[...], b_ref[...])


def _out_proj_ln(acts, ws, x2, g, b):
    t = x2.shape[0]
    tm = OUT_PROJ_TM
    row = lambda w: pl.BlockSpec((tm, w), lambda i: (i, 0))
    return pl.pallas_call(
        functools.partial(_out_proj_kernel, n_act=len(acts)),
        grid=(t // tm,),
        in_specs=[row(a.shape[1]) for a in acts] + [_const_spec(w.shape) for w in ws]
        + [row(D_MODEL), _const_spec(g.shape), _const_spec(b.shape)],
        out_specs=row(D_MODEL),
        out_shape=jax.ShapeDtypeStruct((t, D_MODEL), F32),
        compiler_params=_params(("parallel",)),
        name="out_proj_ln",
    )(*acts, *ws, x2, g, b)


FFN_TM = 256
FFN_CHUNK = 256


def _ffn_kernel(x_ref, p_ref, wup_ref, cw_ref, cb_ref, wdn_ref, g_ref, b_ref, wg_ref, wp_ref, o_ref, gbuf_ref,
                *, tiles_per_seq):
    tm = FFN_TM
    hist = SUBLANES

    @pl.when(pl.program_id(0) % tiles_per_seq == 0)
    def _():
        gbuf_ref[0:hist, :] = jnp.zeros((hist, D_FF), F32)

    x = x_ref[...]
    xb = x.astype(BF16)
    n_chunks = D_FF // FFN_CHUNK

    def up_proj(c):
        lo = c * FFN_CHUNK
        return (_dot(xb, wup_ref[:, lo:lo + FFN_CHUNK]), _dot(xb, wup_ref[:, D_FF + lo:D_FF + lo + FFN_CHUNK]))

    f = jnp.zeros((tm, D_MODEL), F32)
    nxt = up_proj(0)
    for c in range(n_chunks):
        cs = slice(c * FFN_CHUNK, (c + 1) * FFN_CHUNK)
        gate, up = nxt
        if c + 1 < n_chunks:
            nxt = up_proj(c + 1)
        gbuf_ref[hist:hist + tm, cs] = gate
        conv = cw_ref[FFN_CONV - 1:FFN_CONV, cs] * gate + cb_ref[:, cs]
        for k in range(FFN_CONV - 1):
            d = FFN_CONV - 1 - k
            conv = conv + cw_ref[k:k + 1, cs] * gbuf_ref[hist - d:hist - d + tm, cs]
        hid = 0.5 * conv * (1.0 + lax.erf(conv * INV_SQRT2)) * up
        f = f + _dot(hid.astype(BF16), wdn_ref[cs, :])
    proj = _dot(p_ref[...].astype(BF16), wp_ref[...])
    gbuf_ref[0:hist, :] = gbuf_ref[tm:tm + hist, :]
    y = _layer_norm(DEEPNORM_ALPHA * x + f, g_ref[...], b_ref[...])
    gate = jax.nn.sigmoid(_dot(y.astype(BF16), wg_ref[...]))
    o_ref[...] = y + gate * proj


def _ffn_ln_ple(x2, p2, w_up, conv_w, conv_b, w_down, g, b, w_gate, w_proj, seq):
    t = x2.shape[0]
    tm = FFN_TM
    row = lambda w: pl.BlockSpec((tm, w), lambda i: (i, 0))
    consts = [w_up, conv_w, conv_b, w_down, g, b, w_gate, w_proj]
    return pl.pallas_call(
        functools.partial(_ffn_kernel, tiles_per_seq=seq // tm),
        grid=(t // tm,),
        in_specs=[row(D_MODEL), row(PLE_DIM)] + [_const_spec(w.shape) for w in consts],
        out_specs=row(D_MODEL),
        out_shape=jax.ShapeDtypeStruct((t, D_MODEL), F32),
        scratch_shapes=[pltpu.VMEM((tm + SUBLANES, D_FF), F32)],
        compiler_params=_params(("arbitrary",)),
        name="ffn_ln_ple",
    )(x2, p2, *consts)


SSD_PROJ_TM = 256
SSD_PROJ_CHUNK = 256
SSD_BC = SSD_GROUPS * SSD_STATE


def _ssd_proj_kernel(x_ref, win_ref, cw_ref, cb_ref, dtb_ref, a_ref, tri_ref,
                     z_ref, xs_ref, b_ref, c_ref, dt_ref, la_ref, lat_ref, cbuf_ref, xb_ref, *, tiles_per_seq):
    tm = SSD_PROJ_TM
    hist = SUBLANES
    ck = SSD_PROJ_CHUNK

    @pl.when(pl.program_id(0) % tiles_per_seq == 0)
    def _():
        cbuf_ref[0:hist, :] = jnp.zeros((hist, SSD_CONV_DIM), F32)

    xb_ref[...] = x_ref[...].astype(BF16)
    hd = _dot(xb_ref[...], win_ref[:, SSD_INNER + SSD_CONV_DIM:]) + dtb_ref[...]
    dt = jnp.maximum(hd, 0.0) + jnp.log1p(jnp.exp(-jnp.abs(hd)))
    dt_ref[...] = dt
    la = jnp.dot(tri_ref[...], dt * a_ref[...], preferred_element_type=F32, precision=lax.Precision.HIGHEST) * LOG2_E
    la_ref[...] = la
    lat_ref[...] = la.T
    for c in range(SSD_INNER // ck):
        z_ref[:, c * ck:(c + 1) * ck] = _dot(xb_ref[...], win_ref[:, c * ck:(c + 1) * ck]).astype(z_ref.dtype)

    def conv_in(c):
        return _dot(xb_ref[...], win_ref[:, SSD_INNER + c * ck:SSD_INNER + (c + 1) * ck])

    n_conv = SSD_CONV_DIM // ck
    nxt = conv_in(0)
    for c in range(n_conv):
        cs = slice(c * ck, (c + 1) * ck)
        h = nxt
        if c + 1 < n_conv:
            nxt = conv_in(c + 1)
        cbuf_ref[hist:hist + tm, cs] = h
        conv = cw_ref[SSD_CONV - 1:SSD_CONV, cs] * h + cb_ref[:, cs]
        for k in range(SSD_CONV - 1):
            d = SSD_CONV - 1 - k
            conv = conv + cw_ref[k:k + 1, cs] * cbuf_ref[hist - d:hist - d + tm, cs]
        act = (conv * jax.nn.sigmoid(conv)).astype(xs_ref.dtype)
        lo = c * ck
        if lo < SSD_INNER:
            xs_ref[:, cs] = act
        elif lo < SSD_INNER + SSD_BC:
            b_ref[:, lo - SSD_INNER:lo - SSD_INNER + ck] = act
        else:
            c_ref[:, lo - SSD_INNER - SSD_BC:lo - SSD_INNER - SSD_BC + ck] = act
    cbuf_ref[0:hist, :] = cbuf_ref[tm:tm + hist, :]


def _ssd_proj(x2, w_in, conv_w, conv_b, dt_bias, a_row, seq):
    t = x2.shape[0]
    tm = SSD_PROJ_TM
    assert SSD_BC % SSD_PROJ_CHUNK == 0 and SSD_INNER % SSD_PROJ_CHUNK == 0 and tm % SSD_CHUNK == 0
    row = lambda w: pl.BlockSpec((tm, w), lambda i: (i, 0))
    out = lambda w, dt: jax.ShapeDtypeStruct((t, w), dt)
    idx = jnp.arange(tm)
    tri = ((idx[:, None] // SSD_CHUNK == idx[None, :] // SSD_CHUNK) & (idx[None, :] <= idx[:, None])).astype(F32)
    consts = [w_in, conv_w, conv_b, dt_bias, a_row, tri]
    return pl.pallas_call(
        functools.partial(_ssd_proj_kernel, tiles_per_seq=seq // tm),
        grid=(t // tm,),
        in_specs=[row(D_MODEL)] + [_const_spec(w.shape) for w in consts],
        out_specs=[row(SSD_INNER), row(SSD_INNER), row(SSD_BC), row(SSD_BC), row(LANES), row(LANES),
                   pl.BlockSpec((LANES, tm), lambda i: (0, i))],
        out_shape=[out(SSD_INNER, BF16), out(SSD_INNER, BF16), out(SSD_BC, BF16), out(SSD_BC, BF16),
                   out(LANES, F32), out(LANES, F32), jax.ShapeDtypeStruct((LANES, t), F32)],
        scratch_shapes=[pltpu.VMEM((tm + SUBLANES, SSD_CONV_DIM), F32), pltpu.VMEM((tm, D_MODEL), BF16)],
        compiler_params=_params(("arbitrary",)),
        name="ssd_proj",
    )(x2, *consts)


def _ssd_scan_kernel(xs_ref, z_ref, b_ref, c_ref, dt_ref, la_ref, lat_ref, d_ref, nw_ref, o_ref, state_ref):
    L = SSD_CHUNK
    hg = SSD_HEADS // SSD_GROUPS
    gw = hg * SSD_HEAD_DIM

    @pl.when(pl.program_id(1) == 0)
    def _():
        state_ref[...] = jnp.zeros_like(state_ref)

    lane = lax.broadcasted_iota(jnp.int32, (1, LANES), 1)
    low = lane < LANES // 2
    row = lax.broadcasted_iota(jnp.int32, (L, L), 0)
    col = lax.broadcasted_iota(jnp.int32, (L, L), 1)
    causal = col <= row

    dt = dt_ref[...]
    la = la_ref[...]
    la_t = lat_ref[...]

    for g in range(SSD_GROUPS):
        bg = b_ref[:, g * SSD_STATE:(g + 1) * SSD_STATE]
        cg = c_ref[:, g * SSD_STATE:(g + 1) * SSD_STATE]
        cb = _dot_nt(cg, bg)
        bg_t = bg.astype(F32).T.astype(BF16)
        la_bc = [jnp.broadcast_to(la[:, h:h + 1], (L, LANES)) for h in range(g * hg, (g + 1) * hg)]
        dt_bc = [jnp.broadcast_to(dt[:, h:h + 1], (L, LANES)) for h in range(g * hg, (g + 1) * hg)]
        y_parts = []
        for pp in range(hg // 2):
            h0 = g * hg + 2 * pp
            ps = slice(h0 * SSD_HEAD_DIM, (h0 + 2) * SSD_HEAD_DIM)
            xs = xs_ref[:, ps].astype(F32)
            dt_pair = jnp.where(low, dt_bc[2 * pp], dt_bc[2 * pp + 1])
            la_pair = jnp.where(low, la_bc[2 * pp], la_bc[2 * pp + 1])
            xdt = xs * dt_pair
            xdt_b = xdt.astype(BF16)
            y = jnp.zeros((L, LANES), F32)
            for hh in range(2):
                seg = la_bc[2 * pp + hh] - la_t[h0 + hh:h0 + hh + 1, :]
                w = (cb * jnp.exp2(jnp.where(causal, seg, -jnp.inf))).astype(BF16)
                keep = low if hh == 0 else jnp.logical_not(low)
                y = y + _dot(w, jnp.where(keep, xdt_b, jnp.zeros_like(xdt_b)))
            st = state_ref[:, ps]
            y = y + _dot(cg, st.astype(BF16)) * jnp.exp2(la_pair)
            la_end = la_pair[L - 1:L, :]
            to_end = jnp.exp2(la_end - la_pair)
            state_ref[:, ps] = st * jnp.exp2(la_end) + _dot(bg_t, (xdt * to_end).astype(BF16))
            y = y + d_ref[:, ps] * xs
            zz = z_ref[:, ps].astype(F32)
            y_parts.append(y * (zz * jax.nn.sigmoid(zz)))
        yg = jnp.concatenate(y_parts, axis=1)
        ms = jnp.mean(jnp.square(yg), axis=-1, keepdims=True)
        gs = slice(g * gw, (g + 1) * gw)
        o_ref[:, gs] = (yg * lax.rsqrt(ms + RMS_EPS) * nw_ref[:, gs]).astype(o_ref.dtype)


def _ssd_scan(xs, z, bm, cm, dt, la, la_t, d_row, norm_w, batch, seq):
    t = xs.shape[0]
    L = SSD_CHUNK
    nc = seq // L
    assert SSD_INNER // SSD_GROUPS == SSD_BC
    row = lambda w: pl.BlockSpec((L, w), lambda bi, ci: (bi * nc + ci, 0))
    consts = [d_row, norm_w]
    return pl.pallas_call(
        _ssd_scan_kernel,
        grid=(batch, nc),
        in_specs=[row(SSD_INNER), row(SSD_INNER), row(SSD_BC), row(SSD_BC), row(LANES), row(LANES),
                  pl.BlockSpec((LANES, L), lambda bi, ci: (0, bi * nc + ci))]
        + [_const_spec(w.shape) for w in consts],
        out_specs=row(SSD_INNER),
        out_shape=jax.ShapeDtypeStruct((t, SSD_INNER), BF16),
        scratch_shapes=[pltpu.VMEM((SSD_STATE, SSD_INNER), F32)],
        compiler_params=_params(("parallel", "arbitrary")),
        name="ssd_scan",
    )(xs, z, bm, cm, dt, la, la_t, *consts)


def _pad_cols(w, n):
    return jnp.concatenate([w, jnp.zeros((w.shape[0], n), w.dtype)], axis=1)


def _attention_layer(x2, tabs, batch, seq, w_in, q_norm, w_uq, kv_norm, w_ukv, w_out):
    o1 = MLA_Q_LORA + MLA_KV_LORA
    zc = lambda n: jnp.zeros((D_MODEL, n), w_in.dtype)
    o2 = o1 + MLA_ROPE + 2 * MOBA_WIDTH
    w_in2 = jnp.concatenate([w_in[:, :o1], zc(MLA_NOPE), w_in[:, o1:o1 + MLA_ROPE],
                             zc(LANES - MLA_NOPE - MLA_ROPE), w_in[:, o1 + MLA_ROPE:o2]], axis=1).astype(BF16)
    w_vb_t = w_in[:, o2:].T.astype(BF16)
    dqk = MLA_NOPE + MLA_ROPE
    w_uq2 = jnp.pad(w_uq.reshape(MLA_Q_LORA, MLA_HEADS, dqk), ((0, 0), (0, 0), (0, HEAD_PAD - dqk)))
    w_uq2 = w_uq2.reshape(MLA_Q_LORA, MLA_HEADS * HEAD_PAD).astype(BF16)
    w_kv3 = w_ukv.reshape(MLA_KV_LORA, MLA_HEADS, MLA_NOPE + MLA_V)
    w_k2 = jnp.pad(w_kv3[:, :, :MLA_NOPE], ((0, 0), (0, 0), (0, HEAD_PAD - MLA_NOPE)))
    w_k2 = w_k2.reshape(MLA_KV_LORA, MLA_HEADS * HEAD_PAD).astype(BF16)
    w_v_t = w_kv3[:, :, MLA_NOPE:].reshape(MLA_KV_LORA, MLA_HEADS * MLA_V).T.astype(BF16)

    qm, km, vm, qb, kb, vb, kmean = _att_proj(x2, tabs, w_in2, w_uq2, w_k2, w_v_t, w_vb_t,
                                              q_norm.reshape(1, -1), kv_norm.reshape(1, -1))
    o_mla = _attention(qm, km, vm, batch, seq)
    qa = _moba_select(qb, kmean, batch, seq)
    pos = jnp.arange(seq)
    onehot = (pos[:, None] // MOBA_BLOCK == jnp.arange(LANES)[None, :]).astype(BF16)
    o_moba = _attention(qa, kb, vb, batch, seq, onehot=onehot)
    n_mla = MLA_HEADS * MLA_V
    w_o = w_out.astype(BF16)
    return [o_mla, o_moba], [w_o[:n_mla], w_o[n_mla:]]


def _ssd_layer(x2, batch, seq, w_in, conv_w, conv_b, dt_bias, a_log, d_skip, norm_w, w_out):
    w_in2 = _pad_cols(w_in, LANES - SSD_HEADS).astype(BF16)
    dtb = _pad_cols(dt_bias.reshape(1, -1), LANES - SSD_HEADS)
    a_row = _pad_cols((-jnp.exp(a_log.astype(F32))).reshape(1, -1), LANES - SSD_HEADS)
    z, xs, bm, cm, dt, la, la_t = _ssd_proj(x2, w_in2, conv_w, conv_b.reshape(1, -1), dtb, a_row, seq)
    d_row = jnp.repeat(d_skip.astype(F32), SSD_HEAD_DIM).reshape(1, -1)
    y = _ssd_scan(xs, z, bm, cm, dt, la, la_t, d_row, norm_w.reshape(1, -1), batch, seq)
    return [y], [w_out.astype(BF16)]


def kernel(x, p, positions, att_w_in, mla_q_norm, mla_w_uq, mla_kv_norm, mla_w_ukv, att_w_out, ssd_w_in, ssd_conv_w, ssd_conv_b, ssd_dt_bias, ssd_a_log, ssd_d, ssd_norm, ssd_w_out, ln_mix_g, ln_mix_b, ffn_w_up, ffn_conv_w, ffn_conv_b, ffn_w_down, ln_ffn_g, ln_ffn_b, ple_w_gate, ple_w_proj):
    batch, seq, _ = x.shape
    depth = p.shape[0]
    t = batch * seq
    x2 = x.reshape(t, D_MODEL)
    tabs = _rope_tables(positions)
    for i in range(depth):
        j = i // 2
        if i % 2 == 0:
            acts, ws = _attention_layer(x2, tabs, batch, seq, att_w_in[j], mla_q_norm[j], mla_w_uq[j],
                                        mla_kv_norm[j], mla_w_ukv[j], att_w_out[j])
        else:
            acts, ws = _ssd_layer(x2, batch, seq, ssd_w_in[j], ssd_conv_w[j], ssd_conv_b[j], ssd_dt_bias[j],
                                  ssd_a_log[j], ssd_d[j], ssd_norm[j], ssd_w_out[j])
        x2 = _out_proj_ln(acts, ws, x2, ln_mix_g[i].reshape(1, -1), ln_mix_b[i].reshape(1, -1))
        x2 = _ffn_ln_ple(x2, p[i].reshape(t, PLE_DIM), ffn_w_up[i].astype(BF16), ffn_conv_w[i],
                         ffn_conv_b[i].reshape(1, -1), ffn_w_down[i].astype(BF16),
                         ln_ffn_g[i].reshape(1, -1), ln_ffn_b[i].reshape(1, -1),
                         ple_w_gate[i].astype(BF16), ple_w_proj[i].astype(BF16), seq)
    return x2.reshape(batch, seq, D_MODEL)
```

```python
import functools
import math

import jax
import jax.numpy as jnp
from jax import lax
from jax.experimental import pallas as pl
from jax.experimental.pallas import tpu as pltpu

D_MODEL = 1024
PLE_DIM = 256
ROPE_THETA = 500000.0
MLA_HEADS = 8
MLA_Q_LORA = 256
MLA_KV_LORA = 128
MLA_NOPE = 64
MLA_ROPE = 32
MLA_V = 64
MOBA_HEADS = 8
MOBA_HEAD_DIM = 64
MOBA_ROT = MOBA_HEAD_DIM // 4
MOBA_BLOCK = 256
MOBA_TOPK = 3
MOBA_WIDTH = MOBA_HEADS * MOBA_HEAD_DIM
SSD_INNER = 2 * D_MODEL
SSD_HEAD_DIM = 64
SSD_HEADS = SSD_INNER // SSD_HEAD_DIM
SSD_GROUPS = 4
SSD_STATE = 128
SSD_CONV = 4
SSD_CHUNK = 128
SSD_CONV_DIM = SSD_INNER + 2 * SSD_GROUPS * SSD_STATE
D_FF = 2816
FFN_CONV = 3
LN_EPS = 1e-5
RMS_EPS = 1e-6
DEPTH = 2
DEEPNORM_ALPHA = (2 * DEPTH) ** 0.25

LANES = 128
SUBLANES = 8
HEAD_PAD = LANES
VMEM_LIMIT = 56 * 1024 * 1024
NEG_BIG = -1e30
INV_SQRT2 = 0.7071067811865476
LOG2_E = 1.4426950408889634

F32 = jnp.float32
BF16 = jnp.bfloat16


def _dot(a, b):
    return jnp.dot(a, b, preferred_element_type=F32)


def _dot_nt(a, b):
    return lax.dot_general(a, b, (((1,), (1,)), ((), ())), preferred_element_type=F32)


def _const_spec(shape):
    return pl.BlockSpec(shape, lambda *_: (0,) * len(shape), pipeline_mode=pl.Buffered(1))


def _params(semantics):
    return pltpu.CompilerParams(dimension_semantics=semantics, vmem_limit_bytes=VMEM_LIMIT)


ROPE_PACK = 4


def _split3(a):
    hi = a.astype(BF16)
    r1 = a - hi.astype(F32)
    mid = r1.astype(BF16)
    lo = (r1 - mid.astype(F32)).astype(BF16)
    return jnp.concatenate([hi, mid, lo], axis=1)


def _rope_table_kernel(pos_ref, f_ref, ec_ref, e1_ref, e2_ref, bias_ref, c_ref, s1_ref, s2_ref):
    pos = pos_ref[...].astype(F32)
    ang = pos[:, 0:1] * f_ref[0:1, :]
    for u in range(1, ROPE_PACK):
        ang = ang + pos[:, u:u + 1] * f_ref[u:u + 1, :]
    cos3, sin3 = _split3(jnp.cos(ang)), _split3(jnp.sin(ang))
    for r in range(2):
        c_ref[r] = _dot(cos3, ec_ref[r]) + bias_ref[r:r + 1, :]
        s1_ref[r] = _dot(sin3, e1_ref[r])
        s2_ref[r] = _dot(sin3, e2_ref[r])


def _rope_tables(positions):
    t = positions.size
    inv_m = ROPE_THETA ** (-jnp.arange(0, MLA_ROPE, 2, dtype=F32) / MLA_ROPE)
    inv_b = ROPE_THETA ** (-jnp.arange(0, MOBA_ROT, 2, dtype=F32) / MOBA_ROT)
    hm, hb = MLA_ROPE // 2, MOBA_ROT // 2
    slot = LANES // ROPE_PACK
    assert hm + hb <= slot and t % ROPE_PACK == 0
    f_tok = jnp.concatenate([inv_m, inv_b, jnp.zeros((slot - hm - hb,), F32)])
    f = jnp.kron(jnp.eye(ROPE_PACK, dtype=F32), f_tok[None, :])
    src = jnp.arange(slot)[:, None]
    dst = jnp.arange(LANES)[None, :]
    m_x1 = (src < hm) & (dst == MLA_NOPE + src)
    m_x2 = (src < hm) & (dst == MLA_NOPE + hm + src)
    i_b = src - hm
    in_b = (i_b >= 0) & (i_b < hb)
    b_x1 = in_b & (dst % MOBA_HEAD_DIM == i_b)
    b_x2 = in_b & (dst % MOBA_HEAD_DIM == hb + i_b)

    def expand(m, sign=1.0):
        e = jnp.kron(jnp.eye(ROPE_PACK, dtype=F32), sign * m.astype(F32))
        return jnp.concatenate([e, e, e], axis=0).astype(BF16)

    ec = jnp.stack([expand(m_x1 | m_x2), expand(b_x1 | b_x2)])
    e1 = jnp.stack([expand(m_x1, -1.0), expand(b_x1, -1.0)])
    e2 = jnp.stack([expand(m_x2), expand(b_x2)])
    bias = jnp.stack([jnp.tile(1.0 - jnp.sum((m_x1 | m_x2).astype(F32), axis=0), ROPE_PACK),
                      jnp.tile(1.0 - jnp.sum((b_x1 | b_x2).astype(F32), axis=0), ROPE_PACK)])
    rows = t // ROPE_PACK
    tr = min(rows, 512)
    out = jax.ShapeDtypeStruct((2, rows, ROPE_PACK * LANES), F32)
    tab_spec = pl.BlockSpec((2, tr, ROPE_PACK * LANES), lambda i: (0, i, 0))
    consts = [f, ec, e1, e2, bias]
    tabs = pl.pallas_call(
        _rope_table_kernel,
        grid=(rows // tr,),
        in_specs=[pl.BlockSpec((tr, ROPE_PACK), lambda i: (i, 0))] + [_const_spec(w.shape) for w in consts],
        out_specs=[tab_spec, tab_spec, tab_spec],
        out_shape=[out, out, out],
        compiler_params=_params(("parallel",)),
        name="rope_tables",
    )(positions.reshape(rows, ROPE_PACK), *consts)
    return [tab.reshape(2, t, LANES) for tab in tabs]


def _rope(t, c, s1, s2, half):
    w = t.shape[-1]
    return t * c + pltpu.roll(t, w - half, 1) * s1 + pltpu.roll(t, half, 1) * s2


def _rms(x, g):
    ms = jnp.mean(jnp.square(x), axis=-1, keepdims=True)
    return x * lax.rsqrt(ms + RMS_EPS) * g


def _layer_norm(x, g, b):
    mu = jnp.mean(x, axis=-1, keepdims=True)
    xc = x - mu
    var = jnp.mean(jnp.square(xc), axis=-1, keepdims=True)
    return xc * lax.rsqrt(var + LN_EPS) * g + b


ATT_PROJ_TM = 256


def _att_proj_kernel(x_ref, win_ref, wuq_ref, wk_ref, wv_ref, wvb_ref, qn_ref, kvn_ref, c_ref, s1_ref, s2_ref,
                     qm_ref, km_ref, vm_ref, qb_ref, kb_ref, vb_ref, kmean_ref):
    xb = x_ref[...].astype(BF16)
    cm, s1m, s2m = c_ref[0], s1_ref[0], s2_ref[0]
    cb, s1b, s2b = c_ref[1], s1_ref[1], s2_ref[1]
    hm, hb = MLA_ROPE // 2, MOBA_ROT // 2
    mla_scale = (MLA_NOPE + MLA_ROPE) ** -0.5 * LOG2_E
    moba_scale = MOBA_HEAD_DIM ** -0.5 * LOG2_E

    h_lat = _dot(xb, win_ref[:, 0:512])
    c_q = h_lat[:, 0:MLA_Q_LORA]
    c_kv = h_lat[:, MLA_Q_LORA:MLA_Q_LORA + MLA_KV_LORA]
    hq = _dot(xb, win_ref[:, 512:1024])
    hk = _dot(xb, win_ref[:, 1024:1536])
    vb_ref[...] = _dot_nt(wvb_ref[...], xb).astype(BF16)
    k_rope = _rope(h_lat[:, 384:512], cm, s1m, s2m, hm)

    q = _dot(_rms(c_q, qn_ref[...]).astype(BF16), wuq_ref[...])
    for h in range(MLA_HEADS):
        sl = slice(h * HEAD_PAD, (h + 1) * HEAD_PAD)
        qm_ref[:, sl] = (_rope(q[:, sl], cm, s1m, s2m, hm) * mla_scale).astype(BF16)

    ckv = _rms(c_kv, kvn_ref[...]).astype(BF16)
    k = _dot(ckv, wk_ref[...])
    for h in range(MLA_HEADS):
        sl = slice(h * HEAD_PAD, (h + 1) * HEAD_PAD)
        km_ref[:, sl] = (k[:, sl] + k_rope).astype(BF16)
    vm_ref[...] = _dot_nt(wv_ref[...], ckv).astype(BF16)

    for g in range(MOBA_WIDTH // LANES):
        sl = slice(g * LANES, (g + 1) * LANES)
        qb_ref[:, sl] = (_rope(hq[:, sl], cb, s1b, s2b, hb) * moba_scale).astype(BF16)
        kr = _rope(hk[:, sl], cb, s1b, s2b, hb)
        kb_ref[:, sl] = kr.astype(BF16)
        for r in range(ATT_PROJ_TM // MOBA_BLOCK):
            rows = slice(r * MOBA_BLOCK, (r + 1) * MOBA_BLOCK)
            kmean_ref[r, :, sl] = jnp.mean(kr[rows], axis=0, keepdims=True)


def _att_proj(x2, tabs, w_in, w_uq, w_k, w_v_t, w_vb_t, q_norm, kv_norm):
    t = x2.shape[0]
    tm = ATT_PROJ_TM
    c, s1, s2 = tabs
    row = lambda w: pl.BlockSpec((tm, w), lambda i: (i, 0))
    col = lambda h: pl.BlockSpec((h, tm), lambda i: (0, i))
    tab = pl.BlockSpec((2, tm, LANES), lambda i: (0, i, 0))
    bf = lambda w: jax.ShapeDtypeStruct((t, w), BF16)
    bf_t = lambda h: jax.ShapeDtypeStruct((h, t), BF16)
    nblk = tm // MOBA_BLOCK
    consts = [w_in, w_uq, w_k, w_v_t, w_vb_t, q_norm, kv_norm]
    return pl.pallas_call(
        _att_proj_kernel,
        grid=(t // tm,),
        in_specs=[row(D_MODEL)] + [_const_spec(w.shape) for w in consts] + [tab, tab, tab],
        out_specs=[row(1024), row(1024), col(512), row(512), row(512), col(512),
                   pl.BlockSpec((nblk, 1, MOBA_WIDTH), lambda i: (i, 0, 0))],
        out_shape=[bf(1024), bf(1024), bf_t(512), bf(512), bf(512), bf_t(512),
                   jax.ShapeDtypeStruct((t // MOBA_BLOCK, 1, MOBA_WIDTH), F32)],
        compiler_params=_params(("parallel",)),
        name="att_proj",
    )(x2, *consts, c, s1, s2)


def _moba_select_kernel(q_ref, kab_ref, r_ref, lc_ref, qa_ref, *, nb):
    own = pl.program_id(1)
    lane = lax.broadcasted_iota(jnp.int32, (1, LANES), 1)
    pair_valid = jnp.where(lc_ref[0:1, :] < own.astype(F32), 1.0, 0.0)
    tie_first = lc_ref[1:2, :]
    low = lane < (LANES // 2)
    lt_own = lane < own
    rest_pen = jnp.where((lane > own) & (lane < nb), NEG_BIG, 0.0)
    gates = []
    for h in range(MOBA_HEADS):
        qp = q_ref[:, (h // 2) * LANES:(h // 2 + 1) * LANES]
        qm = jnp.where(low if h % 2 == 0 else jnp.logical_not(low), qp, jnp.zeros_like(qp))
        qa_ref[:, 2 * h * LANES:(2 * h + 1) * LANES] = qm
        gates.append(_dot(qm, kab_ref[0, h]))
    beats = []
    for ab in gates:
        a, b = ab[:, 0:LANES], ab[:, LANES:2 * LANES]
        beats.append((jnp.where(b > a, 1.0, jnp.where(b == a, tie_first, 0.0)) * pair_valid).astype(BF16))
    counts = [_dot(bt, r_ref[...]) for bt in beats]
    for h, cnt in enumerate(counts):
        pen = jnp.where(lt_own, jnp.where(cnt < MOBA_TOPK, 0.0, NEG_BIG), rest_pen)
        qa_ref[:, (2 * h + 1) * LANES:(2 * h + 2) * LANES] = pen.astype(BF16)


def _moba_select(qb, kmean, batch, seq):
    t = qb.shape[0]
    nb = seq // MOBA_BLOCK
    assert nb * nb <= LANES
    km = kmean.reshape(batch, nb, MOBA_HEADS, MOBA_HEAD_DIM).transpose(0, 2, 3, 1)
    a = jnp.repeat(km, nb, axis=-1)
    b = jnp.tile(km, (1, 1, 1, nb))
    zc = jnp.zeros(km.shape[:3] + (LANES - nb * nb,), F32)
    ab = jnp.concatenate([a, zc, b, zc], axis=-1)
    zr = jnp.zeros_like(ab)
    odd = (jnp.arange(MOBA_HEADS) % 2 == 1)[None, :, None, None]
    kab = jnp.where(odd, jnp.concatenate([zr, ab], axis=2), jnp.concatenate([ab, zr], axis=2)).astype(BF16)
    cidx = jnp.arange(LANES)
    rmat = ((cidx[:, None] // nb == cidx[None, :]) & (cidx[:, None] < nb * nb)).astype(BF16)
    used = cidx < nb * nb
    lane_consts = jnp.zeros((SUBLANES, LANES), F32)
    lane_consts = lane_consts.at[0].set(jnp.where(used, cidx % nb, nb).astype(F32))
    lane_consts = lane_consts.at[1].set((used & (cidx % nb < cidx // nb)).astype(F32))
    tq = MOBA_BLOCK
    return pl.pallas_call(
        functools.partial(_moba_select_kernel, nb=nb),
        grid=(batch, seq // tq),
        in_specs=[pl.BlockSpec((tq, MOBA_WIDTH), lambda bi, qi: (bi * (seq // tq) + qi, 0)),
                  pl.BlockSpec((1, MOBA_HEADS, LANES, 2 * LANES), lambda bi, qi: (bi, 0, 0, 0)),
                  _const_spec((LANES, LANES)), _const_spec((SUBLANES, LANES))],
        out_specs=pl.BlockSpec((tq, MOBA_HEADS * 2 * LANES), lambda bi, qi: (bi * (seq // tq) + qi, 0)),
        out_shape=jax.ShapeDtypeStruct((t, MOBA_HEADS * 2 * LANES), BF16),
        compiler_params=_params(("parallel", "parallel")),
        name="moba_select",
    )(qb, kab, rmat, lane_consts)


ATT_TILE = 256
ATT_PAIRS = 4


def _attn_kernel(*refs, moba):
    if moba:
        q_ref, k_ref, oh_ref, v_ref, o_ref, va_ref, vb_ref = refs
    else:
        q_ref, k_ref, v_ref, o_ref, va_ref, vb_ref = refs
        oh_ref = None
    seq = q_ref.shape[0]
    tq = ATT_TILE
    pairs = v_ref.shape[0] // LANES
    heads = 2 * pairs
    qw = q_ref.shape[1] // heads
    sub = lax.broadcasted_iota(jnp.int32, (LANES, 1), 0)
    half = LANES // 2
    for p in range(pairs):
        ps = slice(p * LANES, (p + 1) * LANES)
        v = v_ref[ps, :].astype(F32)
        va_ref[ps, :] = jnp.where(sub < half, v, jnp.where(sub == half, 1.0, 0.0)).astype(BF16)
        vb_ref[ps, :] = jnp.where(sub >= half, v, jnp.where(sub == 0, 1.0, 0.0)).astype(BF16)
    key = lax.broadcasted_iota(jnp.int32, (tq, tq), 0)
    qry = lax.broadcasted_iota(jnp.int32, (tq, tq), 1)
    causal = key <= qry

    def load_q(r0, h):
        return q_ref[pl.ds(r0, tq), h * qw:(h + 1) * qw]

    def load_k(c0, h):
        if moba:
            ps = slice((h // 2) * LANES, (h // 2 + 1) * LANES)
            return jnp.concatenate([k_ref[pl.ds(c0, tq), ps], oh_ref[pl.ds(c0, tq), :]], axis=1)
        return k_ref[pl.ds(c0, tq), h * HEAD_PAD:(h + 1) * HEAD_PAD]

    def load_v(c0, h):
        ps = slice((h // 2) * LANES, (h // 2 + 1) * LANES)
        return (va_ref if h % 2 == 0 else vb_ref)[ps, pl.ds(c0, tq)]

    def scores_of(r0, c0):
        return [_dot_nt(load_k(c0, h), load_q(r0, h)) for h in range(heads)]

    def value_update(c0, accs, alphas, ps):
        return tuple(alphas[h] * accs[h] + _dot(load_v(c0, h), ps[h]) for h in range(heads))

    def softmax_update(scores, ms):
        out = []
        for h in range(heads):
            m_new = jnp.maximum(ms[h], jnp.max(scores[h], axis=0, keepdims=True))
            alpha = jnp.exp2(ms[h] - m_new)
            p = jnp.exp2((scores[h] - m_new).astype(BF16))
            out.append((m_new, alpha, p))
        return tuple(zip(*out))

    def q_tile(qi):
        r0 = qi * tq

        rep = lambda a: (a,) * heads
        scores = [jnp.where(causal, s_t, -jnp.inf) for s_t in scores_of(r0, r0)]
        ms, alphas, ps = softmax_update(scores, rep(jnp.full((1, tq), -jnp.inf, F32)))

        def prev_tile(j):
            if isinstance(j, int):
                return (qi if j == 0 else j - 1) * tq
            return pl.multiple_of(jnp.where(j == 0, qi, j - 1) * tq, tq)

        def body(j, carry):
            ms, accs, alphas, ps = carry
            scores = scores_of(r0, pl.multiple_of(j * tq, tq))
            accs = value_update(prev_tile(j), accs, alphas, ps)
            ms, alphas, ps = softmax_update(scores, ms)
            return ms, accs, alphas, ps

        init = (ms, rep(jnp.zeros((LANES, tq), F32)), alphas, ps)
        ms, accs, alphas, ps = lax.fori_loop(0, qi, body, init)
        accs = value_update(prev_tile(qi), accs, alphas, ps)
        for p in range(pairs):
            a0, a1 = accs[2 * p], accs[2 * p + 1]
            o_t = jnp.where(sub < half, a0 * (1.0 / a0[half:half + 1, :]), a1 * (1.0 / a1[0:1, :]))
            o_ref[pl.ds(r0, tq), p * LANES:(p + 1) * LANES] = o_t.T.astype(o_ref.dtype)

    for qi in range(seq // tq):
        q_tile(qi)


def _attention(q, k, v_t, batch, seq, onehot=None):
    t = q.shape[0]
    moba = onehot is not None
    pairs = v_t.shape[0] // LANES
    steps = pairs // ATT_PAIRS
    blk = lambda a: pl.BlockSpec((seq, a.shape[1] // steps), lambda bi, p: (bi, p))
    vw = ATT_PAIRS * LANES
    in_specs = [blk(q), blk(k)]
    args = [q, k]
    if moba:
        in_specs.append(_const_spec((seq, LANES)))
        args.append(onehot)
    in_specs.append(pl.BlockSpec((vw, seq), lambda bi, p: (p, bi)))
    args.append(v_t)
    return pl.pallas_call(
        functools.partial(_attn_kernel, moba=moba),
        grid=(batch, steps),
        in_specs=in_specs,
        out_specs=pl.BlockSpec((seq, vw), lambda bi, p: (bi, p)),
        out_shape=jax.ShapeDtypeStruct((t, pairs * LANES), BF16),
        scratch_shapes=[pltpu.VMEM((vw, seq), BF16), pltpu.VMEM((vw, seq), BF16)],
        compiler_params=_params(("parallel", "parallel")),
        name="moba_attn" if moba else "mla_attn",
    )(*args)


OUT_PROJ_TM = 512
OUT_PROJ_SPLIT = 4


def _out_proj_kernel(*refs, n_act):
    acts = refs[:n_act]
    ws = refs[n_act:2 * n_act]
    x_ref, g_ref, b_ref, o_ref = refs[2 * n_act:]
    sub = OUT_PROJ_TM // OUT_PROJ_SPLIT
    mixes = []
    for r in range(OUT_PROJ_SPLIT):
        rows = slice(r * sub, (r + 1) * sub)
        m = _dot(acts[0][rows, :], ws[0][...])
        for a_ref, w_ref in zip(acts[1:], ws[1:]):
            m = m + _dot(a_ref[rows, :], w_ref[...])
        mixes.append(m)
    for r in range(OUT_PROJ_SPLIT):
        rows = slice(r * sub, (r + 1) * sub)
        o_ref[rows, :] = _layer_norm(DEEPNORM_ALPHA * x_ref[rows, :] + mixes[r], g_ref[...], b_ref[...])


def _out_proj_ln(acts, ws, x2, g, b):
    t = x2.shape[0]
    tm = OUT_PROJ_TM
    row = lambda w: pl.BlockSpec((tm, w), lambda i: (i, 0))
    return pl.pallas_call(
        functools.partial(_out_proj_kernel, n_act=len(acts)),
        grid=(t // tm,),
        in_specs=[row(a.shape[1]) for a in acts] + [_const_spec(w.shape) for w in ws]
        + [row(D_MODEL), _const_spec(g.shape), _const_spec(b.shape)],
        out_specs=row(D_MODEL),
        out_shape=jax.ShapeDtypeStruct((t, D_MODEL), F32),
        compiler_params=_params(("parallel",)),
        name="out_proj_ln",
    )(*acts, *ws, x2, g, b)


FFN_TM = 256
FFN_CHUNK = 256


def _ffn_kernel(x_ref, p_ref, wup_ref, cw_ref, cb_ref, wdn_ref, g_ref, b_ref, wg_ref, wp_ref, o_ref, gbuf_ref,
                *, tiles_per_seq):
    tm = FFN_TM
    hist = SUBLANES

    @pl.when(pl.program_id(0) % tiles_per_seq == 0)
    def _():
        gbuf_ref[0:hist, :] = jnp.zeros((hist, D_FF), F32)

    x = x_ref[...]
    xb = x.astype(BF16)
    n_chunks = D_FF // FFN_CHUNK

    def up_proj(c):
        lo = c * FFN_CHUNK
        return (_dot(xb, wup_ref[:, lo:lo + FFN_CHUNK]), _dot(xb, wup_ref[:, D_FF + lo:D_FF + lo + FFN_CHUNK]))

    f = jnp.zeros((tm, D_MODEL), F32)
    nxt = up_proj(0)
    for c in range(n_chunks):
        cs = slice(c * FFN_CHUNK, (c + 1) * FFN_CHUNK)
        gate, up = nxt
        if c + 1 < n_chunks:
            nxt = up_proj(c + 1)
        gbuf_ref[hist:hist + tm, cs] = gate
        conv = cw_ref[FFN_CONV - 1:FFN_CONV, cs] * gate + cb_ref[:, cs]
        for k in range(FFN_CONV - 1):
            d = FFN_CONV - 1 - k
            conv = conv + cw_ref[k:k + 1, cs] * gbuf_ref[hist - d:hist - d + tm, cs]
        hid = 0.5 * conv * (1.0 + lax.erf(conv * INV_SQRT2)) * up
        f = f + _dot(hid.astype(BF16), wdn_ref[cs, :])
    proj = _dot(p_ref[...].astype(BF16), wp_ref[...])
    gbuf_ref[0:hist, :] = gbuf_ref[tm:tm + hist, :]
    half = tm // 2
    for r in range(2):
        rows = slice(r * half, (r + 1) * half)
        y = _layer_norm(DEEPNORM_ALPHA * x[rows] + f[rows], g_ref[...], b_ref[...])
        gate = jax.nn.sigmoid(_dot(y.astype(BF16), wg_ref[...]))
        o_ref[rows, :] = y + gate * proj[rows]


def _ffn_ln_ple(x2, p_all, layer, w_up, conv_w, conv_b, w_down, g, b, w_gate, w_proj, seq):
    t = x2.shape[0]
    tm = FFN_TM
    row = lambda w: pl.BlockSpec((tm, w), lambda i: (i, 0))
    p_spec = pl.BlockSpec((tm, PLE_DIM), lambda i: (layer * (t // tm) + i, 0))
    consts = [w_up, conv_w, conv_b, w_down, g, b, w_gate, w_proj]
    return pl.pallas_call(
        functools.partial(_ffn_kernel, tiles_per_seq=seq // tm),
        grid=(t // tm,),
        in_specs=[row(D_MODEL), p_spec] + [_const_spec(w.shape) for w in consts],
        out_specs=row(D_MODEL),
        out_shape=jax.ShapeDtypeStruct((t, D_MODEL), F32),
        scratch_shapes=[pltpu.VMEM((tm + SUBLANES, D_FF), F32)],
        compiler_params=_params(("arbitrary",)),
        name="ffn_ln_ple",
    )(x2, p_all, *consts)


SSD_PROJ_TM = 256
SSD_PROJ_CHUNK = 256
SSD_BC = SSD_GROUPS * SSD_STATE


def _ssd_proj_kernel(x_ref, win_ref, cw_ref, cb_ref, dtb_ref, a_ref, tri_ref,
                     z_ref, xs_ref, b_ref, c_ref, dt_ref, la_ref, lat_ref, cbuf_ref, xb_ref, *, tiles_per_seq):
    tm = SSD_PROJ_TM
    hist = SUBLANES
    ck = SSD_PROJ_CHUNK

    @pl.when(pl.program_id(0) % tiles_per_seq == 0)
    def _():
        cbuf_ref[0:hist, :] = jnp.zeros((hist, SSD_CONV_DIM), F32)

    xb_ref[...] = x_ref[...].astype(BF16)
    hd = _dot(xb_ref[...], win_ref[:, SSD_INNER + SSD_CONV_DIM:]) + dtb_ref[...]
    dt = jnp.maximum(hd, 0.0) + jnp.log1p(jnp.exp(-jnp.abs(hd)))
    dt_ref[...] = dt
    la = jnp.dot(tri_ref[...], dt * a_ref[...], preferred_element_type=F32, precision=lax.Precision.HIGHEST) * LOG2_E
    la_ref[...] = la
    lat_ref[...] = la.T
    for c in range(SSD_INNER // ck):
        z_ref[:, c * ck:(c + 1) * ck] = _dot(xb_ref[...], win_ref[:, c * ck:(c + 1) * ck]).astype(z_ref.dtype)

    def conv_in(c):
        return _dot(xb_ref[...], win_ref[:, SSD_INNER + c * ck:SSD_INNER + (c + 1) * ck])

    n_conv = SSD_CONV_DIM // ck
    nxt = conv_in(0)
    for c in range(n_conv):
        cs = slice(c * ck, (c + 1) * ck)
        h = nxt
        if c + 1 < n_conv:
            nxt = conv_in(c + 1)
        cbuf_ref[hist:hist + tm, cs] = h
        conv = cw_ref[SSD_CONV - 1:SSD_CONV, cs] * h + cb_ref[:, cs]
        for k in range(SSD_CONV - 1):
            d = SSD_CONV - 1 - k
            conv = conv + cw_ref[k:k + 1, cs] * cbuf_ref[hist - d:hist - d + tm, cs]
        act = (conv * jax.nn.sigmoid(conv)).astype(xs_ref.dtype)
        lo = c * ck
        if lo < SSD_INNER:
            xs_ref[:, cs] = act
        elif lo < SSD_INNER + SSD_BC:
            b_ref[:, lo - SSD_INNER:lo - SSD_INNER + ck] = act
        else:
            c_ref[:, lo - SSD_INNER - SSD_BC:lo - SSD_INNER - SSD_BC + ck] = act
    cbuf_ref[0:hist, :] = cbuf_ref[tm:tm + hist, :]


def _ssd_proj(x2, w_in, conv_w, conv_b, dt_bias, a_row, seq):
    t = x2.shape[0]
    tm = SSD_PROJ_TM
    assert SSD_BC % SSD_PROJ_CHUNK == 0 and SSD_INNER % SSD_PROJ_CHUNK == 0 and tm % SSD_CHUNK == 0
    row = lambda w: pl.BlockSpec((tm, w), lambda i: (i, 0))
    out = lambda w, dt: jax.ShapeDtypeStruct((t, w), dt)
    idx = jnp.arange(tm)
    tri = ((idx[:, None] // SSD_CHUNK == idx[None, :] // SSD_CHUNK) & (idx[None, :] <= idx[:, None])).astype(F32)
    consts = [w_in, conv_w, conv_b, dt_bias, a_row, tri]
    return pl.pallas_call(
        functools.partial(_ssd_proj_kernel, tiles_per_seq=seq // tm),
        grid=(t // tm,),
        in_specs=[row(D_MODEL)] + [_const_spec(w.shape) for w in consts],
        out_specs=[row(SSD_INNER), row(SSD_INNER), row(SSD_BC), row(SSD_BC), row(LANES), row(LANES),
                   pl.BlockSpec((LANES, tm), lambda i: (0, i))],
        out_shape=[out(SSD_INNER, BF16), out(SSD_INNER, BF16), out(SSD_BC, BF16), out(SSD_BC, BF16),
                   out(LANES, F32), out(LANES, F32), jax.ShapeDtypeStruct((LANES, t), F32)],
        scratch_shapes=[pltpu.VMEM((tm + SUBLANES, SSD_CONV_DIM), F32), pltpu.VMEM((tm, D_MODEL), BF16)],
        compiler_params=_params(("arbitrary",)),
        name="ssd_proj",
    )(x2, *consts)


def _ssd_scan_kernel(xs_ref, z_ref, b_ref, c_ref, dt_ref, la_ref, lat_ref, d_ref, nw_ref, o_ref, state_ref):
    L = SSD_CHUNK
    hg = SSD_HEADS // SSD_GROUPS
    gw = hg * SSD_HEAD_DIM

    @pl.when(pl.program_id(1) == 0)
    def _():
        state_ref[...] = jnp.zeros_like(state_ref)

    lane = lax.broadcasted_iota(jnp.int32, (1, LANES), 1)
    low = lane < LANES // 2
    row = lax.broadcasted_iota(jnp.int32, (L, L), 0)
    col = lax.broadcasted_iota(jnp.int32, (L, L), 1)
    causal = col <= row

    dt = dt_ref[...]
    la = la_ref[...]
    la_t = lat_ref[...]

    for g in range(SSD_GROUPS):
        bg = b_ref[:, g * SSD_STATE:(g + 1) * SSD_STATE]
        cg = c_ref[:, g * SSD_STATE:(g + 1) * SSD_STATE]
        cb = _dot_nt(cg, bg)
        bg_t = bg.astype(F32).T.astype(BF16)
        la_bc = [jnp.broadcast_to(la[:, h:h + 1], (L, LANES)) for h in range(g * hg, (g + 1) * hg)]
        dt_bc = [jnp.broadcast_to(dt[:, h:h + 1], (L, LANES)) for h in range(g * hg, (g + 1) * hg)]
        y_parts = []
        for pp in range(hg // 2):
            h0 = g * hg + 2 * pp
            ps = slice(h0 * SSD_HEAD_DIM, (h0 + 2) * SSD_HEAD_DIM)
            xs = xs_ref[:, ps].astype(F32)
            dt_pair = jnp.where(low, dt_bc[2 * pp], dt_bc[2 * pp + 1])
            la_pair = jnp.where(low, la_bc[2 * pp], la_bc[2 * pp + 1])
            xdt = xs * dt_pair
            xdt_b = xdt.astype(BF16)
            y = jnp.zeros((L, LANES), F32)
            for hh in range(2):
                seg = la_bc[2 * pp + hh] - la_t[h0 + hh:h0 + hh + 1, :]
                w = (cb * jnp.exp2(jnp.where(causal, seg, -jnp.inf))).astype(BF16)
                keep = low if hh == 0 else jnp.logical_not(low)
                y = y + _dot(w, jnp.where(keep, xdt_b, jnp.zeros_like(xdt_b)))
            st = state_ref[:, ps]
            y = y + _dot(cg, st.astype(BF16)) * jnp.exp2(la_pair)
            la_end = la_pair[L - 1:L, :]
            to_end = jnp.exp2(la_end - la_pair)
            state_ref[:, ps] = st * jnp.exp2(la_end) + _dot(bg_t, (xdt * to_end).astype(BF16))
            y = y + d_ref[:, ps] * xs
            zz = z_ref[:, ps].astype(F32)
            y_parts.append(y * (zz * jax.nn.sigmoid(zz)))
        yg = jnp.concatenate(y_parts, axis=1)
        ms = jnp.mean(jnp.square(yg), axis=-1, keepdims=True)
        gs = slice(g * gw, (g + 1) * gw)
        o_ref[:, gs] = (yg * lax.rsqrt(ms + RMS_EPS) * nw_ref[:, gs]).astype(o_ref.dtype)


def _ssd_scan(xs, z, bm, cm, dt, la, la_t, d_row, norm_w, batch, seq):
    t = xs.shape[0]
    L = SSD_CHUNK
    nc = seq // L
    assert SSD_INNER // SSD_GROUPS == SSD_BC
    row = lambda w: pl.BlockSpec((L, w), lambda bi, ci: (bi * nc + ci, 0))
    consts = [d_row, norm_w]
    return pl.pallas_call(
        _ssd_scan_kernel,
        grid=(batch, nc),
        in_specs=[row(SSD_INNER), row(SSD_INNER), row(SSD_BC), row(SSD_BC), row(LANES), row(LANES),
                  pl.BlockSpec((LANES, L), lambda bi, ci: (0, bi * nc + ci))]
        + [_const_spec(w.shape) for w in consts],
        out_specs=row(SSD_INNER),
        out_shape=jax.ShapeDtypeStruct((t, SSD_INNER), BF16),
        scratch_shapes=[pltpu.VMEM((SSD_STATE, SSD_INNER), F32)],
        compiler_params=_params(("parallel", "arbitrary")),
        name="ssd_scan",
    )(xs, z, bm, cm, dt, la, la_t, *consts)


def _pad_cols(w, n):
    return jnp.concatenate([w, jnp.zeros((w.shape[0], n), w.dtype)], axis=1)


def _attention_layer(x2, tabs, batch, seq, w_in, q_norm, w_uq, kv_norm, w_ukv, w_out):
    o1 = MLA_Q_LORA + MLA_KV_LORA
    zc = lambda n: jnp.zeros((D_MODEL, n), w_in.dtype)
    o2 = o1 + MLA_ROPE + 2 * MOBA_WIDTH
    w_in2 = jnp.concatenate([w_in[:, :o1], zc(MLA_NOPE), w_in[:, o1:o1 + MLA_ROPE],
                             zc(LANES - MLA_NOPE - MLA_ROPE), w_in[:, o1 + MLA_ROPE:o2]], axis=1).astype(BF16)
    w_vb_t = w_in[:, o2:].T.astype(BF16)
    dqk = MLA_NOPE + MLA_ROPE
    w_uq2 = jnp.pad(w_uq.reshape(MLA_Q_LORA, MLA_HEADS, dqk), ((0, 0), (0, 0), (0, HEAD_PAD - dqk)))
    w_uq2 = w_uq2.reshape(MLA_Q_LORA, MLA_HEADS * HEAD_PAD).astype(BF16)
    w_kv3 = w_ukv.reshape(MLA_KV_LORA, MLA_HEADS, MLA_NOPE + MLA_V)
    w_k2 = jnp.pad(w_kv3[:, :, :MLA_NOPE], ((0, 0), (0, 0), (0, HEAD_PAD - MLA_NOPE)))
    w_k2 = w_k2.reshape(MLA_KV_LORA, MLA_HEADS * HEAD_PAD).astype(BF16)
    w_v_t = w_kv3[:, :, MLA_NOPE:].reshape(MLA_KV_LORA, MLA_HEADS * MLA_V).T.astype(BF16)

    qm, km, vm, qb, kb, vb, kmean = _att_proj(x2, tabs, w_in2, w_uq2, w_k2, w_v_t, w_vb_t,
                                              q_norm.reshape(1, -1), kv_norm.reshape(1, -1))
    o_mla = _attention(qm, km, vm, batch, seq)
    qa = _moba_select(qb, kmean, batch, seq)
    pos = jnp.arange(seq)
    onehot = (pos[:, None] // MOBA_BLOCK == jnp.arange(LANES)[None, :]).astype(BF16)
    o_moba = _attention(qa, kb, vb, batch, seq, onehot=onehot)
    n_mla = MLA_HEADS * MLA_V
    w_o = w_out.astype(BF16)
    return [o_mla, o_moba], [w_o[:n_mla], w_o[n_mla:]]


def _ssd_layer(x2, batch, seq, w_in, conv_w, conv_b, dt_bias, a_log, d_skip, norm_w, w_out):
    w_in2 = _pad_cols(w_in, LANES - SSD_HEADS).astype(BF16)
    dtb = _pad_cols(dt_bias.reshape(1, -1), LANES - SSD_HEADS)
    a_row = _pad_cols((-jnp.exp(a_log.astype(F32))).reshape(1, -1), LANES - SSD_HEADS)
    z, xs, bm, cm, dt, la, la_t = _ssd_proj(x2, w_in2, conv_w, conv_b.reshape(1, -1), dtb, a_row, seq)
    d_row = jnp.repeat(d_skip.astype(F32), SSD_HEAD_DIM).reshape(1, -1)
    y = _ssd_scan(xs, z, bm, cm, dt, la, la_t, d_row, norm_w.reshape(1, -1), batch, seq)
    return [y], [w_out.astype(BF16)]


def kernel(x, p, positions, att_w_in, mla_q_norm, mla_w_uq, mla_kv_norm, mla_w_ukv, att_w_out, ssd_w_in, ssd_conv_w, ssd_conv_b, ssd_dt_bias, ssd_a_log, ssd_d, ssd_norm, ssd_w_out, ln_mix_g, ln_mix_b, ffn_w_up, ffn_conv_w, ffn_conv_b, ffn_w_down, ln_ffn_g, ln_ffn_b, ple_w_gate, ple_w_proj):
    batch, seq, _ = x.shape
    depth = p.shape[0]
    t = batch * seq
    x2 = x.reshape(t, D_MODEL)
    tabs = _rope_tables(positions)
    for i in range(depth):
        j = i // 2
        if i % 2 == 0:
            acts, ws = _attention_layer(x2, tabs, batch, seq, att_w_in[j], mla_q_norm[j], mla_w_uq[j],
                                        mla_kv_norm[j], mla_w_ukv[j], att_w_out[j])
        else:
            acts, ws = _ssd_layer(x2, batch, seq, ssd_w_in[j], ssd_conv_w[j], ssd_conv_b[j], ssd_dt_bias[j],
                                  ssd_a_log[j], ssd_d[j], ssd_norm[j], ssd_w_out[j])
        x2 = _out_proj_ln(acts, ws, x2, ln_mix_g[i].reshape(1, -1), ln_mix_b[i].reshape(1, -1))
        x2 = _ffn_ln_ple(x2, p.reshape(depth * t, PLE_DIM), i, ffn_w_up[i].astype(BF16), ffn_conv_w[i],
                         ffn_conv_b[i].reshape(1, -1), ffn_w_down[i].astype(BF16),
                         ln_ffn_g[i].reshape(1, -1), ln_ffn_b[i].reshape(1, -1),
                         ple_w_gate[i].astype(BF16), ple_w_proj[i].astype(BF16), seq)
    return x2.reshape(batch, seq, D_MODEL)
```

```python
import functools
import math

import jax
import jax.numpy as jnp
from jax import lax
from jax.experimental import pallas as pl
from jax.experimental.pallas import tpu as pltpu

D_MODEL = 1024
PLE_DIM = 256
ROPE_THETA = 500000.0
MLA_HEADS = 8
MLA_Q_LORA = 256
MLA_KV_LORA = 128
MLA_NOPE = 64
MLA_ROPE = 32
MLA_V = 64
MOBA_HEADS = 8
MOBA_HEAD_DIM = 64
MOBA_ROT = MOBA_HEAD_DIM // 4
MOBA_BLOCK = 256
MOBA_TOPK = 3
MOBA_WIDTH = MOBA_HEADS * MOBA_HEAD_DIM
SSD_INNER = 2 * D_MODEL
SSD_HEAD_DIM = 64
SSD_HEADS = SSD_INNER // SSD_HEAD_DIM
SSD_GROUPS = 4
SSD_STATE = 128
SSD_CONV = 4
SSD_CHUNK = 128
SSD_CONV_DIM = SSD_INNER + 2 * SSD_GROUPS * SSD_STATE
D_FF = 2816
FFN_CONV = 3
LN_EPS = 1e-5
RMS_EPS = 1e-6
DEPTH = 2
DEEPNORM_ALPHA = (2 * DEPTH) ** 0.25

LANES = 128
SUBLANES = 8
HEAD_PAD = LANES
VMEM_LIMIT = 56 * 1024 * 1024
NEG_BIG = -1e30
INV_SQRT2 = 0.7071067811865476
LOG2_E = 1.4426950408889634

F32 = jnp.float32
BF16 = jnp.bfloat16


def _dot(a, b):
    return jnp.dot(a, b, preferred_element_type=F32)


def _dot_nt(a, b):
    return lax.dot_general(a, b, (((1,), (1,)), ((), ())), preferred_element_type=F32)


def _const_spec(shape):
    return pl.BlockSpec(shape, lambda *_: (0,) * len(shape), pipeline_mode=pl.Buffered(1))


def _params(semantics):
    return pltpu.CompilerParams(dimension_semantics=semantics, vmem_limit_bytes=VMEM_LIMIT)


ROPE_PACK = 4


def _split3(a):
    hi = a.astype(BF16)
    r1 = a - hi.astype(F32)
    mid = r1.astype(BF16)
    lo = (r1 - mid.astype(F32)).astype(BF16)
    return jnp.concatenate([hi, mid, lo], axis=1)


def _rope_table_kernel(pos_ref, f_ref, ec_ref, e1_ref, e2_ref, bias_ref, c_ref, s1_ref, s2_ref):
    pos = pos_ref[...].astype(F32)
    ang = pos[:, 0:1] * f_ref[0:1, :]
    for u in range(1, ROPE_PACK):
        ang = ang + pos[:, u:u + 1] * f_ref[u:u + 1, :]
    cos3, sin3 = _split3(jnp.cos(ang)), _split3(jnp.sin(ang))
    rows = pos_ref.shape[0]
    for r in range(2):
        c = _dot(cos3, ec_ref[r]) + bias_ref[r:r + 1, :]
        s1 = _dot(sin3, e1_ref[r])
        s2 = _dot(sin3, e2_ref[r])
        for u in range(ROPE_PACK):
            ls = slice(u * LANES, (u + 1) * LANES)
            dst = pl.ds(u, rows, stride=ROPE_PACK)
            c_ref[r, dst, :] = c[:, ls]
            s1_ref[r, dst, :] = s1[:, ls]
            s2_ref[r, dst, :] = s2[:, ls]


def _rope_tables(positions):
    t = positions.size
    inv_m = ROPE_THETA ** (-jnp.arange(0, MLA_ROPE, 2, dtype=F32) / MLA_ROPE)
    inv_b = ROPE_THETA ** (-jnp.arange(0, MOBA_ROT, 2, dtype=F32) / MOBA_ROT)
    hm, hb = MLA_ROPE // 2, MOBA_ROT // 2
    slot = LANES // ROPE_PACK
    assert hm + hb <= slot and t % ROPE_PACK == 0
    f_tok = jnp.concatenate([inv_m, inv_b, jnp.zeros((slot - hm - hb,), F32)])
    f = jnp.kron(jnp.eye(ROPE_PACK, dtype=F32), f_tok[None, :])
    src = jnp.arange(slot)[:, None]
    dst = jnp.arange(LANES)[None, :]
    m_x1 = (src < hm) & (dst == MLA_NOPE + src)
    m_x2 = (src < hm) & (dst == MLA_NOPE + hm + src)
    i_b = src - hm
    in_b = (i_b >= 0) & (i_b < hb)
    b_x1 = in_b & (dst % MOBA_HEAD_DIM == i_b)
    b_x2 = in_b & (dst % MOBA_HEAD_DIM == hb + i_b)

    def expand(m, sign=1.0):
        e = jnp.kron(jnp.eye(ROPE_PACK, dtype=F32), sign * m.astype(F32))
        return jnp.concatenate([e, e, e], axis=0).astype(BF16)

    ec = jnp.stack([expand(m_x1 | m_x2), expand(b_x1 | b_x2)])
    e1 = jnp.stack([expand(m_x1, -1.0), expand(b_x1, -1.0)])
    e2 = jnp.stack([expand(m_x2), expand(b_x2)])
    bias = jnp.stack([jnp.tile(1.0 - jnp.sum((m_x1 | m_x2).astype(F32), axis=0), ROPE_PACK),
                      jnp.tile(1.0 - jnp.sum((b_x1 | b_x2).astype(F32), axis=0), ROPE_PACK)])
    rows = t // ROPE_PACK
    tr = min(rows, 512)
    out = jax.ShapeDtypeStruct((2, t, LANES), F32)
    tab_spec = pl.BlockSpec((2, tr * ROPE_PACK, LANES), lambda i: (0, i, 0))
    consts = [f, ec, e1, e2, bias]
    return pl.pallas_call(
        _rope_table_kernel,
        grid=(rows // tr,),
        in_specs=[pl.BlockSpec((tr, ROPE_PACK), lambda i: (i, 0))] + [_const_spec(w.shape) for w in consts],
        out_specs=[tab_spec, tab_spec, tab_spec],
        out_shape=[out, out, out],
        compiler_params=_params(("parallel",)),
        name="rope_tables",
    )(positions.reshape(rows, ROPE_PACK), *consts)


def _rope(t, c, s1, s2, half):
    w = t.shape[-1]
    return t * c + pltpu.roll(t, w - half, 1) * s1 + pltpu.roll(t, half, 1) * s2


def _rms(x, g):
    ms = jnp.mean(jnp.square(x), axis=-1, keepdims=True)
    return x * lax.rsqrt(ms + RMS_EPS) * g


def _layer_norm(x, g, b):
    mu = jnp.mean(x, axis=-1, keepdims=True)
    xc = x - mu
    var = jnp.mean(jnp.square(xc), axis=-1, keepdims=True)
    return xc * lax.rsqrt(var + LN_EPS) * g + b


ATT_PROJ_TM = 256


def _att_proj_kernel(x_ref, win_ref, wuq_ref, wk_ref, wv_ref, wvb_ref, qn_ref, kvn_ref, c_ref, s1_ref, s2_ref,
                     qm_ref, km_ref, vm_ref, qb_ref, kb_ref, vb_ref, kmean_ref):
    xb = x_ref[...].astype(BF16)
    cm, s1m, s2m = c_ref[0], s1_ref[0], s2_ref[0]
    cb, s1b, s2b = c_ref[1], s1_ref[1], s2_ref[1]
    hm, hb = MLA_ROPE // 2, MOBA_ROT // 2
    mla_scale = (MLA_NOPE + MLA_ROPE) ** -0.5 * LOG2_E
    moba_scale = MOBA_HEAD_DIM ** -0.5 * LOG2_E

    h_lat = _dot(xb, win_ref[:, 0:512])
    c_q = h_lat[:, 0:MLA_Q_LORA]
    c_kv = h_lat[:, MLA_Q_LORA:MLA_Q_LORA + MLA_KV_LORA]
    hq = _dot(xb, win_ref[:, 512:1024])
    hk = _dot(xb, win_ref[:, 1024:1536])
    vb_ref[...] = _dot_nt(wvb_ref[...], xb).astype(BF16)
    k_rope = _rope(h_lat[:, 384:512], cm, s1m, s2m, hm)

    q = _dot(_rms(c_q, qn_ref[...]).astype(BF16), wuq_ref[...])
    for h in range(MLA_HEADS):
        sl = slice(h * HEAD_PAD, (h + 1) * HEAD_PAD)
        qm_ref[:, sl] = (_rope(q[:, sl], cm, s1m, s2m, hm) * mla_scale).astype(BF16)

    ckv = _rms(c_kv, kvn_ref[...]).astype(BF16)
    k = _dot(ckv, wk_ref[...])
    for h in range(MLA_HEADS):
        sl = slice(h * HEAD_PAD, (h + 1) * HEAD_PAD)
        km_ref[:, sl] = (k[:, sl] + k_rope).astype(BF16)
    vm_ref[...] = _dot_nt(wv_ref[...], ckv).astype(BF16)

    for g in range(MOBA_WIDTH // LANES):
        sl = slice(g * LANES, (g + 1) * LANES)
        qb_ref[:, sl] = (_rope(hq[:, sl], cb, s1b, s2b, hb) * moba_scale).astype(BF16)
        kr = _rope(hk[:, sl], cb, s1b, s2b, hb)
        kb_ref[:, sl] = kr.astype(BF16)
        for r in range(ATT_PROJ_TM // MOBA_BLOCK):
            rows = slice(r * MOBA_BLOCK, (r + 1) * MOBA_BLOCK)
            kmean_ref[r, :, sl] = jnp.mean(kr[rows], axis=0, keepdims=True)


def _att_proj(x2, tabs, w_in, w_uq, w_k, w_v_t, w_vb_t, q_norm, kv_norm):
    t = x2.shape[0]
    tm = ATT_PROJ_TM
    c, s1, s2 = tabs
    row = lambda w: pl.BlockSpec((tm, w), lambda i: (i, 0))
    col = lambda h: pl.BlockSpec((h, tm), lambda i: (0, i))
    tab = pl.BlockSpec((2, tm, LANES), lambda i: (0, i, 0))
    bf = lambda w: jax.ShapeDtypeStruct((t, w), BF16)
    bf_t = lambda h: jax.ShapeDtypeStruct((h, t), BF16)
    nblk = tm // MOBA_BLOCK
    consts = [w_in, w_uq, w_k, w_v_t, w_vb_t, q_norm, kv_norm]
    return pl.pallas_call(
        _att_proj_kernel,
        grid=(t // tm,),
        in_specs=[row(D_MODEL)] + [_const_spec(w.shape) for w in consts] + [tab, tab, tab],
        out_specs=[row(1024), row(1024), col(512), row(512), row(512), col(512),
                   pl.BlockSpec((nblk, 1, MOBA_WIDTH), lambda i: (i, 0, 0))],
        out_shape=[bf(1024), bf(1024), bf_t(512), bf(512), bf(512), bf_t(512),
                   jax.ShapeDtypeStruct((t // MOBA_BLOCK, 1, MOBA_WIDTH), F32)],
        compiler_params=_params(("parallel",)),
        name="att_proj",
    )(x2, *consts, c, s1, s2)


def _moba_select_kernel(q_ref, kab_ref, r_ref, lc_ref, qa_ref, *, nb):
    own = pl.program_id(1)
    lane = lax.broadcasted_iota(jnp.int32, (1, LANES), 1)
    pair_valid = jnp.where(lc_ref[0:1, :] < own.astype(F32), 1.0, 0.0)
    tie_first = lc_ref[1:2, :]
    low = lane < (LANES // 2)
    lt_own = lane < own
    rest_pen = jnp.where((lane > own) & (lane < nb), NEG_BIG, 0.0)
    gates = []
    for h in range(MOBA_HEADS):
        qp = q_ref[:, (h // 2) * LANES:(h // 2 + 1) * LANES]
        qm = jnp.where(low if h % 2 == 0 else jnp.logical_not(low), qp, jnp.zeros_like(qp))
        qa_ref[:, 2 * h * LANES:(2 * h + 1) * LANES] = qm
        gates.append(_dot(qm, kab_ref[0, h]))
    beats = []
    for ab in gates:
        a, b = ab[:, 0:LANES], ab[:, LANES:2 * LANES]
        beats.append((jnp.where(b > a, 1.0, jnp.where(b == a, tie_first, 0.0)) * pair_valid).astype(BF16))
    counts = [_dot(bt, r_ref[...]) for bt in beats]
    for h, cnt in enumerate(counts):
        pen = jnp.where(lt_own, jnp.where(cnt < MOBA_TOPK, 0.0, NEG_BIG), rest_pen)
        qa_ref[:, (2 * h + 1) * LANES:(2 * h + 2) * LANES] = pen.astype(BF16)


def _moba_select(qb, kmean, batch, seq):
    t = qb.shape[0]
    nb = seq // MOBA_BLOCK
    assert nb * nb <= LANES
    km = kmean.reshape(batch, nb, MOBA_HEADS, MOBA_HEAD_DIM).transpose(0, 2, 3, 1)
    a = jnp.repeat(km, nb, axis=-1)
    b = jnp.tile(km, (1, 1, 1, nb))
    zc = jnp.zeros(km.shape[:3] + (LANES - nb * nb,), F32)
    ab = jnp.concatenate([a, zc, b, zc], axis=-1)
    zr = jnp.zeros_like(ab)
    odd = (jnp.arange(MOBA_HEADS) % 2 == 1)[None, :, None, None]
    kab = jnp.where(odd, jnp.concatenate([zr, ab], axis=2), jnp.concatenate([ab, zr], axis=2)).astype(BF16)
    cidx = jnp.arange(LANES)
    rmat = ((cidx[:, None] // nb == cidx[None, :]) & (cidx[:, None] < nb * nb)).astype(BF16)
    used = cidx < nb * nb
    lane_consts = jnp.zeros((SUBLANES, LANES), F32)
    lane_consts = lane_consts.at[0].set(jnp.where(used, cidx % nb, nb).astype(F32))
    lane_consts = lane_consts.at[1].set((used & (cidx % nb < cidx // nb)).astype(F32))
    tq = MOBA_BLOCK
    return pl.pallas_call(
        functools.partial(_moba_select_kernel, nb=nb),
        grid=(batch, seq // tq),
        in_specs=[pl.BlockSpec((tq, MOBA_WIDTH), lambda bi, qi: (bi * (seq // tq) + qi, 0)),
                  pl.BlockSpec((1, MOBA_HEADS, LANES, 2 * LANES), lambda bi, qi: (bi, 0, 0, 0)),
                  _const_spec((LANES, LANES)), _const_spec((SUBLANES, LANES))],
        out_specs=pl.BlockSpec((tq, MOBA_HEADS * 2 * LANES), lambda bi, qi: (bi * (seq // tq) + qi, 0)),
        out_shape=jax.ShapeDtypeStruct((t, MOBA_HEADS * 2 * LANES), BF16),
        compiler_params=_params(("parallel", "parallel")),
        name="moba_select",
    )(qb, kab, rmat, lane_consts)


ATT_TILE = 256
ATT_PAIRS = 4


def _attn_kernel(*refs, moba):
    if moba:
        q_ref, k_ref, oh_ref, v_ref, o_ref, va_ref, vb_ref = refs
    else:
        q_ref, k_ref, v_ref, o_ref, va_ref, vb_ref = refs
        oh_ref = None
    seq = q_ref.shape[0]
    tq = ATT_TILE
    pairs = v_ref.shape[0] // LANES
    heads = 2 * pairs
    qw = q_ref.shape[1] // heads
    sub = lax.broadcasted_iota(jnp.int32, (LANES, 1), 0)
    half = LANES // 2
    for p in range(pairs):
        ps = slice(p * LANES, (p + 1) * LANES)
        v = v_ref[ps, :].astype(F32)
        va_ref[ps, :] = jnp.where(sub < half, v, jnp.where(sub == half, 1.0, 0.0)).astype(BF16)
        vb_ref[ps, :] = jnp.where(sub >= half, v, jnp.where(sub == 0, 1.0, 0.0)).astype(BF16)
    key = lax.broadcasted_iota(jnp.int32, (tq, tq), 0)
    qry = lax.broadcasted_iota(jnp.int32, (tq, tq), 1)
    causal = key <= qry

    def load_q(r0, h):
        return q_ref[pl.ds(r0, tq), h * qw:(h + 1) * qw]

    def load_k(c0, h):
        if moba:
            ps = slice((h // 2) * LANES, (h // 2 + 1) * LANES)
            return jnp.concatenate([k_ref[pl.ds(c0, tq), ps], oh_ref[pl.ds(c0, tq), :]], axis=1)
        return k_ref[pl.ds(c0, tq), h * HEAD_PAD:(h + 1) * HEAD_PAD]

    def load_v(c0, h):
        ps = slice((h // 2) * LANES, (h // 2 + 1) * LANES)
        return (va_ref if h % 2 == 0 else vb_ref)[ps, pl.ds(c0, tq)]

    def scores_of(r0, c0):
        return [_dot_nt(load_k(c0, h), load_q(r0, h)) for h in range(heads)]

    def value_update(c0, accs, alphas, ps):
        return tuple(alphas[h] * accs[h] + _dot(load_v(c0, h), ps[h]) for h in range(heads))

    def softmax_update(scores, ms):
        out = []
        for h in range(heads):
            m_new = jnp.maximum(ms[h], jnp.max(scores[h], axis=0, keepdims=True))
            alpha = jnp.exp2(ms[h] - m_new)
            p = jnp.exp2((scores[h] - m_new).astype(BF16))
            out.append((m_new, alpha, p))
        return tuple(zip(*out))

    def q_tile(qi):
        r0 = qi * tq

        rep = lambda a: (a,) * heads
        scores = [jnp.where(causal, s_t, -jnp.inf) for s_t in scores_of(r0, r0)]
        ms, alphas, ps = softmax_update(scores, rep(jnp.full((1, tq), -jnp.inf, F32)))

        def prev_tile(j):
            if isinstance(j, int):
                return (qi if j == 0 else j - 1) * tq
            return pl.multiple_of(jnp.where(j == 0, qi, j - 1) * tq, tq)

        def body(j, carry):
            ms, accs, alphas, ps = carry
            scores = scores_of(r0, pl.multiple_of(j * tq, tq))
            accs = value_update(prev_tile(j), accs, alphas, ps)
            ms, alphas, ps = softmax_update(scores, ms)
            return ms, accs, alphas, ps

        init = (ms, rep(jnp.zeros((LANES, tq), F32)), alphas, ps)
        ms, accs, alphas, ps = lax.fori_loop(0, qi, body, init)
        accs = value_update(prev_tile(qi), accs, alphas, ps)
        for p in range(pairs):
            a0, a1 = accs[2 * p], accs[2 * p + 1]
            o_t = jnp.where(sub < half, a0 * (1.0 / a0[half:half + 1, :]), a1 * (1.0 / a1[0:1, :]))
            o_ref[pl.ds(r0, tq), p * LANES:(p + 1) * LANES] = o_t.T.astype(o_ref.dtype)

    for qi in range(seq // tq):
        q_tile(qi)


def _attention(q, k, v_t, batch, seq, onehot=None):
    t = q.shape[0]
    moba = onehot is not None
    pairs = v_t.shape[0] // LANES
    steps = pairs // ATT_PAIRS
    blk = lambda a: pl.BlockSpec((seq, a.shape[1] // steps), lambda bi, p: (bi, p))
    vw = ATT_PAIRS * LANES
    in_specs = [blk(q), blk(k)]
    args = [q, k]
    if moba:
        in_specs.append(_const_spec((seq, LANES)))
        args.append(onehot)
    in_specs.append(pl.BlockSpec((vw, seq), lambda bi, p: (p, bi)))
    args.append(v_t)
    return pl.pallas_call(
        functools.partial(_attn_kernel, moba=moba),
        grid=(batch, steps),
        in_specs=in_specs,
        out_specs=pl.BlockSpec((seq, vw), lambda bi, p: (bi, p)),
        out_shape=jax.ShapeDtypeStruct((t, pairs * LANES), BF16),
        scratch_shapes=[pltpu.VMEM((vw, seq), BF16), pltpu.VMEM((vw, seq), BF16)],
        compiler_params=_params(("parallel", "parallel")),
        name="moba_attn" if moba else "mla_attn",
    )(*args)


OUT_PROJ_TM = 512
OUT_PROJ_SPLIT = 4


def _out_proj_kernel(*refs, n_act):
    acts = refs[:n_act]
    ws = refs[n_act:2 * n_act]
    x_ref, g_ref, b_ref, o_ref = refs[2 * n_act:]
    sub = OUT_PROJ_TM // OUT_PROJ_SPLIT
    mixes = []
    for r in range(OUT_PROJ_SPLIT):
        rows = slice(r * sub, (r + 1) * sub)
        m = _dot(acts[0][rows, :], ws[0][...])
        for a_ref, w_ref in zip(acts[1:], ws[1:]):
            m = m + _dot(a_ref[rows, :], w_ref[...])
        mixes.append(m)
    for r in range(OUT_PROJ_SPLIT):
        rows = slice(r * sub, (r + 1) * sub)
        o_ref[rows, :] = _layer_norm(DEEPNORM_ALPHA * x_ref[rows, :] + mixes[r], g_ref[...], b_ref[...])


def _out_proj_ln(acts, ws, x2, g, b):
    t = x2.shape[0]
    tm = OUT_PROJ_TM
    row = lambda w: pl.BlockSpec((tm, w), lambda i: (i, 0))
    return pl.pallas_call(
        functools.partial(_out_proj_kernel, n_act=len(acts)),
        grid=(t // tm,),
        in_specs=[row(a.shape[1]) for a in acts] + [_const_spec(w.shape) for w in ws]
        + [row(D_MODEL), _const_spec(g.shape), _const_spec(b.shape)],
        out_specs=row(D_MODEL),
        out_shape=jax.ShapeDtypeStruct((t, D_MODEL), F32),
        compiler_params=_params(("parallel",)),
        name="out_proj_ln",
    )(*acts, *ws, x2, g, b)


FFN_TM = 512
FFN_SUB = 256
FFN_CHUNK = 256


def _ffn_kernel(x_ref, p_ref, wup_ref, cw_ref, cb_ref, wdn_ref, g_ref, b_ref, wg_ref, wp_ref, o_ref, gbuf_ref,
                *, tiles_per_seq):
    hist = SUBLANES
    sub = FFN_SUB
    n_chunks = D_FF // FFN_CHUNK

    @pl.when(pl.program_id(0) % tiles_per_seq == 0)
    def _():
        gbuf_ref[0:hist, :] = jnp.zeros((hist, D_FF), F32)

    for s in range(FFN_TM // sub):
        r0 = s * sub
        x = x_ref[r0:r0 + sub, :]
        xb = x.astype(BF16)

        def up_proj(c):
            lo = c * FFN_CHUNK
            return (_dot(xb, wup_ref[:, lo:lo + FFN_CHUNK]),
                    _dot(xb, wup_ref[:, D_FF + lo:D_FF + lo + FFN_CHUNK]))

        f = jnp.zeros((sub, D_MODEL), F32)
        nxt = up_proj(0)
        for c in range(n_chunks):
            cs = slice(c * FFN_CHUNK, (c + 1) * FFN_CHUNK)
            gate, up = nxt
            if c + 1 < n_chunks:
                nxt = up_proj(c + 1)
            g0 = hist + r0
            gbuf_ref[g0:g0 + sub, cs] = gate
            conv = cw_ref[FFN_CONV - 1:FFN_CONV, cs] * gate + cb_ref[:, cs]
            for k in range(FFN_CONV - 1):
                d = FFN_CONV - 1 - k
                conv = conv + cw_ref[k:k + 1, cs] * gbuf_ref[g0 - d:g0 - d + sub, cs]
            hid = 0.5 * conv * (1.0 + lax.erf(conv * INV_SQRT2)) * up
            f = f + _dot(hid.astype(BF16), wdn_ref[cs, :])
        proj = _dot(p_ref[r0:r0 + sub, :].astype(BF16), wp_ref[...])
        half = sub // 2
        for r in range(2):
            rows = slice(r * half, (r + 1) * half)
            y = _layer_norm(DEEPNORM_ALPHA * x[rows] + f[rows], g_ref[...], b_ref[...])
            gate = jax.nn.sigmoid(_dot(y.astype(BF16), wg_ref[...]))
            o_ref[r0 + r * half:r0 + (r + 1) * half, :] = y + gate * proj[rows]
    gbuf_ref[0:hist, :] = gbuf_ref[FFN_TM:FFN_TM + hist, :]


def _ffn_ln_ple(x2, p_all, layer, w_up, conv_w, conv_b, w_down, g, b, w_gate, w_proj, seq):
    t = x2.shape[0]
    tm = FFN_TM
    row = lambda w: pl.BlockSpec((tm, w), lambda i: (i, 0))
    p_spec = pl.BlockSpec((tm, PLE_DIM), lambda i: (layer * (t // tm) + i, 0))
    consts = [w_up, conv_w, conv_b, w_down, g, b, w_gate, w_proj]
    return pl.pallas_call(
        functools.partial(_ffn_kernel, tiles_per_seq=seq // tm),
        grid=(t // tm,),
        in_specs=[row(D_MODEL), p_spec] + [_const_spec(w.shape) for w in consts],
        out_specs=row(D_MODEL),
        out_shape=jax.ShapeDtypeStruct((t, D_MODEL), F32),
        scratch_shapes=[pltpu.VMEM((tm + SUBLANES, D_FF), F32)],
        compiler_params=_params(("arbitrary",)),
        name="ffn_ln_ple",
    )(x2, p_all, *consts)


SSD_PROJ_TM = 512
SSD_PROJ_SUB = 256
SSD_PROJ_CHUNK = 256
SSD_BC = SSD_GROUPS * SSD_STATE


def _ssd_proj_kernel(x_ref, win_ref, cw_ref, cb_ref, dtb_ref, a_ref, tri_ref,
                     z_ref, xs_ref, b_ref, c_ref, dt_ref, la_ref, lat_ref, cbuf_ref, xb_ref, *, tiles_per_seq):
    tm = SSD_PROJ_TM
    sub = SSD_PROJ_SUB
    hist = SUBLANES
    ck = SSD_PROJ_CHUNK
    n_conv = SSD_CONV_DIM // ck

    @pl.when(pl.program_id(0) % tiles_per_seq == 0)
    def _():
        cbuf_ref[0:hist, :] = jnp.zeros((hist, SSD_CONV_DIM), F32)

    xb_ref[...] = x_ref[...].astype(BF16)
    for s in range(tm // sub):
        rs = slice(s * sub, (s + 1) * sub)
        hd = _dot(xb_ref[rs, :], win_ref[:, SSD_INNER + SSD_CONV_DIM:]) + dtb_ref[...]
        dt = jnp.maximum(hd, 0.0) + jnp.log1p(jnp.exp(-jnp.abs(hd)))
        dt_ref[rs, :] = dt
        la = jnp.dot(tri_ref[...], dt * a_ref[...], preferred_element_type=F32,
                     precision=lax.Precision.HIGHEST) * LOG2_E
        la_ref[rs, :] = la
        lat_ref[:, rs] = la.T
        for c in range(SSD_INNER // ck):
            z_ref[rs, c * ck:(c + 1) * ck] = _dot(xb_ref[rs, :], win_ref[:, c * ck:(c + 1) * ck]).astype(z_ref.dtype)

        def conv_in(c):
            return _dot(xb_ref[rs, :], win_ref[:, SSD_INNER + c * ck:SSD_INNER + (c + 1) * ck])

        nxt = conv_in(0)
        for c in range(n_conv):
            cs = slice(c * ck, (c + 1) * ck)
            h = nxt
            if c + 1 < n_conv:
                nxt = conv_in(c + 1)
            g0 = hist + s * sub
            cbuf_ref[g0:g0 + sub, cs] = h
            conv = cw_ref[SSD_CONV - 1:SSD_CONV, cs] * h + cb_ref[:, cs]
            for k in range(SSD_CONV - 1):
                d = SSD_CONV - 1 - k
                conv = conv + cw_ref[k:k + 1, cs] * cbuf_ref[g0 - d:g0 - d + sub, cs]
            act = (conv * jax.nn.sigmoid(conv)).astype(xs_ref.dtype)
            lo = c * ck
            if lo < SSD_INNER:
                xs_ref[rs, cs] = act
            elif lo < SSD_INNER + SSD_BC:
                b_ref[rs, lo - SSD_INNER:lo - SSD_INNER + ck] = act
            else:
                c_ref[rs, lo - SSD_INNER - SSD_BC:lo - SSD_INNER - SSD_BC + ck] = act
    cbuf_ref[0:hist, :] = cbuf_ref[tm:tm + hist, :]


def _ssd_proj(x2, w_in, conv_w, conv_b, dt_bias, a_row, seq):
    t = x2.shape[0]
    tm = SSD_PROJ_TM
    assert SSD_BC % SSD_PROJ_CHUNK == 0 and SSD_INNER % SSD_PROJ_CHUNK == 0 and SSD_PROJ_SUB % SSD_CHUNK == 0
    row = lambda w: pl.BlockSpec((tm, w), lambda i: (i, 0))
    out = lambda w, dt: jax.ShapeDtypeStruct((t, w), dt)
    idx = jnp.arange(SSD_PROJ_SUB)
    tri = ((idx[:, None] // SSD_CHUNK == idx[None, :] // SSD_CHUNK) & (idx[None, :] <= idx[:, None])).astype(F32)
    consts = [w_in, conv_w, conv_b, dt_bias, a_row, tri]
    return pl.pallas_call(
        functools.partial(_ssd_proj_kernel, tiles_per_seq=seq // tm),
        grid=(t // tm,),
        in_specs=[row(D_MODEL)] + [_const_spec(w.shape) for w in consts],
        out_specs=[row(SSD_INNER), row(SSD_INNER), row(SSD_BC), row(SSD_BC), row(LANES), row(LANES),
                   pl.BlockSpec((LANES, tm), lambda i: (0, i))],
        out_shape=[out(SSD_INNER, BF16), out(SSD_INNER, BF16), out(SSD_BC, BF16), out(SSD_BC, BF16),
                   out(LANES, F32), out(LANES, F32), jax.ShapeDtypeStruct((LANES, t), F32)],
        scratch_shapes=[pltpu.VMEM((tm + SUBLANES, SSD_CONV_DIM), F32), pltpu.VMEM((tm, D_MODEL), BF16)],
        compiler_params=_params(("arbitrary",)),
        name="ssd_proj",
    )(x2, *consts)


SSD_SCAN_CHUNKS = 2


def _ssd_scan_kernel(xs_ref, z_ref, b_ref, c_ref, dt_ref, la_ref, lat_ref, d_ref, nw_ref, o_ref, state_ref):
    L = SSD_CHUNK

    @pl.when(pl.program_id(1) == 0)
    def _():
        state_ref[...] = jnp.zeros_like(state_ref)

    lane = lax.broadcasted_iota(jnp.int32, (1, LANES), 1)
    low = lane < LANES // 2
    row = lax.broadcasted_iota(jnp.int32, (L, L), 0)
    col = lax.broadcasted_iota(jnp.int32, (L, L), 1)
    causal = col <= row

    for ci in range(xs_ref.shape[0] // L):
        _ssd_chunk(slice(ci * L, (ci + 1) * L), xs_ref, z_ref, b_ref, c_ref, dt_ref, la_ref, lat_ref, d_ref, nw_ref,
                   o_ref, state_ref, low, causal)


def _ssd_chunk(rs, xs_ref, z_ref, b_ref, c_ref, dt_ref, la_ref, lat_ref, d_ref, nw_ref, o_ref, state_ref,
               low, causal):
    L = SSD_CHUNK
    hg = SSD_HEADS // SSD_GROUPS
    gw = hg * SSD_HEAD_DIM
    dt = dt_ref[rs, :]
    la = la_ref[rs, :]
    la_t = lat_ref[:, rs]

    for g in range(SSD_GROUPS):
        bg = b_ref[rs, g * SSD_STATE:(g + 1) * SSD_STATE]
        cg = c_ref[rs, g * SSD_STATE:(g + 1) * SSD_STATE]
        cb = _dot_nt(cg, bg)
        bg_t = bg.astype(F32).T.astype(BF16)
        la_bc = [jnp.broadcast_to(la[:, h:h + 1], (L, LANES)) for h in range(g * hg, (g + 1) * hg)]
        dt_bc = [jnp.broadcast_to(dt[:, h:h + 1], (L, LANES)) for h in range(g * hg, (g + 1) * hg)]
        y_parts = []
        for pp in range(hg // 2):
            h0 = g * hg + 2 * pp
            ps = slice(h0 * SSD_HEAD_DIM, (h0 + 2) * SSD_HEAD_DIM)
            xs = xs_ref[rs, ps].astype(F32)
            dt_pair = jnp.where(low, dt_bc[2 * pp], dt_bc[2 * pp + 1])
            la_pair = jnp.where(low, la_bc[2 * pp], la_bc[2 * pp + 1])
            xdt = xs * dt_pair
            xdt_b = xdt.astype(BF16)
            y = jnp.zeros((L, LANES), F32)
            for hh in range(2):
                seg = la_bc[2 * pp + hh] - la_t[h0 + hh:h0 + hh + 1, :]
                w = (cb * jnp.exp2(jnp.where(causal, seg, -jnp.inf))).astype(BF16)
                keep = low if hh == 0 else jnp.logical_not(low)
                y = y + _dot(w, jnp.where(keep, xdt_b, jnp.zeros_like(xdt_b)))
            st = state_ref[:, ps]
            y = y + _dot(cg, st.astype(BF16)) * jnp.exp2(la_pair)
            la_end = la_pair[L - 1:L, :]
            to_end = jnp.exp2(la_end - la_pair)
            state_ref[:, ps] = st * jnp.exp2(la_end) + _dot(bg_t, (xdt * to_end).astype(BF16))
            y = y + d_ref[:, ps] * xs
            zz = z_ref[rs, ps].astype(F32)
            y_parts.append(y * (zz * jax.nn.sigmoid(zz)))
        yg = jnp.concatenate(y_parts, axis=1)
        ms = jnp.mean(jnp.square(yg), axis=-1, keepdims=True)
        gs = slice(g * gw, (g + 1) * gw)
        o_ref[rs, gs] = (yg * lax.rsqrt(ms + RMS_EPS) * nw_ref[:, gs]).astype(o_ref.dtype)


def _ssd_scan(xs, z, bm, cm, dt, la, la_t, d_row, norm_w, batch, seq):
    t = xs.shape[0]
    rows = min(SSD_SCAN_CHUNKS * SSD_CHUNK, seq)
    nc = seq // rows
    assert SSD_INNER // SSD_GROUPS == SSD_BC
    row = lambda w: pl.BlockSpec((rows, w), lambda bi, ci: (bi * nc + ci, 0))
    consts = [d_row, norm_w]
    return pl.pallas_call(
        _ssd_scan_kernel,
        grid=(batch, nc),
        in_specs=[row(SSD_INNER), row(SSD_INNER), row(SSD_BC), row(SSD_BC), row(LANES), row(LANES),
                  pl.BlockSpec((LANES, rows), lambda bi, ci: (0, bi * nc + ci))]
        + [_const_spec(w.shape) for w in consts],
        out_specs=row(SSD_INNER),
        out_shape=jax.ShapeDtypeStruct((t, SSD_INNER), BF16),
        scratch_shapes=[pltpu.VMEM((SSD_STATE, SSD_INNER), F32)],
        compiler_params=_params(("parallel", "arbitrary")),
        name="ssd_scan",
    )(xs, z, bm, cm, dt, la, la_t, *consts)


def _pad_cols(w, n):
    return jnp.concatenate([w, jnp.zeros((w.shape[0], n), w.dtype)], axis=1)


def _attention_layer(x2, tabs, batch, seq, w_in, q_norm, w_uq, kv_norm, w_ukv, w_out):
    o1 = MLA_Q_LORA + MLA_KV_LORA
    zc = lambda n: jnp.zeros((D_MODEL, n), w_in.dtype)
    o2 = o1 + MLA_ROPE + 2 * MOBA_WIDTH
    w_in2 = jnp.concatenate([w_in[:, :o1], zc(MLA_NOPE), w_in[:, o1:o1 + MLA_ROPE],
                             zc(LANES - MLA_NOPE - MLA_ROPE), w_in[:, o1 + MLA_ROPE:o2]], axis=1).astype(BF16)
    w_vb_t = w_in[:, o2:].T.astype(BF16)
    dqk = MLA_NOPE + MLA_ROPE
    w_uq2 = jnp.pad(w_uq.reshape(MLA_Q_LORA, MLA_HEADS, dqk), ((0, 0), (0, 0), (0, HEAD_PAD - dqk)))
    w_uq2 = w_uq2.reshape(MLA_Q_LORA, MLA_HEADS * HEAD_PAD).astype(BF16)
    w_kv3 = w_ukv.reshape(MLA_KV_LORA, MLA_HEADS, MLA_NOPE + MLA_V)
    w_k2 = jnp.pad(w_kv3[:, :, :MLA_NOPE], ((0, 0), (0, 0), (0, HEAD_PAD - MLA_NOPE)))
    w_k2 = w_k2.reshape(MLA_KV_LORA, MLA_HEADS * HEAD_PAD).astype(BF16)
    w_v_t = w_kv3[:, :, MLA_NOPE:].reshape(MLA_KV_LORA, MLA_HEADS * MLA_V).T.astype(BF16)

    qm, km, vm, qb, kb, vb, kmean = _att_proj(x2, tabs, w_in2, w_uq2, w_k2, w_v_t, w_vb_t,
                                              q_norm.reshape(1, -1), kv_norm.reshape(1, -1))
    o_mla = _attention(qm, km, vm, batch, seq)
    qa = _moba_select(qb, kmean, batch, seq)
    pos = jnp.arange(seq)
    onehot = (pos[:, None] // MOBA_BLOCK == jnp.arange(LANES)[None, :]).astype(BF16)
    o_moba = _attention(qa, kb, vb, batch, seq, onehot=onehot)
    n_mla = MLA_HEADS * MLA_V
    w_o = w_out.astype(BF16)
    return [o_mla, o_moba], [w_o[:n_mla], w_o[n_mla:]]


def _ssd_layer(x2, batch, seq, w_in, conv_w, conv_b, dt_bias, a_log, d_skip, norm_w, w_out):
    w_in2 = _pad_cols(w_in, LANES - SSD_HEADS).astype(BF16)
    dtb = _pad_cols(dt_bias.reshape(1, -1), LANES - SSD_HEADS)
    a_row = _pad_cols((-jnp.exp(a_log.astype(F32))).reshape(1, -1), LANES - SSD_HEADS)
    z, xs, bm, cm, dt, la, la_t = _ssd_proj(x2, w_in2, conv_w, conv_b.reshape(1, -1), dtb, a_row, seq)
    d_row = jnp.repeat(d_skip.astype(F32), SSD_HEAD_DIM).reshape(1, -1)
    y = _ssd_scan(xs, z, bm, cm, dt, la, la_t, d_row, norm_w.reshape(1, -1), batch, seq)
    return [y], [w_out.astype(BF16)]


def kernel(x, p, positions, att_w_in, mla_q_norm, mla_w_uq, mla_kv_norm, mla_w_ukv, att_w_out, ssd_w_in, ssd_conv_w, ssd_conv_b, ssd_dt_bias, ssd_a_log, ssd_d, ssd_norm, ssd_w_out, ln_mix_g, ln_mix_b, ffn_w_up, ffn_conv_w, ffn_conv_b, ffn_w_down, ln_ffn_g, ln_ffn_b, ple_w_gate, ple_w_proj):
    batch, seq, _ = x.shape
    depth = p.shape[0]
    t = batch * seq
    x2 = x.reshape(t, D_MODEL)
    tabs = _rope_tables(positions)
    for i in range(depth):
        j = i // 2
        if i % 2 == 0:
            acts, ws = _attention_layer(x2, tabs, batch, seq, att_w_in[j], mla_q_norm[j], mla_w_uq[j],
                                        mla_kv_norm[j], mla_w_ukv[j], att_w_out[j])
        else:
            acts, ws = _ssd_layer(x2, batch, seq, ssd_w_in[j], ssd_conv_w[j], ssd_conv_b[j], ssd_dt_bias[j],
                                  ssd_a_log[j], ssd_d[j], ssd_norm[j], ssd_w_out[j])
        x2 = _out_proj_ln(acts, ws, x2, ln_mix_g[i].reshape(1, -1), ln_mix_b[i].reshape(1, -1))
        x2 = _ffn_ln_ple(x2, p.reshape(depth * t, PLE_DIM), i, ffn_w_up[i].astype(BF16), ffn_conv_w[i],
                         ffn_conv_b[i].reshape(1, -1), ffn_w_down[i].astype(BF16),
                         ln_ffn_g[i].reshape(1, -1), ln_ffn_b[i].reshape(1, -1),
                         ple_w_gate[i].astype(BF16), ple_w_proj[i].astype(BF16), seq)
    return x2.reshape(batch, seq, D_MODEL)
```

```python
import functools
import math

import jax
import jax.numpy as jnp
from jax import lax
from jax.experimental import pallas as pl
from jax.experimental.pallas import tpu as pltpu

D_MODEL = 1024
PLE_DIM = 256
ROPE_THETA = 500000.0
MLA_HEADS = 8
MLA_Q_LORA = 256
MLA_KV_LORA = 128
MLA_NOPE = 64
MLA_ROPE = 32
MLA_V = 64
MOBA_HEADS = 8
MOBA_HEAD_DIM = 64
MOBA_ROT = MOBA_HEAD_DIM // 4
MOBA_BLOCK = 256
MOBA_TOPK = 3
MOBA_WIDTH = MOBA_HEADS * MOBA_HEAD_DIM
SSD_INNER = 2 * D_MODEL
SSD_HEAD_DIM = 64
SSD_HEADS = SSD_INNER // SSD_HEAD_DIM
SSD_GROUPS = 4
SSD_STATE = 128
SSD_CONV = 4
SSD_CHUNK = 128
SSD_CONV_DIM = SSD_INNER + 2 * SSD_GROUPS * SSD_STATE
D_FF = 2816
FFN_CONV = 3
LN_EPS = 1e-5
RMS_EPS = 1e-6
DEPTH = 2
DEEPNORM_ALPHA = (2 * DEPTH) ** 0.25

LANES = 128
SUBLANES = 8
HEAD_PAD = LANES
VMEM_LIMIT = 56 * 1024 * 1024
NEG_BIG = -1e30
INV_SQRT2 = 0.7071067811865476
LOG2_E = 1.4426950408889634

F32 = jnp.float32
BF16 = jnp.bfloat16


def _dot(a, b):
    return jnp.dot(a, b, preferred_element_type=F32)


def _dot_nt(a, b):
    return lax.dot_general(a, b, (((1,), (1,)), ((), ())), preferred_element_type=F32)


def _const_spec(shape):
    return pl.BlockSpec(shape, lambda *_: (0,) * len(shape), pipeline_mode=pl.Buffered(1))


def _params(semantics):
    return pltpu.CompilerParams(dimension_semantics=semantics, vmem_limit_bytes=VMEM_LIMIT)


ROPE_PACK = 4


def _split3(a):
    hi = a.astype(BF16)
    r1 = a - hi.astype(F32)
    mid = r1.astype(BF16)
    lo = (r1 - mid.astype(F32)).astype(BF16)
    return jnp.concatenate([hi, mid, lo], axis=1)


def _rope_table_kernel(pos_ref, f_ref, ec_ref, e1_ref, e2_ref, bias_ref, c_ref, s1_ref, s2_ref):
    pos = pos_ref[...].astype(F32)
    ang = pos[:, 0:1] * f_ref[0:1, :]
    for u in range(1, ROPE_PACK):
        ang = ang + pos[:, u:u + 1] * f_ref[u:u + 1, :]
    cos3, sin3 = _split3(jnp.cos(ang)), _split3(jnp.sin(ang))
    rows = pos_ref.shape[0]
    for r in range(2):
        c = _dot(cos3, ec_ref[r]) + bias_ref[r:r + 1, :]
        s1 = _dot(sin3, e1_ref[r])
        s2 = _dot(sin3, e2_ref[r])
        for u in range(ROPE_PACK):
            ls = slice(u * LANES, (u + 1) * LANES)
            dst = pl.ds(u, rows, stride=ROPE_PACK)
            c_ref[r, dst, :] = c[:, ls]
            s1_ref[r, dst, :] = s1[:, ls]
            s2_ref[r, dst, :] = s2[:, ls]


def _rope_tables(positions):
    t = positions.size
    inv_m = ROPE_THETA ** (-jnp.arange(0, MLA_ROPE, 2, dtype=F32) / MLA_ROPE)
    inv_b = ROPE_THETA ** (-jnp.arange(0, MOBA_ROT, 2, dtype=F32) / MOBA_ROT)
    hm, hb = MLA_ROPE // 2, MOBA_ROT // 2
    slot = LANES // ROPE_PACK
    assert hm + hb <= slot and t % ROPE_PACK == 0
    f_tok = jnp.concatenate([inv_m, inv_b, jnp.zeros((slot - hm - hb,), F32)])
    f = jnp.kron(jnp.eye(ROPE_PACK, dtype=F32), f_tok[None, :])
    src = jnp.arange(slot)[:, None]
    dst = jnp.arange(LANES)[None, :]
    m_x1 = (src < hm) & (dst == MLA_NOPE + src)
    m_x2 = (src < hm) & (dst == MLA_NOPE + hm + src)
    i_b = src - hm
    in_b = (i_b >= 0) & (i_b < hb)
    b_x1 = in_b & (dst % MOBA_HEAD_DIM == i_b)
    b_x2 = in_b & (dst % MOBA_HEAD_DIM == hb + i_b)

    def expand(m, sign=1.0):
        e = jnp.kron(jnp.eye(ROPE_PACK, dtype=F32), sign * m.astype(F32))
        return jnp.concatenate([e, e, e], axis=0).astype(BF16)

    ec = jnp.stack([expand(m_x1 | m_x2), expand(b_x1 | b_x2)])
    e1 = jnp.stack([expand(m_x1, -1.0), expand(b_x1, -1.0)])
    e2 = jnp.stack([expand(m_x2), expand(b_x2)])
    bias = jnp.stack([jnp.tile(1.0 - jnp.sum((m_x1 | m_x2).astype(F32), axis=0), ROPE_PACK),
                      jnp.tile(1.0 - jnp.sum((b_x1 | b_x2).astype(F32), axis=0), ROPE_PACK)])
    rows = t // ROPE_PACK
    tr = min(rows, 512)
    out = jax.ShapeDtypeStruct((2, t, LANES), F32)
    tab_spec = pl.BlockSpec((2, tr * ROPE_PACK, LANES), lambda i: (0, i, 0))
    consts = [f, ec, e1, e2, bias]
    return pl.pallas_call(
        _rope_table_kernel,
        grid=(rows // tr,),
        in_specs=[pl.BlockSpec((tr, ROPE_PACK), lambda i: (i, 0))] + [_const_spec(w.shape) for w in consts],
        out_specs=[tab_spec, tab_spec, tab_spec],
        out_shape=[out, out, out],
        compiler_params=_params(("parallel",)),
        name="rope_tables",
    )(positions.reshape(rows, ROPE_PACK), *consts)


def _rope(t, c, s1, s2, half):
    w = t.shape[-1]
    return t * c + pltpu.roll(t, w - half, 1) * s1 + pltpu.roll(t, half, 1) * s2


def _rms(x, g):
    ms = jnp.mean(jnp.square(x), axis=-1, keepdims=True)
    return x * lax.rsqrt(ms + RMS_EPS) * g


def _layer_norm(x, g, b):
    mu = jnp.mean(x, axis=-1, keepdims=True)
    xc = x - mu
    var = jnp.mean(jnp.square(xc), axis=-1, keepdims=True)
    return xc * lax.rsqrt(var + LN_EPS) * g + b


ATT_PROJ_TM = 256


def _att_proj_kernel(x_ref, win_ref, wuq_ref, wk_ref, wv_ref, wvb_ref, qn_ref, kvn_ref, c_ref, s1_ref, s2_ref,
                     qm_ref, km_ref, vm_ref, qb_ref, kb_ref, vb_ref, kmean_ref):
    xb = x_ref[...].astype(BF16)
    cm, s1m, s2m = c_ref[0], s1_ref[0], s2_ref[0]
    cb, s1b, s2b = c_ref[1], s1_ref[1], s2_ref[1]
    hm, hb = MLA_ROPE // 2, MOBA_ROT // 2
    mla_scale = (MLA_NOPE + MLA_ROPE) ** -0.5 * LOG2_E
    moba_scale = MOBA_HEAD_DIM ** -0.5 * LOG2_E

    h_lat = _dot(xb, win_ref[:, 0:512])
    c_q = h_lat[:, 0:MLA_Q_LORA]
    c_kv = h_lat[:, MLA_Q_LORA:MLA_Q_LORA + MLA_KV_LORA]
    hq = _dot(xb, win_ref[:, 512:1024])
    hk = _dot(xb, win_ref[:, 1024:1536])
    vb_ref[...] = _dot_nt(wvb_ref[...], xb).astype(BF16)
    k_rope = _rope(h_lat[:, 384:512], cm, s1m, s2m, hm)

    q = _dot(_rms(c_q, qn_ref[...]).astype(BF16), wuq_ref[...])
    for h in range(MLA_HEADS):
        sl = slice(h * HEAD_PAD, (h + 1) * HEAD_PAD)
        qm_ref[:, sl] = (_rope(q[:, sl], cm, s1m, s2m, hm) * mla_scale).astype(BF16)

    ckv = _rms(c_kv, kvn_ref[...]).astype(BF16)
    k = _dot(ckv, wk_ref[...])
    for h in range(MLA_HEADS):
        sl = slice(h * HEAD_PAD, (h + 1) * HEAD_PAD)
        km_ref[:, sl] = (k[:, sl] + k_rope).astype(BF16)
    vm_ref[...] = _dot_nt(wv_ref[...], ckv).astype(BF16)

    for g in range(MOBA_WIDTH // LANES):
        sl = slice(g * LANES, (g + 1) * LANES)
        qb_ref[:, sl] = (_rope(hq[:, sl], cb, s1b, s2b, hb) * moba_scale).astype(BF16)
        kr = _rope(hk[:, sl], cb, s1b, s2b, hb)
        kb_ref[:, sl] = kr.astype(BF16)
        for r in range(ATT_PROJ_TM // MOBA_BLOCK):
            rows = slice(r * MOBA_BLOCK, (r + 1) * MOBA_BLOCK)
            kmean_ref[r, :, sl] = jnp.mean(kr[rows], axis=0, keepdims=True)


def _att_proj(x2, tabs, w_in, w_uq, w_k, w_v_t, w_vb_t, q_norm, kv_norm):
    t = x2.shape[0]
    tm = ATT_PROJ_TM
    c, s1, s2 = tabs
    row = lambda w: pl.BlockSpec((tm, w), lambda i: (i, 0))
    col = lambda h: pl.BlockSpec((h, tm), lambda i: (0, i))
    tab = pl.BlockSpec((2, tm, LANES), lambda i: (0, i, 0))
    bf = lambda w: jax.ShapeDtypeStruct((t, w), BF16)
    bf_t = lambda h: jax.ShapeDtypeStruct((h, t), BF16)
    nblk = tm // MOBA_BLOCK
    consts = [w_in, w_uq, w_k, w_v_t, w_vb_t, q_norm, kv_norm]
    return pl.pallas_call(
        _att_proj_kernel,
        grid=(t // tm,),
        in_specs=[row(D_MODEL)] + [_const_spec(w.shape) for w in consts] + [tab, tab, tab],
        out_specs=[row(1024), row(1024), col(512), row(512), row(512), col(512),
                   pl.BlockSpec((nblk, 1, MOBA_WIDTH), lambda i: (i, 0, 0))],
        out_shape=[bf(1024), bf(1024), bf_t(512), bf(512), bf(512), bf_t(512),
                   jax.ShapeDtypeStruct((t // MOBA_BLOCK, 1, MOBA_WIDTH), F32)],
        compiler_params=_params(("parallel",)),
        name="att_proj",
    )(x2, *consts, c, s1, s2)


def _moba_select_kernel(q_ref, kab_ref, r_ref, lc_ref, qa_ref, *, nb):
    own = pl.program_id(1)
    lane = lax.broadcasted_iota(jnp.int32, (1, LANES), 1)
    pair_valid = jnp.where(lc_ref[0:1, :] < own.astype(F32), 1.0, 0.0)
    tie_first = lc_ref[1:2, :]
    low = lane < (LANES // 2)
    lt_own = lane < own
    rest_pen = jnp.where((lane > own) & (lane < nb), NEG_BIG, 0.0)
    gates = []
    for h in range(MOBA_HEADS):
        qp = q_ref[:, (h // 2) * LANES:(h // 2 + 1) * LANES]
        qm = jnp.where(low if h % 2 == 0 else jnp.logical_not(low), qp, jnp.zeros_like(qp))
        qa_ref[:, 2 * h * LANES:(2 * h + 1) * LANES] = qm
        gates.append(_dot(qm, kab_ref[0, h]))
    beats = []
    for ab in gates:
        a, b = ab[:, 0:LANES], ab[:, LANES:2 * LANES]
        beats.append((jnp.where(b > a, 1.0, jnp.where(b == a, tie_first, 0.0)) * pair_valid).astype(BF16))
    counts = [_dot(bt, r_ref[...]) for bt in beats]
    for h, cnt in enumerate(counts):
        pen = jnp.where(lt_own, jnp.where(cnt < MOBA_TOPK, 0.0, NEG_BIG), rest_pen)
        qa_ref[:, (2 * h + 1) * LANES:(2 * h + 2) * LANES] = pen.astype(BF16)


def _moba_select(qb, kmean, batch, seq):
    t = qb.shape[0]
    nb = seq // MOBA_BLOCK
    assert nb * nb <= LANES
    km = kmean.reshape(batch, nb, MOBA_HEADS, MOBA_HEAD_DIM).transpose(0, 2, 3, 1)
    a = jnp.repeat(km, nb, axis=-1)
    b = jnp.tile(km, (1, 1, 1, nb))
    zc = jnp.zeros(km.shape[:3] + (LANES - nb * nb,), F32)
    ab = jnp.concatenate([a, zc, b, zc], axis=-1)
    zr = jnp.zeros_like(ab)
    odd = (jnp.arange(MOBA_HEADS) % 2 == 1)[None, :, None, None]
    kab = jnp.where(odd, jnp.concatenate([zr, ab], axis=2), jnp.concatenate([ab, zr], axis=2)).astype(BF16)
    cidx = jnp.arange(LANES)
    rmat = ((cidx[:, None] // nb == cidx[None, :]) & (cidx[:, None] < nb * nb)).astype(BF16)
    used = cidx < nb * nb
    lane_consts = jnp.zeros((SUBLANES, LANES), F32)
    lane_consts = lane_consts.at[0].set(jnp.where(used, cidx % nb, nb).astype(F32))
    lane_consts = lane_consts.at[1].set((used & (cidx % nb < cidx // nb)).astype(F32))
    tq = MOBA_BLOCK
    return pl.pallas_call(
        functools.partial(_moba_select_kernel, nb=nb),
        grid=(batch, seq // tq),
        in_specs=[pl.BlockSpec((tq, MOBA_WIDTH), lambda bi, qi: (bi * (seq // tq) + qi, 0)),
                  pl.BlockSpec((1, MOBA_HEADS, LANES, 2 * LANES), lambda bi, qi: (bi, 0, 0, 0)),
                  _const_spec((LANES, LANES)), _const_spec((SUBLANES, LANES))],
        out_specs=pl.BlockSpec((tq, MOBA_HEADS * 2 * LANES), lambda bi, qi: (bi * (seq // tq) + qi, 0)),
        out_shape=jax.ShapeDtypeStruct((t, MOBA_HEADS * 2 * LANES), BF16),
        compiler_params=_params(("parallel", "parallel")),
        name="moba_select",
    )(qb, kab, rmat, lane_consts)


ATT_TILE = 256
ATT_PAIRS = 4


def _attn_kernel(*refs, moba):
    if moba:
        q_ref, k_ref, oh_ref, v_ref, o_ref, va_ref, vb_ref = refs
    else:
        q_ref, k_ref, v_ref, o_ref, va_ref, vb_ref = refs
        oh_ref = None
    seq = q_ref.shape[0]
    tq = ATT_TILE
    pairs = v_ref.shape[0] // LANES
    heads = 2 * pairs
    qw = q_ref.shape[1] // heads
    sub = lax.broadcasted_iota(jnp.int32, (LANES, 1), 0)
    half = LANES // 2
    for p in range(pairs):
        ps = slice(p * LANES, (p + 1) * LANES)
        v = v_ref[ps, :].astype(F32)
        va_ref[ps, :] = jnp.where(sub < half, v, jnp.where(sub == half, 1.0, 0.0)).astype(BF16)
        vb_ref[ps, :] = jnp.where(sub >= half, v, jnp.where(sub == 0, 1.0, 0.0)).astype(BF16)
    key = lax.broadcasted_iota(jnp.int32, (tq, tq), 0)
    qry = lax.broadcasted_iota(jnp.int32, (tq, tq), 1)
    causal = key <= qry

    def load_q(r0, h):
        return q_ref[pl.ds(r0, tq), h * qw:(h + 1) * qw]

    def load_k(c0, h):
        if moba:
            ps = slice((h // 2) * LANES, (h // 2 + 1) * LANES)
            return jnp.concatenate([k_ref[pl.ds(c0, tq), ps], oh_ref[pl.ds(c0, tq), :]], axis=1)
        return k_ref[pl.ds(c0, tq), h * HEAD_PAD:(h + 1) * HEAD_PAD]

    def load_v(c0, h):
        ps = slice((h // 2) * LANES, (h // 2 + 1) * LANES)
        return (va_ref if h % 2 == 0 else vb_ref)[ps, pl.ds(c0, tq)]

    def scores_of(r0, c0):
        return [_dot_nt(load_k(c0, h), load_q(r0, h)) for h in range(heads)]

    def value_update(c0, accs, alphas, ps):
        return tuple(alphas[h] * accs[h] + _dot(load_v(c0, h), ps[h]) for h in range(heads))

    def softmax_update(scores, ms):
        out = []
        for h in range(heads):
            m_new = jnp.maximum(ms[h], jnp.max(scores[h], axis=0, keepdims=True))
            alpha = jnp.exp2(ms[h] - m_new)
            p = jnp.exp2((scores[h] - m_new).astype(BF16))
            out.append((m_new, alpha, p))
        return tuple(zip(*out))

    def q_tile(qi):
        r0 = qi * tq

        rep = lambda a: (a,) * heads
        scores = [jnp.where(causal, s_t, -jnp.inf) for s_t in scores_of(r0, r0)]
        ms, alphas, ps = softmax_update(scores, rep(jnp.full((1, tq), -jnp.inf, F32)))

        def prev_tile(j):
            if isinstance(j, int):
                return (qi if j == 0 else j - 1) * tq
            return pl.multiple_of(jnp.where(j == 0, qi, j - 1) * tq, tq)

        def body(j, carry):
            ms, accs, alphas, ps = carry
            scores = scores_of(r0, pl.multiple_of(j * tq, tq))
            accs = value_update(prev_tile(j), accs, alphas, ps)
            ms, alphas, ps = softmax_update(scores, ms)
            return ms, accs, alphas, ps

        init = (ms, rep(jnp.zeros((LANES, tq), F32)), alphas, ps)
        ms, accs, alphas, ps = lax.fori_loop(0, qi, body, init)
        accs = value_update(prev_tile(qi), accs, alphas, ps)
        for p in range(pairs):
            a0, a1 = accs[2 * p], accs[2 * p + 1]
            o_t = jnp.where(sub < half, a0 * (1.0 / a0[half:half + 1, :]), a1 * (1.0 / a1[0:1, :]))
            o_ref[pl.ds(r0, tq), p * LANES:(p + 1) * LANES] = o_t.T.astype(o_ref.dtype)

    for qi in range(seq // tq):
        q_tile(qi)


def _attention(q, k, v_t, batch, seq, onehot=None):
    t = q.shape[0]
    moba = onehot is not None
    pairs = v_t.shape[0] // LANES
    steps = pairs // ATT_PAIRS
    blk = lambda a: pl.BlockSpec((seq, a.shape[1] // steps), lambda bi, p: (bi, p))
    vw = ATT_PAIRS * LANES
    in_specs = [blk(q), blk(k)]
    args = [q, k]
    if moba:
        in_specs.append(_const_spec((seq, LANES)))
        args.append(onehot)
    in_specs.append(pl.BlockSpec((vw, seq), lambda bi, p: (p, bi)))
    args.append(v_t)
    return pl.pallas_call(
        functools.partial(_attn_kernel, moba=moba),
        grid=(batch, steps),
        in_specs=in_specs,
        out_specs=pl.BlockSpec((seq, vw), lambda bi, p: (bi, p)),
        out_shape=jax.ShapeDtypeStruct((t, pairs * LANES), BF16),
        scratch_shapes=[pltpu.VMEM((vw, seq), BF16), pltpu.VMEM((vw, seq), BF16)],
        compiler_params=_params(("parallel", "parallel")),
        name="moba_attn" if moba else "mla_attn",
    )(*args)


OUT_PROJ_TM = 512
OUT_PROJ_SPLIT = 4


def _out_proj_kernel(*refs, n_act):
    acts = refs[:n_act]
    ws = refs[n_act:2 * n_act]
    x_ref, g_ref, b_ref, o_ref = refs[2 * n_act:]
    sub = OUT_PROJ_TM // OUT_PROJ_SPLIT
    mixes = []
    for r in range(OUT_PROJ_SPLIT):
        rows = slice(r * sub, (r + 1) * sub)
        m = _dot(acts[0][rows, :], ws[0][...])
        for a_ref, w_ref in zip(acts[1:], ws[1:]):
            m = m + _dot(a_ref[rows, :], w_ref[...])
        mixes.append(m)
    for r in range(OUT_PROJ_SPLIT):
        rows = slice(r * sub, (r + 1) * sub)
        o_ref[rows, :] = _layer_norm(DEEPNORM_ALPHA * x_ref[rows, :] + mixes[r], g_ref[...], b_ref[...])


def _out_proj_ln(acts, ws, x2, g, b):
    t = x2.shape[0]
    tm = OUT_PROJ_TM
    row = lambda w: pl.BlockSpec((tm, w), lambda i: (i, 0))
    return pl.pallas_call(
        functools.partial(_out_proj_kernel, n_act=len(acts)),
        grid=(t // tm,),
        in_specs=[row(a.shape[1]) for a in acts] + [_const_spec(w.shape) for w in ws]
        + [row(D_MODEL), _const_spec(g.shape), _const_spec(b.shape)],
        out_specs=row(D_MODEL),
        out_shape=jax.ShapeDtypeStruct((t, D_MODEL), F32),
        compiler_params=_params(("parallel",)),
        name="out_proj_ln",
    )(*acts, *ws, x2, g, b)


FFN_TM = 512
FFN_SUB = 256
FFN_CHUNK = 256


def _ffn_kernel(*refs, n_act, tiles_per_seq):
    acts = refs[:n_act]
    wos = refs[n_act:2 * n_act]
    (x_ref, p_ref, gm_ref, bm_ref, wup_ref, cw_ref, cb_ref, wdn_ref, g_ref, b_ref, wg_ref, wp_ref,
     o_ref, gbuf_ref) = refs[2 * n_act:]
    hist = SUBLANES
    sub = FFN_SUB
    n_sub = FFN_TM // sub
    n_chunks = D_FF // FFN_CHUNK

    @pl.when(pl.program_id(0) % tiles_per_seq == 0)
    def _():
        gbuf_ref[0:hist, :] = jnp.zeros((hist, D_FF), F32)

    mixes = []
    for s in range(n_sub):
        rows = slice(s * sub, (s + 1) * sub)
        m = _dot(acts[0][rows, :], wos[0][...])
        for a_ref, w_ref in zip(acts[1:], wos[1:]):
            m = m + _dot(a_ref[rows, :], w_ref[...])
        mixes.append(m)

    for s in range(n_sub):
        r0 = s * sub
        x = _layer_norm(DEEPNORM_ALPHA * x_ref[r0:r0 + sub, :] + mixes[s], gm_ref[...], bm_ref[...])
        xb = x.astype(BF16)

        def up_proj(c):
            lo = c * FFN_CHUNK
            return (_dot(xb, wup_ref[:, lo:lo + FFN_CHUNK]),
                    _dot(xb, wup_ref[:, D_FF + lo:D_FF + lo + FFN_CHUNK]))

        f = jnp.zeros((sub, D_MODEL), F32)
        nxt = up_proj(0)
        for c in range(n_chunks):
            cs = slice(c * FFN_CHUNK, (c + 1) * FFN_CHUNK)
            gate, up = nxt
            if c + 1 < n_chunks:
                nxt = up_proj(c + 1)
            g0 = hist + r0
            gbuf_ref[g0:g0 + sub, cs] = gate
            conv = cw_ref[FFN_CONV - 1:FFN_CONV, cs] * gate + cb_ref[:, cs]
            for k in range(FFN_CONV - 1):
                d = FFN_CONV - 1 - k
                conv = conv + cw_ref[k:k + 1, cs] * gbuf_ref[g0 - d:g0 - d + sub, cs]
            hid = 0.5 * conv * (1.0 + lax.erf(conv * INV_SQRT2)) * up
            f = f + _dot(hid.astype(BF16), wdn_ref[cs, :])
        proj = _dot(p_ref[r0:r0 + sub, :].astype(BF16), wp_ref[...])
        half = sub // 2
        for r in range(2):
            rows = slice(r * half, (r + 1) * half)
            y = _layer_norm(DEEPNORM_ALPHA * x[rows] + f[rows], g_ref[...], b_ref[...])
            gate = jax.nn.sigmoid(_dot(y.astype(BF16), wg_ref[...]))
            o_ref[r0 + r * half:r0 + (r + 1) * half, :] = y + gate * proj[rows]
    gbuf_ref[0:hist, :] = gbuf_ref[FFN_TM:FFN_TM + hist, :]


def _mix_ffn_ple(acts, w_outs, x2, p_all, layer, g_mix, b_mix, w_up, conv_w, conv_b, w_down, g, b, w_gate, w_proj,
                 seq):
    t = x2.shape[0]
    tm = FFN_TM
    row = lambda w: pl.BlockSpec((tm, w), lambda i: (i, 0))
    p_spec = pl.BlockSpec((tm, PLE_DIM), lambda i: (layer * (t // tm) + i, 0))
    consts = [g_mix, b_mix, w_up, conv_w, conv_b, w_down, g, b, w_gate, w_proj]
    return pl.pallas_call(
        functools.partial(_ffn_kernel, n_act=len(acts), tiles_per_seq=seq // tm),
        grid=(t // tm,),
        in_specs=[row(a.shape[1]) for a in acts] + [_const_spec(w.shape) for w in w_outs]
        + [row(D_MODEL), p_spec] + [_const_spec(w.shape) for w in consts],
        out_specs=row(D_MODEL),
        out_shape=jax.ShapeDtypeStruct((t, D_MODEL), F32),
        scratch_shapes=[pltpu.VMEM((tm + SUBLANES, D_FF), F32)],
        compiler_params=_params(("arbitrary",)),
        name="mix_ffn_ple",
    )(*acts, *w_outs, x2, p_all, *consts)


SSD_PROJ_TM = 256
SSD_PROJ_SUB = 256
SSD_PROJ_CHUNK = 256
SSD_BC = SSD_GROUPS * SSD_STATE


def _ssd_proj_kernel(x_ref, win_ref, cw_ref, cb_ref, dtb_ref, a_ref, tri_ref,
                     z_ref, xs_ref, b_ref, c_ref, dt_ref, la_ref, lat_ref, cbuf_ref, xb_ref, *, tiles_per_seq):
    tm = SSD_PROJ_TM
    sub = SSD_PROJ_SUB
    hist = SUBLANES
    ck = SSD_PROJ_CHUNK
    n_conv = SSD_CONV_DIM // ck

    @pl.when(pl.program_id(0) % tiles_per_seq == 0)
    def _():
        cbuf_ref[0:hist, :] = jnp.zeros((hist, SSD_CONV_DIM), F32)

    xb_ref[...] = x_ref[...].astype(BF16)
    for s in range(tm // sub):
        rs = slice(s * sub, (s + 1) * sub)
        hd = _dot(xb_ref[rs, :], win_ref[:, SSD_INNER + SSD_CONV_DIM:]) + dtb_ref[...]
        dt = jnp.maximum(hd, 0.0) + jnp.log1p(jnp.exp(-jnp.abs(hd)))
        dt_ref[rs, :] = dt
        la = jnp.dot(tri_ref[...], dt * a_ref[...], preferred_element_type=F32,
                     precision=lax.Precision.HIGHEST) * LOG2_E
        la_ref[rs, :] = la
        lat_ref[:, rs] = la.T
        for c in range(SSD_INNER // ck):
            z_ref[rs, c * ck:(c + 1) * ck] = _dot(xb_ref[rs, :], win_ref[:, c * ck:(c + 1) * ck]).astype(z_ref.dtype)

        def conv_in(c):
            return _dot(xb_ref[rs, :], win_ref[:, SSD_INNER + c * ck:SSD_INNER + (c + 1) * ck])

        nxt = conv_in(0)
        for c in range(n_conv):
            cs = slice(c * ck, (c + 1) * ck)
            h = nxt
            if c + 1 < n_conv:
                nxt = conv_in(c + 1)
            g0 = hist + s * sub
            cbuf_ref[g0:g0 + sub, cs] = h
            conv = cw_ref[SSD_CONV - 1:SSD_CONV, cs] * h + cb_ref[:, cs]
            for k in range(SSD_CONV - 1):
                d = SSD_CONV - 1 - k
                conv = conv + cw_ref[k:k + 1, cs] * cbuf_ref[g0 - d:g0 - d + sub, cs]
            act = (conv * jax.nn.sigmoid(conv)).astype(xs_ref.dtype)
            lo = c * ck
            if lo < SSD_INNER:
                xs_ref[rs, cs] = act
            elif lo < SSD_INNER + SSD_BC:
                b_ref[rs, lo - SSD_INNER:lo - SSD_INNER + ck] = act
            else:
                c_ref[rs, lo - SSD_INNER - SSD_BC:lo - SSD_INNER - SSD_BC + ck] = act
    cbuf_ref[0:hist, :] = cbuf_ref[tm:tm + hist, :]


def _ssd_proj(x2, w_in, conv_w, conv_b, dt_bias, a_row, seq):
    t = x2.shape[0]
    tm = SSD_PROJ_TM
    assert SSD_BC % SSD_PROJ_CHUNK == 0 and SSD_INNER % SSD_PROJ_CHUNK == 0 and SSD_PROJ_SUB % SSD_CHUNK == 0
    row = lambda w: pl.BlockSpec((tm, w), lambda i: (i, 0))
    out = lambda w, dt: jax.ShapeDtypeStruct((t, w), dt)
    idx = jnp.arange(SSD_PROJ_SUB)
    tri = ((idx[:, None] // SSD_CHUNK == idx[None, :] // SSD_CHUNK) & (idx[None, :] <= idx[:, None])).astype(F32)
    consts = [w_in, conv_w, conv_b, dt_bias, a_row, tri]
    return pl.pallas_call(
        functools.partial(_ssd_proj_kernel, tiles_per_seq=seq // tm),
        grid=(t // tm,),
        in_specs=[row(D_MODEL)] + [_const_spec(w.shape) for w in consts],
        out_specs=[row(SSD_INNER), row(SSD_INNER), row(SSD_BC), row(SSD_BC), row(LANES), row(LANES),
                   pl.BlockSpec((LANES, tm), lambda i: (0, i))],
        out_shape=[out(SSD_INNER, BF16), out(SSD_INNER, BF16), out(SSD_BC, BF16), out(SSD_BC, BF16),
                   out(LANES, F32), out(LANES, F32), jax.ShapeDtypeStruct((LANES, t), F32)],
        scratch_shapes=[pltpu.VMEM((tm + SUBLANES, SSD_CONV_DIM), F32), pltpu.VMEM((tm, D_MODEL), BF16)],
        compiler_params=_params(("arbitrary",)),
        name="ssd_proj",
    )(x2, *consts)


SSD_SCAN_CHUNKS = 2


def _ssd_scan_kernel(xs_ref, z_ref, b_ref, c_ref, dt_ref, la_ref, lat_ref, d_ref, nw_ref, o_ref, state_ref):
    L = SSD_CHUNK

    @pl.when(pl.program_id(1) == 0)
    def _():
        state_ref[...] = jnp.zeros_like(state_ref)

    lane = lax.broadcasted_iota(jnp.int32, (1, LANES), 1)
    low = lane < LANES // 2
    row = lax.broadcasted_iota(jnp.int32, (L, L), 0)
    col = lax.broadcasted_iota(jnp.int32, (L, L), 1)
    causal = col <= row

    for ci in range(xs_ref.shape[0] // L):
        _ssd_chunk(slice(ci * L, (ci + 1) * L), xs_ref, z_ref, b_ref, c_ref, dt_ref, la_ref, lat_ref, d_ref, nw_ref,
                   o_ref, state_ref, low, causal)


def _ssd_chunk(rs, xs_ref, z_ref, b_ref, c_ref, dt_ref, la_ref, lat_ref, d_ref, nw_ref, o_ref, state_ref,
               low, causal):
    L = SSD_CHUNK
    hg = SSD_HEADS // SSD_GROUPS
    gw = hg * SSD_HEAD_DIM
    dt = dt_ref[rs, :]
    la = la_ref[rs, :]
    la_t = lat_ref[:, rs]

    for g in range(SSD_GROUPS):
        bg = b_ref[rs, g * SSD_STATE:(g + 1) * SSD_STATE]
        cg = c_ref[rs, g * SSD_STATE:(g + 1) * SSD_STATE]
        cb = _dot_nt(cg, bg)
        bg_t = bg.astype(F32).T.astype(BF16)
        la_bc = [jnp.broadcast_to(la[:, h:h + 1], (L, LANES)) for h in range(g * hg, (g + 1) * hg)]
        dt_bc = [jnp.broadcast_to(dt[:, h:h + 1], (L, LANES)) for h in range(g * hg, (g + 1) * hg)]
        y_parts = []
        for pp in range(hg // 2):
            h0 = g * hg + 2 * pp
            ps = slice(h0 * SSD_HEAD_DIM, (h0 + 2) * SSD_HEAD_DIM)
            xs = xs_ref[rs, ps].astype(F32)
            dt_pair = jnp.where(low, dt_bc[2 * pp], dt_bc[2 * pp + 1])
            la_pair = jnp.where(low, la_bc[2 * pp], la_bc[2 * pp + 1])
            xdt = xs * dt_pair
            xdt_b = xdt.astype(BF16)
            y = jnp.zeros((L, LANES), F32)
            for hh in range(2):
                seg = la_bc[2 * pp + hh] - la_t[h0 + hh:h0 + hh + 1, :]
                w = (cb * jnp.exp2(jnp.where(causal, seg, -jnp.inf))).astype(BF16)
                keep = low if hh == 0 else jnp.logical_not(low)
                y = y + _dot(w, jnp.where(keep, xdt_b, jnp.zeros_like(xdt_b)))
            st = state_ref[:, ps]
            y = y + _dot(cg, st.astype(BF16)) * jnp.exp2(la_pair)
            la_end = la_pair[L - 1:L, :]
            to_end = jnp.exp2(la_end - la_pair)
            state_ref[:, ps] = st * jnp.exp2(la_end) + _dot(bg_t, (xdt * to_end).astype(BF16))
            y = y + d_ref[:, ps] * xs
            zz = z_ref[rs, ps].astype(F32)
            y_parts.append(y * (zz * jax.nn.sigmoid(zz)))
        yg = jnp.concatenate(y_parts, axis=1)
        ms = jnp.mean(jnp.square(yg), axis=-1, keepdims=True)
        gs = slice(g * gw, (g + 1) * gw)
        o_ref[rs, gs] = (yg * lax.rsqrt(ms + RMS_EPS) * nw_ref[:, gs]).astype(o_ref.dtype)


def _ssd_scan(xs, z, bm, cm, dt, la, la_t, d_row, norm_w, batch, seq):
    t = xs.shape[0]
    rows = min(SSD_SCAN_CHUNKS * SSD_CHUNK, seq)
    nc = seq // rows
    assert SSD_INNER // SSD_GROUPS == SSD_BC
    row = lambda w: pl.BlockSpec((rows, w), lambda bi, ci: (bi * nc + ci, 0))
    consts = [d_row, norm_w]
    return pl.pallas_call(
        _ssd_scan_kernel,
        grid=(batch, nc),
        in_specs=[row(SSD_INNER), row(SSD_INNER), row(SSD_BC), row(SSD_BC), row(LANES), row(LANES),
                  pl.BlockSpec((LANES, rows), lambda bi, ci: (0, bi * nc + ci))]
        + [_const_spec(w.shape) for w in consts],
        out_specs=row(SSD_INNER),
        out_shape=jax.ShapeDtypeStruct((t, SSD_INNER), BF16),
        scratch_shapes=[pltpu.VMEM((SSD_STATE, SSD_INNER), F32)],
        compiler_params=_params(("parallel", "arbitrary")),
        name="ssd_scan",
    )(xs, z, bm, cm, dt, la, la_t, *consts)


def _pad_cols(w, n):
    return jnp.concatenate([w, jnp.zeros((w.shape[0], n), w.dtype)], axis=1)


def _attention_layer(x2, tabs, batch, seq, w_in, q_norm, w_uq, kv_norm, w_ukv, w_out):
    o1 = MLA_Q_LORA + MLA_KV_LORA
    zc = lambda n: jnp.zeros((D_MODEL, n), w_in.dtype)
    o2 = o1 + MLA_ROPE + 2 * MOBA_WIDTH
    w_in2 = jnp.concatenate([w_in[:, :o1], zc(MLA_NOPE), w_in[:, o1:o1 + MLA_ROPE],
                             zc(LANES - MLA_NOPE - MLA_ROPE), w_in[:, o1 + MLA_ROPE:o2]], axis=1).astype(BF16)
    w_vb_t = w_in[:, o2:].T.astype(BF16)
    dqk = MLA_NOPE + MLA_ROPE
    w_uq2 = jnp.pad(w_uq.reshape(MLA_Q_LORA, MLA_HEADS, dqk), ((0, 0), (0, 0), (0, HEAD_PAD - dqk)))
    w_uq2 = w_uq2.reshape(MLA_Q_LORA, MLA_HEADS * HEAD_PAD).astype(BF16)
    w_kv3 = w_ukv.reshape(MLA_KV_LORA, MLA_HEADS, MLA_NOPE + MLA_V)
    w_k2 = jnp.pad(w_kv3[:, :, :MLA_NOPE], ((0, 0), (0, 0), (0, HEAD_PAD - MLA_NOPE)))
    w_k2 = w_k2.reshape(MLA_KV_LORA, MLA_HEADS * HEAD_PAD).astype(BF16)
    w_v_t = w_kv3[:, :, MLA_NOPE:].reshape(MLA_KV_LORA, MLA_HEADS * MLA_V).T.astype(BF16)

    qm, km, vm, qb, kb, vb, kmean = _att_proj(x2, tabs, w_in2, w_uq2, w_k2, w_v_t, w_vb_t,
                                              q_norm.reshape(1, -1), kv_norm.reshape(1, -1))
    o_mla = _attention(qm, km, vm, batch, seq)
    qa = _moba_select(qb, kmean, batch, seq)
    pos = jnp.arange(seq)
    onehot = (pos[:, None] // MOBA_BLOCK == jnp.arange(LANES)[None, :]).astype(BF16)
    o_moba = _attention(qa, kb, vb, batch, seq, onehot=onehot)
    n_mla = MLA_HEADS * MLA_V
    w_o = w_out.astype(BF16)
    return [o_mla, o_moba], [w_o[:n_mla], w_o[n_mla:]]


def _ssd_layer(x2, batch, seq, w_in, conv_w, conv_b, dt_bias, a_log, d_skip, norm_w, w_out):
    w_in2 = _pad_cols(w_in, LANES - SSD_HEADS).astype(BF16)
    dtb = _pad_cols(dt_bias.reshape(1, -1), LANES - SSD_HEADS)
    a_row = _pad_cols((-jnp.exp(a_log.astype(F32))).reshape(1, -1), LANES - SSD_HEADS)
    z, xs, bm, cm, dt, la, la_t = _ssd_proj(x2, w_in2, conv_w, conv_b.reshape(1, -1), dtb, a_row, seq)
    d_row = jnp.repeat(d_skip.astype(F32), SSD_HEAD_DIM).reshape(1, -1)
    y = _ssd_scan(xs, z, bm, cm, dt, la, la_t, d_row, norm_w.reshape(1, -1), batch, seq)
    return [y], [w_out.astype(BF16)]


def kernel(x, p, positions, att_w_in, mla_q_norm, mla_w_uq, mla_kv_norm, mla_w_ukv, att_w_out, ssd_w_in, ssd_conv_w, ssd_conv_b, ssd_dt_bias, ssd_a_log, ssd_d, ssd_norm, ssd_w_out, ln_mix_g, ln_mix_b, ffn_w_up, ffn_conv_w, ffn_conv_b, ffn_w_down, ln_ffn_g, ln_ffn_b, ple_w_gate, ple_w_proj):
    batch, seq, _ = x.shape
    depth = p.shape[0]
    t = batch * seq
    x2 = x.reshape(t, D_MODEL)
    tabs = _rope_tables(positions)
    for i in range(depth):
        j = i // 2
        if i % 2 == 0:
            acts, ws = _attention_layer(x2, tabs, batch, seq, att_w_in[j], mla_q_norm[j], mla_w_uq[j],
                                        mla_kv_norm[j], mla_w_ukv[j], att_w_out[j])
        else:
            acts, ws = _ssd_layer(x2, batch, seq, ssd_w_in[j], ssd_conv_w[j], ssd_conv_b[j], ssd_dt_bias[j],
                                  ssd_a_log[j], ssd_d[j], ssd_norm[j], ssd_w_out[j])
        x2 = _mix_ffn_ple(acts, ws, x2, p.reshape(depth * t, PLE_DIM), i,
                          ln_mix_g[i].reshape(1, -1), ln_mix_b[i].reshape(1, -1),
                          ffn_w_up[i].astype(BF16), ffn_conv_w[i], ffn_conv_b[i].reshape(1, -1),
                          ffn_w_down[i].astype(BF16), ln_ffn_g[i].reshape(1, -1), ln_ffn_b[i].reshape(1, -1),
                          ple_w_gate[i].astype(BF16), ple_w_proj[i].astype(BF16), seq)
    return x2.reshape(batch, seq, D_MODEL)
```

```python
import functools

import jax
import jax.numpy as jnp
from jax import lax
from jax.experimental import pallas as pl
from jax.experimental.pallas import tpu as pltpu

D_MODEL = 1024
PLE_DIM = 256
ROPE_THETA = 500000.0
MLA_HEADS = 8
MLA_Q_LORA = 256
MLA_KV_LORA = 128
MLA_NOPE = 64
MLA_ROPE = 32
MLA_V = 64
MOBA_HEADS = 8
MOBA_HEAD_DIM = 64
MOBA_ROT = MOBA_HEAD_DIM // 4
MOBA_BLOCK = 256
MOBA_TOPK = 3
MOBA_WIDTH = MOBA_HEADS * MOBA_HEAD_DIM
SSD_INNER = 2 * D_MODEL
SSD_HEAD_DIM = 64
SSD_HEADS = SSD_INNER // SSD_HEAD_DIM
SSD_GROUPS = 4
SSD_STATE = 128
SSD_CONV = 4
SSD_CHUNK = 128
SSD_CONV_DIM = SSD_INNER + 2 * SSD_GROUPS * SSD_STATE
D_FF = 2816
FFN_CONV = 3
LN_EPS = 1e-5
RMS_EPS = 1e-6
DEPTH = 2
DEEPNORM_ALPHA = (2 * DEPTH) ** 0.25

LANES = 128
SUBLANES = 8
HEAD_PAD = LANES
VMEM_LIMIT = 56 * 1024 * 1024
NEG_BIG = -1e30
INV_SQRT2 = 0.7071067811865476
LOG2_E = 1.4426950408889634

F32 = jnp.float32
BF16 = jnp.bfloat16


def _dot(a, b):
    return jnp.dot(a, b, preferred_element_type=F32)


def _dot_nt(a, b):
    return lax.dot_general(a, b, (((1,), (1,)), ((), ())), preferred_element_type=F32)


def _const_spec(shape):
    return pl.BlockSpec(shape, lambda *_: (0,) * len(shape), pipeline_mode=pl.Buffered(1))


def _params(semantics):
    return pltpu.CompilerParams(dimension_semantics=semantics, vmem_limit_bytes=VMEM_LIMIT)


ROPE_PACK = 4


def _split3(a):
    hi = a.astype(BF16)
    r1 = a - hi.astype(F32)
    mid = r1.astype(BF16)
    lo = (r1 - mid.astype(F32)).astype(BF16)
    return jnp.concatenate([hi, mid, lo], axis=1)


def _rope_table_kernel(pos_ref, f_ref, ec_ref, e1_ref, e2_ref, bias_ref, c_ref, s1_ref, s2_ref):
    pos = pos_ref[...].astype(F32)
    ang = pos[:, 0:1] * f_ref[0:1, :]
    for u in range(1, ROPE_PACK):
        ang = ang + pos[:, u:u + 1] * f_ref[u:u + 1, :]
    cos3, sin3 = _split3(jnp.cos(ang)), _split3(jnp.sin(ang))
    rows = pos_ref.shape[0]
    for r in range(2):
        c = _dot(cos3, ec_ref[r]) + bias_ref[r:r + 1, :]
        s1 = _dot(sin3, e1_ref[r])
        s2 = _dot(sin3, e2_ref[r])
        for u in range(ROPE_PACK):
            ls = slice(u * LANES, (u + 1) * LANES)
            dst = pl.ds(u, rows, stride=ROPE_PACK)
            c_ref[r, dst, :] = c[:, ls]
            s1_ref[r, dst, :] = s1[:, ls]
            s2_ref[r, dst, :] = s2[:, ls]


def _rope_tables(positions):
    t = positions.size
    inv_m = ROPE_THETA ** (-jnp.arange(0, MLA_ROPE, 2, dtype=F32) / MLA_ROPE)
    inv_b = ROPE_THETA ** (-jnp.arange(0, MOBA_ROT, 2, dtype=F32) / MOBA_ROT)
    hm, hb = MLA_ROPE // 2, MOBA_ROT // 2
    slot = LANES // ROPE_PACK
    assert hm + hb <= slot and t % ROPE_PACK == 0
    f_tok = jnp.concatenate([inv_m, inv_b, jnp.zeros((slot - hm - hb,), F32)])
    f = jnp.kron(jnp.eye(ROPE_PACK, dtype=F32), f_tok[None, :])
    src = jnp.arange(slot)[:, None]
    dst = jnp.arange(LANES)[None, :]
    m_x1 = (src < hm) & (dst == MLA_NOPE + src)
    m_x2 = (src < hm) & (dst == MLA_NOPE + hm + src)
    i_b = src - hm
    in_b = (i_b >= 0) & (i_b < hb)
    b_x1 = in_b & (dst % MOBA_HEAD_DIM == i_b)
    b_x2 = in_b & (dst % MOBA_HEAD_DIM == hb + i_b)

    def expand(m, sign=1.0):
        e = jnp.kron(jnp.eye(ROPE_PACK, dtype=F32), sign * m.astype(F32))
        return jnp.concatenate([e, e, e], axis=0).astype(BF16)

    ec = jnp.stack([expand(m_x1 | m_x2), expand(b_x1 | b_x2)])
    e1 = jnp.stack([expand(m_x1, -1.0), expand(b_x1, -1.0)])
    e2 = jnp.stack([expand(m_x2), expand(b_x2)])
    bias = jnp.stack([jnp.tile(1.0 - jnp.sum((m_x1 | m_x2).astype(F32), axis=0), ROPE_PACK),
                      jnp.tile(1.0 - jnp.sum((b_x1 | b_x2).astype(F32), axis=0), ROPE_PACK)])
    rows = t // ROPE_PACK
    tr = min(rows, 512)
    out = jax.ShapeDtypeStruct((2, t, LANES), F32)
    tab_spec = pl.BlockSpec((2, tr * ROPE_PACK, LANES), lambda i: (0, i, 0))
    consts = [f, ec, e1, e2, bias]
    return pl.pallas_call(
        _rope_table_kernel,
        grid=(rows // tr,),
        in_specs=[pl.BlockSpec((tr, ROPE_PACK), lambda i: (i, 0))] + [_const_spec(w.shape) for w in consts],
        out_specs=[tab_spec, tab_spec, tab_spec],
        out_shape=[out, out, out],
        compiler_params=_params(("parallel",)),
        name="rope_tables",
    )(positions.reshape(rows, ROPE_PACK), *consts)


def _rope(t, c, s1, s2, half):
    w = t.shape[-1]
    return t * c + pltpu.roll(t, w - half, 1) * s1 + pltpu.roll(t, half, 1) * s2


def _rms(x, g):
    ms = jnp.mean(jnp.square(x), axis=-1, keepdims=True)
    return x * lax.rsqrt(ms + RMS_EPS) * g


def _layer_norm(x, g, b):
    mu = jnp.mean(x, axis=-1, keepdims=True)
    xc = x - mu
    var = jnp.mean(jnp.square(xc), axis=-1, keepdims=True)
    return xc * lax.rsqrt(var + LN_EPS) * g + b


ATT_PROJ_TM = 512


def _att_proj_kernel(x_ref, win_ref, wuq_ref, wk_ref, wv_ref, wvb_ref, qn_ref, kvn_ref, c_ref, s1_ref, s2_ref,
                     qm_ref, km_ref, vm_ref, qb_ref, kb_ref, vb_ref, kmean_ref):
    xb = x_ref[...].astype(BF16)
    cm, s1m, s2m = c_ref[0], s1_ref[0], s2_ref[0]
    cb, s1b, s2b = c_ref[1], s1_ref[1], s2_ref[1]
    hm, hb = MLA_ROPE // 2, MOBA_ROT // 2
    mla_scale = (MLA_NOPE + MLA_ROPE) ** -0.5 * LOG2_E
    moba_scale = MOBA_HEAD_DIM ** -0.5 * LOG2_E

    h_lat = _dot(xb, win_ref[:, 0:512])
    c_q = h_lat[:, 0:MLA_Q_LORA]
    c_kv = h_lat[:, MLA_Q_LORA:MLA_Q_LORA + MLA_KV_LORA]
    hq = _dot(xb, win_ref[:, 512:1024])
    hk = _dot(xb, win_ref[:, 1024:1536])
    vb_ref[...] = _dot_nt(wvb_ref[...], xb).astype(BF16)
    k_rope = _rope(h_lat[:, 384:512], cm, s1m, s2m, hm)

    q = _dot(_rms(c_q, qn_ref[...]).astype(BF16), wuq_ref[...])
    for h in range(MLA_HEADS):
        sl = slice(h * HEAD_PAD, (h + 1) * HEAD_PAD)
        qm_ref[:, sl] = (_rope(q[:, sl], cm, s1m, s2m, hm) * mla_scale).astype(BF16)

    ckv = _rms(c_kv, kvn_ref[...]).astype(BF16)
    k = _dot(ckv, wk_ref[...])
    for h in range(MLA_HEADS):
        sl = slice(h * HEAD_PAD, (h + 1) * HEAD_PAD)
        km_ref[:, sl] = (k[:, sl] + k_rope).astype(BF16)
    vm_ref[...] = _dot_nt(wv_ref[...], ckv).astype(BF16)

    for g in range(MOBA_WIDTH // LANES):
        sl = slice(g * LANES, (g + 1) * LANES)
        qb_ref[:, sl] = (_rope(hq[:, sl], cb, s1b, s2b, hb) * moba_scale).astype(BF16)
        kr = _rope(hk[:, sl], cb, s1b, s2b, hb)
        kb_ref[:, sl] = kr.astype(BF16)
        for r in range(ATT_PROJ_TM // MOBA_BLOCK):
            rows = slice(r * MOBA_BLOCK, (r + 1) * MOBA_BLOCK)
            kmean_ref[r, :, sl] = jnp.mean(kr[rows], axis=0, keepdims=True)


def _att_proj(x2, tabs, w_in, w_uq, w_k, w_v_t, w_vb_t, q_norm, kv_norm):
    t = x2.shape[0]
    tm = ATT_PROJ_TM
    c, s1, s2 = tabs
    row = lambda w: pl.BlockSpec((tm, w), lambda i: (i, 0))
    col = lambda h: pl.BlockSpec((h, tm), lambda i: (0, i))
    tab = pl.BlockSpec((2, tm, LANES), lambda i: (0, i, 0))
    bf = lambda w: jax.ShapeDtypeStruct((t, w), BF16)
    bf_t = lambda h: jax.ShapeDtypeStruct((h, t), BF16)
    nblk = tm // MOBA_BLOCK
    consts = [w_in, w_uq, w_k, w_v_t, w_vb_t, q_norm, kv_norm]
    return pl.pallas_call(
        _att_proj_kernel,
        grid=(t // tm,),
        in_specs=[row(D_MODEL)] + [_const_spec(w.shape) for w in consts] + [tab, tab, tab],
        out_specs=[row(1024), row(1024), col(512), row(512), row(512), col(512),
                   pl.BlockSpec((nblk, 1, MOBA_WIDTH), lambda i: (i, 0, 0))],
        out_shape=[bf(1024), bf(1024), bf_t(512), bf(512), bf(512), bf_t(512),
                   jax.ShapeDtypeStruct((t // MOBA_BLOCK, 1, MOBA_WIDTH), F32)],
        compiler_params=_params(("parallel",)),
        name="att_proj",
    )(x2, *consts, c, s1, s2)


SELECT_BLOCKS = 2


def _moba_select_kernel(q_ref, kab_ref, r_ref, lc_ref, qa_ref, *, nb):
    lane = lax.broadcasted_iota(jnp.int32, (1, LANES), 1)
    tie_first = lc_ref[1:2, :]
    low = lane < (LANES // 2)
    blocks = q_ref.shape[0] // MOBA_BLOCK
    gates = []
    for bk in range(blocks):
        rows = slice(bk * MOBA_BLOCK, (bk + 1) * MOBA_BLOCK)
        for h in range(MOBA_HEADS):
            qp = q_ref[rows, (h // 2) * LANES:(h // 2 + 1) * LANES]
            qm = jnp.where(low if h % 2 == 0 else jnp.logical_not(low), qp, jnp.zeros_like(qp))
            qa_ref[rows, 2 * h * LANES:(2 * h + 1) * LANES] = qm
            gates.append(_dot(qm, kab_ref[0, h]))
    beats = []
    for i, ab in enumerate(gates):
        own = pl.program_id(1) * blocks + i // MOBA_HEADS
        pair_valid = jnp.where(lc_ref[0:1, :] < own.astype(F32), 1.0, 0.0)
        a, b = ab[:, 0:LANES], ab[:, LANES:2 * LANES]
        beats.append((jnp.where(b > a, 1.0, jnp.where(b == a, tie_first, 0.0)) * pair_valid).astype(BF16))
    counts = [_dot(bt, r_ref[...]) for bt in beats]
    for i, cnt in enumerate(counts):
        bk, h = divmod(i, MOBA_HEADS)
        own = pl.program_id(1) * blocks + bk
        rest_pen = jnp.where((lane > own) & (lane < nb), NEG_BIG, 0.0)
        pen = jnp.where(lane < own, jnp.where(cnt < MOBA_TOPK, 0.0, NEG_BIG), rest_pen)
        qa_ref[bk * MOBA_BLOCK:(bk + 1) * MOBA_BLOCK, (2 * h + 1) * LANES:(2 * h + 2) * LANES] = pen.astype(BF16)


def _moba_select(qb, kmean, batch, seq):
    t = qb.shape[0]
    nb = seq // MOBA_BLOCK
    assert nb * nb <= LANES
    km = kmean.reshape(batch, nb, MOBA_HEADS, MOBA_HEAD_DIM).transpose(0, 2, 3, 1)
    a = jnp.repeat(km, nb, axis=-1)
    b = jnp.tile(km, (1, 1, 1, nb))
    zc = jnp.zeros(km.shape[:3] + (LANES - nb * nb,), F32)
    ab = jnp.concatenate([a, zc, b, zc], axis=-1)
    zr = jnp.zeros_like(ab)
    odd = (jnp.arange(MOBA_HEADS) % 2 == 1)[None, :, None, None]
    kab = jnp.where(odd, jnp.concatenate([zr, ab], axis=2), jnp.concatenate([ab, zr], axis=2)).astype(BF16)
    cidx = jnp.arange(LANES)
    rmat = ((cidx[:, None] // nb == cidx[None, :]) & (cidx[:, None] < nb * nb)).astype(BF16)
    used = cidx < nb * nb
    lane_consts = jnp.zeros((SUBLANES, LANES), F32)
    lane_consts = lane_consts.at[0].set(jnp.where(used, cidx % nb, nb).astype(F32))
    lane_consts = lane_consts.at[1].set((used & (cidx % nb < cidx // nb)).astype(F32))
    tq = min(SELECT_BLOCKS, nb) * MOBA_BLOCK
    return pl.pallas_call(
        functools.partial(_moba_select_kernel, nb=nb),
        grid=(batch, seq // tq),
        in_specs=[pl.BlockSpec((tq, MOBA_WIDTH), lambda bi, qi: (bi * (seq // tq) + qi, 0)),
                  pl.BlockSpec((1, MOBA_HEADS, LANES, 2 * LANES), lambda bi, qi: (bi, 0, 0, 0)),
                  _const_spec((LANES, LANES)), _const_spec((SUBLANES, LANES))],
        out_specs=pl.BlockSpec((tq, MOBA_HEADS * 2 * LANES), lambda bi, qi: (bi * (seq // tq) + qi, 0)),
        out_shape=jax.ShapeDtypeStruct((t, MOBA_HEADS * 2 * LANES), BF16),
        compiler_params=_params(("parallel", "parallel")),
        name="moba_select",
    )(qb, kab, rmat, lane_consts)


ATT_TILE = 256
ATT_PAIRS = 4


def _attn_kernel(*refs, moba):
    if moba:
        q_ref, k_ref, oh_ref, v_ref, o_ref, va_ref, vb_ref = refs
    else:
        q_ref, k_ref, v_ref, o_ref, va_ref, vb_ref = refs
        oh_ref = None
    seq = q_ref.shape[0]
    tq = ATT_TILE
    pairs = v_ref.shape[0] // LANES
    heads = 2 * pairs
    qw = q_ref.shape[1] // heads
    sub = lax.broadcasted_iota(jnp.int32, (LANES, 1), 0)
    half = LANES // 2
    for p in range(pairs):
        ps = slice(p * LANES, (p + 1) * LANES)
        v = v_ref[ps, :].astype(F32)
        va_ref[ps, :] = jnp.where(sub < half, v, jnp.where(sub == half, 1.0, 0.0)).astype(BF16)
        vb_ref[ps, :] = jnp.where(sub >= half, v, jnp.where(sub == 0, 1.0, 0.0)).astype(BF16)
    key = lax.broadcasted_iota(jnp.int32, (tq, tq), 0)
    qry = lax.broadcasted_iota(jnp.int32, (tq, tq), 1)
    causal = key <= qry

    def load_q(r0, h):
        return q_ref[pl.ds(r0, tq), h * qw:(h + 1) * qw]

    def load_k(c0, h):
        if moba:
            ps = slice((h // 2) * LANES, (h // 2 + 1) * LANES)
            return jnp.concatenate([k_ref[pl.ds(c0, tq), ps], oh_ref[pl.ds(c0, tq), :]], axis=1)
        return k_ref[pl.ds(c0, tq), h * HEAD_PAD:(h + 1) * HEAD_PAD]

    def load_v(c0, h):
        ps = slice((h // 2) * LANES, (h // 2 + 1) * LANES)
        return (va_ref if h % 2 == 0 else vb_ref)[ps, pl.ds(c0, tq)]

    def scores_of(r0, c0):
        return [_dot_nt(load_k(c0, h), load_q(r0, h)) for h in range(heads)]

    def value_update(c0, accs, alphas, ps):
        return tuple(alphas[h] * accs[h] + _dot(load_v(c0, h), ps[h]) for h in range(heads))

    def softmax_update(scores, ms):
        out = []
        for h in range(heads):
            m_new = jnp.maximum(ms[h], jnp.max(scores[h], axis=0, keepdims=True))
            alpha = jnp.exp2(ms[h] - m_new)
            p = jnp.exp2((scores[h] - m_new).astype(BF16))
            out.append((m_new, alpha, p))
        return tuple(zip(*out))

    def q_tile(qi):
        r0 = qi * tq

        rep = lambda a: (a,) * heads
        scores = [jnp.where(causal, s_t, -jnp.inf) for s_t in scores_of(r0, r0)]
        ms, alphas, ps = softmax_update(scores, rep(jnp.full((1, tq), -jnp.inf, F32)))

        def prev_tile(j):
            if isinstance(j, int):
                return (qi if j == 0 else j - 1) * tq
            return pl.multiple_of(jnp.where(j == 0, qi, j - 1) * tq, tq)

        def body(j, carry):
            ms, accs, alphas, ps = carry
            scores = scores_of(r0, pl.multiple_of(j * tq, tq))
            accs = value_update(prev_tile(j), accs, alphas, ps)
            ms, alphas, ps = softmax_update(scores, ms)
            return ms, accs, alphas, ps

        init = (ms, rep(jnp.zeros((LANES, tq), F32)), alphas, ps)
        ms, accs, alphas, ps = lax.fori_loop(0, qi, body, init)
        accs = value_update(prev_tile(qi), accs, alphas, ps)
        for p in range(pairs):
            a0, a1 = accs[2 * p], accs[2 * p + 1]
            o_t = jnp.where(sub < half, a0 * (1.0 / a0[half:half + 1, :]), a1 * (1.0 / a1[0:1, :]))
            o_ref[pl.ds(r0, tq), p * LANES:(p + 1) * LANES] = o_t.T.astype(o_ref.dtype)

    for qi in range(seq // tq):
        q_tile(qi)


def _attention(q, k, v_t, batch, seq, onehot=None):
    t = q.shape[0]
    moba = onehot is not None
    pairs = v_t.shape[0] // LANES
    steps = pairs // ATT_PAIRS
    blk = lambda a: pl.BlockSpec((seq, a.shape[1] // steps), lambda bi, p: (bi, p))
    vw = ATT_PAIRS * LANES
    in_specs = [blk(q), blk(k)]
    args = [q, k]
    if moba:
        in_specs.append(_const_spec((seq, LANES)))
        args.append(onehot)
    in_specs.append(pl.BlockSpec((vw, seq), lambda bi, p: (p, bi)))
    args.append(v_t)
    return pl.pallas_call(
        functools.partial(_attn_kernel, moba=moba),
        grid=(batch, steps),
        in_specs=in_specs,
        out_specs=pl.BlockSpec((seq, vw), lambda bi, p: (bi, p)),
        out_shape=jax.ShapeDtypeStruct((t, pairs * LANES), BF16),
        scratch_shapes=[pltpu.VMEM((vw, seq), BF16), pltpu.VMEM((vw, seq), BF16)],
        compiler_params=_params(("parallel", "parallel")),
        name="moba_attn" if moba else "mla_attn",
    )(*args)


FFN_TM = 512
FFN_SUB = 256
FFN_CHUNK = 256


def _ffn_kernel(*refs, n_act, tiles_per_seq):
    acts = refs[:n_act]
    wos = refs[n_act:2 * n_act]
    (x_ref, p_ref, gm_ref, bm_ref, wup_ref, cw_ref, cb_ref, wdn_ref, g_ref, b_ref, wg_ref, wp_ref,
     o_ref, gbuf_ref) = refs[2 * n_act:]
    hist = SUBLANES
    sub = FFN_SUB
    n_sub = FFN_TM // sub
    n_chunks = D_FF // FFN_CHUNK

    @pl.when(pl.program_id(0) % tiles_per_seq == 0)
    def _():
        gbuf_ref[0:hist, :] = jnp.zeros((hist, D_FF), F32)

    mixes = []
    for s in range(n_sub):
        rows = slice(s * sub, (s + 1) * sub)
        m = _dot(acts[0][rows, :], wos[0][...])
        for a_ref, w_ref in zip(acts[1:], wos[1:]):
            m = m + _dot(a_ref[rows, :], w_ref[...])
        mixes.append(m)

    for s in range(n_sub):
        r0 = s * sub
        x = _layer_norm(DEEPNORM_ALPHA * x_ref[r0:r0 + sub, :] + mixes[s], gm_ref[...], bm_ref[...])
        xb = x.astype(BF16)

        def up_proj(c):
            lo = c * FFN_CHUNK
            return (_dot(xb, wup_ref[:, lo:lo + FFN_CHUNK]),
                    _dot(xb, wup_ref[:, D_FF + lo:D_FF + lo + FFN_CHUNK]))

        f = jnp.zeros((sub, D_MODEL), F32)
        nxt = up_proj(0)
        for c in range(n_chunks):
            cs = slice(c * FFN_CHUNK, (c + 1) * FFN_CHUNK)
            gate, up = nxt
            if c + 1 < n_chunks:
                nxt = up_proj(c + 1)
            g0 = hist + r0
            gbuf_ref[g0:g0 + sub, cs] = gate
            conv = cw_ref[FFN_CONV - 1:FFN_CONV, cs] * gate + cb_ref[:, cs]
            for k in range(FFN_CONV - 1):
                d = FFN_CONV - 1 - k
                conv = conv + cw_ref[k:k + 1, cs] * gbuf_ref[g0 - d:g0 - d + sub, cs]
            hid = 0.5 * conv * (1.0 + lax.erf(conv * INV_SQRT2)) * up
            f = f + _dot(hid.astype(BF16), wdn_ref[cs, :])
        proj = _dot(p_ref[r0:r0 + sub, :].astype(BF16), wp_ref[...])
        half = sub // 2
        for r in range(2):
            rows = slice(r * half, (r + 1) * half)
            y = _layer_norm(DEEPNORM_ALPHA * x[rows] + f[rows], g_ref[...], b_ref[...])
            gate = jax.nn.sigmoid(_dot(y.astype(BF16), wg_ref[...]))
            o_ref[r0 + r * half:r0 + (r + 1) * half, :] = y + gate * proj[rows]
    gbuf_ref[0:hist, :] = gbuf_ref[FFN_TM:FFN_TM + hist, :]


def _mix_ffn_ple(acts, w_outs, x2, p_all, layer, g_mix, b_mix, w_up, conv_w, conv_b, w_down, g, b, w_gate, w_proj,
                 seq):
    t = x2.shape[0]
    tm = FFN_TM
    row = lambda w: pl.BlockSpec((tm, w), lambda i: (i, 0))
    p_spec = pl.BlockSpec((tm, PLE_DIM), lambda i: (layer * (t // tm) + i, 0))
    consts = [g_mix, b_mix, w_up, conv_w, conv_b, w_down, g, b, w_gate, w_proj]
    return pl.pallas_call(
        functools.partial(_ffn_kernel, n_act=len(acts), tiles_per_seq=seq // tm),
        grid=(t // tm,),
        in_specs=[row(a.shape[1]) for a in acts] + [_const_spec(w.shape) for w in w_outs]
        + [row(D_MODEL), p_spec] + [_const_spec(w.shape) for w in consts],
        out_specs=row(D_MODEL),
        out_shape=jax.ShapeDtypeStruct((t, D_MODEL), F32),
        scratch_shapes=[pltpu.VMEM((tm + SUBLANES, D_FF), F32)],
        compiler_params=_params(("arbitrary",)),
        name="mix_ffn_ple",
    )(*acts, *w_outs, x2, p_all, *consts)


SSD_PROJ_TM = 256
SSD_PROJ_SUB = 256
SSD_PROJ_CHUNK = 256
SSD_BC = SSD_GROUPS * SSD_STATE


def _ssd_proj_kernel(x_ref, win_ref, cw_ref, cb_ref, dtb_ref, a_ref, tri_ref,
                     z_ref, xs_ref, b_ref, c_ref, dt_ref, la_ref, lat_ref, cbuf_ref, xb_ref, *, tiles_per_seq):
    tm = SSD_PROJ_TM
    sub = SSD_PROJ_SUB
    hist = SUBLANES
    ck = SSD_PROJ_CHUNK
    n_conv = SSD_CONV_DIM // ck

    @pl.when(pl.program_id(0) % tiles_per_seq == 0)
    def _():
        cbuf_ref[0:hist, :] = jnp.zeros((hist, SSD_CONV_DIM), F32)

    xb_ref[...] = x_ref[...].astype(BF16)
    for s in range(tm // sub):
        rs = slice(s * sub, (s + 1) * sub)
        hd = _dot(xb_ref[rs, :], win_ref[:, SSD_INNER + SSD_CONV_DIM:]) + dtb_ref[...]
        dt = jnp.maximum(hd, 0.0) + jnp.log1p(jnp.exp(-jnp.abs(hd)))
        dt_ref[rs, :] = dt
        la = jnp.dot(tri_ref[...], dt * a_ref[...], preferred_element_type=F32,
                     precision=lax.Precision.HIGHEST) * LOG2_E
        la_ref[rs, :] = la
        lat_ref[:, rs] = la.T
        for c in range(SSD_INNER // ck):
            z_ref[rs, c * ck:(c + 1) * ck] = _dot(xb_ref[rs, :], win_ref[:, c * ck:(c + 1) * ck]).astype(z_ref.dtype)

        def conv_in(c):
            return _dot(xb_ref[rs, :], win_ref[:, SSD_INNER + c * ck:SSD_INNER + (c + 1) * ck])

        nxt = conv_in(0)
        for c in range(n_conv):
            cs = slice(c * ck, (c + 1) * ck)
            h = nxt
            if c + 1 < n_conv:
                nxt = conv_in(c + 1)
            g0 = hist + s * sub
            cbuf_ref[g0:g0 + sub, cs] = h
            conv = cw_ref[SSD_CONV - 1:SSD_CONV, cs] * h + cb_ref[:, cs]
            for k in range(SSD_CONV - 1):
                d = SSD_CONV - 1 - k
                conv = conv + cw_ref[k:k + 1, cs] * cbuf_ref[g0 - d:g0 - d + sub, cs]
            act = (conv * jax.nn.sigmoid(conv)).astype(xs_ref.dtype)
            lo = c * ck
            if lo < SSD_INNER:
                xs_ref[rs, cs] = act
            elif lo < SSD_INNER + SSD_BC:
                b_ref[rs, lo - SSD_INNER:lo - SSD_INNER + ck] = act
            else:
                c_ref[rs, lo - SSD_INNER - SSD_BC:lo - SSD_INNER - SSD_BC + ck] = act
    cbuf_ref[0:hist, :] = cbuf_ref[tm:tm + hist, :]


def _ssd_proj(x2, w_in, conv_w, conv_b, dt_bias, a_row, seq):
    t = x2.shape[0]
    tm = SSD_PROJ_TM
    assert SSD_BC % SSD_PROJ_CHUNK == 0 and SSD_INNER % SSD_PROJ_CHUNK == 0 and SSD_PROJ_SUB % SSD_CHUNK == 0
    row = lambda w: pl.BlockSpec((tm, w), lambda i: (i, 0))
    out = lambda w, dt: jax.ShapeDtypeStruct((t, w), dt)
    idx = jnp.arange(SSD_PROJ_SUB)
    tri = ((idx[:, None] // SSD_CHUNK == idx[None, :] // SSD_CHUNK) & (idx[None, :] <= idx[:, None])).astype(F32)
    consts = [w_in, conv_w, conv_b, dt_bias, a_row, tri]
    return pl.pallas_call(
        functools.partial(_ssd_proj_kernel, tiles_per_seq=seq // tm),
        grid=(t // tm,),
        in_specs=[row(D_MODEL)] + [_const_spec(w.shape) for w in consts],
        out_specs=[row(SSD_INNER), row(SSD_INNER), row(SSD_BC), row(SSD_BC), row(LANES), row(LANES),
                   pl.BlockSpec((LANES, tm), lambda i: (0, i))],
        out_shape=[out(SSD_INNER, BF16), out(SSD_INNER, BF16), out(SSD_BC, BF16), out(SSD_BC, BF16),
                   out(LANES, F32), out(LANES, F32), jax.ShapeDtypeStruct((LANES, t), F32)],
        scratch_shapes=[pltpu.VMEM((tm + SUBLANES, SSD_CONV_DIM), F32), pltpu.VMEM((tm, D_MODEL), BF16)],
        compiler_params=_params(("arbitrary",)),
        name="ssd_proj",
    )(x2, *consts)


SSD_SCAN_CHUNKS = 4


def _ssd_scan_kernel(xs_ref, z_ref, b_ref, c_ref, dt_ref, la_ref, lat_ref, d_ref, nw_ref, o_ref, state_ref):
    L = SSD_CHUNK

    @pl.when(pl.program_id(1) == 0)
    def _():
        state_ref[...] = jnp.zeros_like(state_ref)

    lane = lax.broadcasted_iota(jnp.int32, (1, LANES), 1)
    low = lane < LANES // 2
    row = lax.broadcasted_iota(jnp.int32, (L, L), 0)
    col = lax.broadcasted_iota(jnp.int32, (L, L), 1)
    causal = col <= row

    for ci in range(xs_ref.shape[0] // L):
        _ssd_chunk(slice(ci * L, (ci + 1) * L), xs_ref, z_ref, b_ref, c_ref, dt_ref, la_ref, lat_ref, d_ref, nw_ref,
                   o_ref, state_ref, low, causal)


def _ssd_chunk(rs, xs_ref, z_ref, b_ref, c_ref, dt_ref, la_ref, lat_ref, d_ref, nw_ref, o_ref, state_ref,
               low, causal):
    L = SSD_CHUNK
    hg = SSD_HEADS // SSD_GROUPS
    gw = hg * SSD_HEAD_DIM
    dt = dt_ref[rs, :]
    la = la_ref[rs, :]
    la_t = lat_ref[:, rs]

    for g in range(SSD_GROUPS):
        bg = b_ref[rs, g * SSD_STATE:(g + 1) * SSD_STATE]
        cg = c_ref[rs, g * SSD_STATE:(g + 1) * SSD_STATE]
        cb = _dot_nt(cg, bg)
        bg_t = bg.astype(F32).T.astype(BF16)
        la_bc = [jnp.broadcast_to(la[:, h:h + 1], (L, LANES)) for h in range(g * hg, (g + 1) * hg)]
        dt_bc = [jnp.broadcast_to(dt[:, h:h + 1], (L, LANES)) for h in range(g * hg, (g + 1) * hg)]
        y_parts = []
        for pp in range(hg // 2):
            h0 = g * hg + 2 * pp
            ps = slice(h0 * SSD_HEAD_DIM, (h0 + 2) * SSD_HEAD_DIM)
            xs = xs_ref[rs, ps].astype(F32)
            dt_pair = jnp.where(low, dt_bc[2 * pp], dt_bc[2 * pp + 1])
            la_pair = jnp.where(low, la_bc[2 * pp], la_bc[2 * pp + 1])
            xdt = xs * dt_pair
            xdt_b = xdt.astype(BF16)
            y = jnp.zeros((L, LANES), F32)
            for hh in range(2):
                seg = la_bc[2 * pp + hh] - la_t[h0 + hh:h0 + hh + 1, :]
                w = (cb * jnp.exp2(jnp.where(causal, seg, -jnp.inf))).astype(BF16)
                keep = low if hh == 0 else jnp.logical_not(low)
                y = y + _dot(w, jnp.where(keep, xdt_b, jnp.zeros_like(xdt_b)))
            st = state_ref[:, ps]
            y = y + _dot(cg, st.astype(BF16)) * jnp.exp2(la_pair)
            la_end = la_pair[L - 1:L, :]
            to_end = jnp.exp2(la_end - la_pair)
            state_ref[:, ps] = st * jnp.exp2(la_end) + _dot(bg_t, (xdt * to_end).astype(BF16))
            y = y + d_ref[:, ps] * xs
            zz = z_ref[rs, ps].astype(F32)
            y_parts.append(y * (zz * jax.nn.sigmoid(zz)))
        yg = jnp.concatenate(y_parts, axis=1)
        ms = jnp.mean(jnp.square(yg), axis=-1, keepdims=True)
        gs = slice(g * gw, (g + 1) * gw)
        o_ref[rs, gs] = (yg * lax.rsqrt(ms + RMS_EPS) * nw_ref[:, gs]).astype(o_ref.dtype)


def _ssd_scan(xs, z, bm, cm, dt, la, la_t, d_row, norm_w, batch, seq):
    t = xs.shape[0]
    rows = min(SSD_SCAN_CHUNKS * SSD_CHUNK, seq)
    nc = seq // rows
    assert SSD_INNER // SSD_GROUPS == SSD_BC
    row = lambda w: pl.BlockSpec((rows, w), lambda bi, ci: (bi * nc + ci, 0))
    consts = [d_row, norm_w]
    return pl.pallas_call(
        _ssd_scan_kernel,
        grid=(batch, nc),
        in_specs=[row(SSD_INNER), row(SSD_INNER), row(SSD_BC), row(SSD_BC), row(LANES), row(LANES),
                  pl.BlockSpec((LANES, rows), lambda bi, ci: (0, bi * nc + ci))]
        + [_const_spec(w.shape) for w in consts],
        out_specs=row(SSD_INNER),
        out_shape=jax.ShapeDtypeStruct((t, SSD_INNER), BF16),
        scratch_shapes=[pltpu.VMEM((SSD_STATE, SSD_INNER), F32)],
        compiler_params=_params(("parallel", "arbitrary")),
        name="ssd_scan",
    )(xs, z, bm, cm, dt, la, la_t, *consts)


def _pad_cols(w, n):
    return jnp.concatenate([w, jnp.zeros((w.shape[0], n), w.dtype)], axis=1)


def _attention_layer(x2, tabs, batch, seq, w_in, q_norm, w_uq, kv_norm, w_ukv, w_out):
    o1 = MLA_Q_LORA + MLA_KV_LORA
    zc = lambda n: jnp.zeros((D_MODEL, n), w_in.dtype)
    o2 = o1 + MLA_ROPE + 2 * MOBA_WIDTH
    w_in2 = jnp.concatenate([w_in[:, :o1], zc(MLA_NOPE), w_in[:, o1:o1 + MLA_ROPE],
                             zc(LANES - MLA_NOPE - MLA_ROPE), w_in[:, o1 + MLA_ROPE:o2]], axis=1).astype(BF16)
    w_vb_t = w_in[:, o2:].T.astype(BF16)
    dqk = MLA_NOPE + MLA_ROPE
    w_uq2 = jnp.pad(w_uq.reshape(MLA_Q_LORA, MLA_HEADS, dqk), ((0, 0), (0, 0), (0, HEAD_PAD - dqk)))
    w_uq2 = w_uq2.reshape(MLA_Q_LORA, MLA_HEADS * HEAD_PAD).astype(BF16)
    w_kv3 = w_ukv.reshape(MLA_KV_LORA, MLA_HEADS, MLA_NOPE + MLA_V)
    w_k2 = jnp.pad(w_kv3[:, :, :MLA_NOPE], ((0, 0), (0, 0), (0, HEAD_PAD - MLA_NOPE)))
    w_k2 = w_k2.reshape(MLA_KV_LORA, MLA_HEADS * HEAD_PAD).astype(BF16)
    w_v_t = w_kv3[:, :, MLA_NOPE:].reshape(MLA_KV_LORA, MLA_HEADS * MLA_V).T.astype(BF16)

    qm, km, vm, qb, kb, vb, kmean = _att_proj(x2, tabs, w_in2, w_uq2, w_k2, w_v_t, w_vb_t,
                                              q_norm.reshape(1, -1), kv_norm.reshape(1, -1))
    o_mla = _attention(qm, km, vm, batch, seq)
    qa = _moba_select(qb, kmean, batch, seq)
    pos = jnp.arange(seq)
    onehot = (pos[:, None] // MOBA_BLOCK == jnp.arange(LANES)[None, :]).astype(BF16)
    o_moba = _attention(qa, kb, vb, batch, seq, onehot=onehot)
    n_mla = MLA_HEADS * MLA_V
    w_o = w_out.astype(BF16)
    return [o_mla, o_moba], [w_o[:n_mla], w_o[n_mla:]]


def _ssd_layer(x2, batch, seq, w_in, conv_w, conv_b, dt_bias, a_log, d_skip, norm_w, w_out):
    w_in2 = _pad_cols(w_in, LANES - SSD_HEADS).astype(BF16)
    dtb = _pad_cols(dt_bias.reshape(1, -1), LANES - SSD_HEADS)
    a_row = _pad_cols((-jnp.exp(a_log.astype(F32))).reshape(1, -1), LANES - SSD_HEADS)
    z, xs, bm, cm, dt, la, la_t = _ssd_proj(x2, w_in2, conv_w, conv_b.reshape(1, -1), dtb, a_row, seq)
    d_row = jnp.repeat(d_skip.astype(F32), SSD_HEAD_DIM).reshape(1, -1)
    y = _ssd_scan(xs, z, bm, cm, dt, la, la_t, d_row, norm_w.reshape(1, -1), batch, seq)
    return [y], [w_out.astype(BF16)]


def kernel(x, p, positions, att_w_in, mla_q_norm, mla_w_uq, mla_kv_norm, mla_w_ukv, att_w_out, ssd_w_in, ssd_conv_w, ssd_conv_b, ssd_dt_bias, ssd_a_log, ssd_d, ssd_norm, ssd_w_out, ln_mix_g, ln_mix_b, ffn_w_up, ffn_conv_w, ffn_conv_b, ffn_w_down, ln_ffn_g, ln_ffn_b, ple_w_gate, ple_w_proj):
    batch, seq, _ = x.shape
    depth = p.shape[0]
    t = batch * seq
    x2 = x.reshape(t, D_MODEL)
    tabs = _rope_tables(positions)
    for i in range(depth):
        j = i // 2
        if i % 2 == 0:
            acts, ws = _attention_layer(x2, tabs, batch, seq, att_w_in[j], mla_q_norm[j], mla_w_uq[j],
                                        mla_kv_norm[j], mla_w_ukv[j], att_w_out[j])
        else:
            acts, ws = _ssd_layer(x2, batch, seq, ssd_w_in[j], ssd_conv_w[j], ssd_conv_b[j], ssd_dt_bias[j],
                                  ssd_a_log[j], ssd_d[j], ssd_norm[j], ssd_w_out[j])
        x2 = _mix_ffn_ple(acts, ws, x2, p.reshape(depth * t, PLE_DIM), i,
                          ln_mix_g[i].reshape(1, -1), ln_mix_b[i].reshape(1, -1),
                          ffn_w_up[i].astype(BF16), ffn_conv_w[i], ffn_conv_b[i].reshape(1, -1),
                          ffn_w_down[i].astype(BF16), ln_ffn_g[i].reshape(1, -1), ln_ffn_b[i].reshape(1, -1),
                          ple_w_gate[i].astype(BF16), ple_w_proj[i].astype(BF16), seq)
    return x2.reshape(batch, seq, D_MODEL)
```

```python
import functools

import jax
import jax.numpy as jnp
from jax import lax
from jax.experimental import pallas as pl
from jax.experimental.pallas import tpu as pltpu

D_MODEL = 1024
PLE_DIM = 256
ROPE_THETA = 500000.0
MLA_HEADS = 8
MLA_Q_LORA = 256
MLA_KV_LORA = 128
MLA_NOPE = 64
MLA_ROPE = 32
MLA_V = 64
MOBA_HEADS = 8
MOBA_HEAD_DIM = 64
MOBA_ROT = MOBA_HEAD_DIM // 4
MOBA_BLOCK = 256
MOBA_TOPK = 3
MOBA_WIDTH = MOBA_HEADS * MOBA_HEAD_DIM
SSD_INNER = 2 * D_MODEL
SSD_HEAD_DIM = 64
SSD_HEADS = SSD_INNER // SSD_HEAD_DIM
SSD_GROUPS = 4
SSD_STATE = 128
SSD_CONV = 4
SSD_CHUNK = 128
SSD_CONV_DIM = SSD_INNER + 2 * SSD_GROUPS * SSD_STATE
D_FF = 2816
FFN_CONV = 3
LN_EPS = 1e-5
RMS_EPS = 1e-6
DEPTH = 2
DEEPNORM_ALPHA = (2 * DEPTH) ** 0.25

LANES = 128
SUBLANES = 8
HEAD_PAD = LANES
VMEM_LIMIT = 56 * 1024 * 1024
NEG_BIG = -1e30
INV_SQRT2 = 0.7071067811865476
LOG2_E = 1.4426950408889634

F32 = jnp.float32
BF16 = jnp.bfloat16


def _dot(a, b):
    return jnp.dot(a, b, preferred_element_type=F32)


def _dot_nt(a, b):
    return lax.dot_general(a, b, (((1,), (1,)), ((), ())), preferred_element_type=F32)


def _const_spec(shape):
    return pl.BlockSpec(shape, lambda *_: (0,) * len(shape), pipeline_mode=pl.Buffered(1))


def _params(semantics):
    return pltpu.CompilerParams(dimension_semantics=semantics, vmem_limit_bytes=VMEM_LIMIT)


ROPE_PACK = 4


def _split3(a):
    hi = a.astype(BF16)
    r1 = a - hi.astype(F32)
    mid = r1.astype(BF16)
    lo = (r1 - mid.astype(F32)).astype(BF16)
    return jnp.concatenate([hi, mid, lo], axis=1)


def _rope_table_kernel(pos_ref, f_ref, ec_ref, e1_ref, e2_ref, bias_ref, c_ref, s1_ref, s2_ref):
    pos = pos_ref[...].astype(F32)
    ang = pos[:, 0:1] * f_ref[0:1, :]
    for u in range(1, ROPE_PACK):
        ang = ang + pos[:, u:u + 1] * f_ref[u:u + 1, :]
    cos3, sin3 = _split3(jnp.cos(ang)), _split3(jnp.sin(ang))
    rows = pos_ref.shape[0]
    for r in range(2):
        c = _dot(cos3, ec_ref[r]) + bias_ref[r:r + 1, :]
        s1 = _dot(sin3, e1_ref[r])
        s2 = _dot(sin3, e2_ref[r])
        for u in range(ROPE_PACK):
            ls = slice(u * LANES, (u + 1) * LANES)
            dst = pl.ds(u, rows, stride=ROPE_PACK)
            c_ref[r, dst, :] = c[:, ls]
            s1_ref[r, dst, :] = s1[:, ls]
            s2_ref[r, dst, :] = s2[:, ls]


def _rope_tables(positions):
    t = positions.size
    inv_m = ROPE_THETA ** (-jnp.arange(0, MLA_ROPE, 2, dtype=F32) / MLA_ROPE)
    inv_b = ROPE_THETA ** (-jnp.arange(0, MOBA_ROT, 2, dtype=F32) / MOBA_ROT)
    hm, hb = MLA_ROPE // 2, MOBA_ROT // 2
    slot = LANES // ROPE_PACK
    assert hm + hb <= slot and t % ROPE_PACK == 0
    f_tok = jnp.concatenate([inv_m, inv_b, jnp.zeros((slot - hm - hb,), F32)])
    f = jnp.kron(jnp.eye(ROPE_PACK, dtype=F32), f_tok[None, :])
    src = jnp.arange(slot)[:, None]
    dst = jnp.arange(LANES)[None, :]
    m_x1 = (src < hm) & (dst == MLA_NOPE + src)
    m_x2 = (src < hm) & (dst == MLA_NOPE + hm + src)
    i_b = src - hm
    in_b = (i_b >= 0) & (i_b < hb)
    b_x1 = in_b & (dst % MOBA_HEAD_DIM == i_b)
    b_x2 = in_b & (dst % MOBA_HEAD_DIM == hb + i_b)

    def expand(m, sign=1.0):
        e = jnp.kron(jnp.eye(ROPE_PACK, dtype=F32), sign * m.astype(F32))
        return jnp.concatenate([e, e, e], axis=0).astype(BF16)

    ec = jnp.stack([expand(m_x1 | m_x2), expand(b_x1 | b_x2)])
    e1 = jnp.stack([expand(m_x1, -1.0), expand(b_x1, -1.0)])
    e2 = jnp.stack([expand(m_x2), expand(b_x2)])
    bias = jnp.stack([jnp.tile(1.0 - jnp.sum((m_x1 | m_x2).astype(F32), axis=0), ROPE_PACK),
                      jnp.tile(1.0 - jnp.sum((b_x1 | b_x2).astype(F32), axis=0), ROPE_PACK)])
    rows = t // ROPE_PACK
    tr = min(rows, 512)
    out = jax.ShapeDtypeStruct((2, t, LANES), F32)
    tab_spec = pl.BlockSpec((2, tr * ROPE_PACK, LANES), lambda i: (0, i, 0))
    consts = [f, ec, e1, e2, bias]
    return pl.pallas_call(
        _rope_table_kernel,
        grid=(rows // tr,),
        in_specs=[pl.BlockSpec((tr, ROPE_PACK), lambda i: (i, 0))] + [_const_spec(w.shape) for w in consts],
        out_specs=[tab_spec, tab_spec, tab_spec],
        out_shape=[out, out, out],
        compiler_params=_params(("parallel",)),
        name="rope_tables",
    )(positions.reshape(rows, ROPE_PACK), *consts)


def _rope(t, c, s1, s2, half):
    w = t.shape[-1]
    return t * c + pltpu.roll(t, w - half, 1) * s1 + pltpu.roll(t, half, 1) * s2


def _rms(x, g):
    ms = jnp.mean(jnp.square(x), axis=-1, keepdims=True)
    return x * lax.rsqrt(ms + RMS_EPS) * g


def _layer_norm(x, g, b):
    mu = jnp.mean(x, axis=-1, keepdims=True)
    xc = x - mu
    var = jnp.mean(jnp.square(xc), axis=-1, keepdims=True)
    return xc * lax.rsqrt(var + LN_EPS) * g + b


ATT_PROJ_TM = 512


def _att_proj_kernel(x_ref, win_ref, wuq_ref, wk_ref, wv_ref, wvb_ref, qn_ref, kvn_ref, c_ref, s1_ref, s2_ref,
                     qm_ref, km_ref, vm_ref, qb_ref, kb_ref, vb_ref, kmean_ref):
    xb = x_ref[...].astype(BF16)
    cm, s1m, s2m = c_ref[0], s1_ref[0], s2_ref[0]
    cb, s1b, s2b = c_ref[1], s1_ref[1], s2_ref[1]
    hm, hb = MLA_ROPE // 2, MOBA_ROT // 2
    mla_scale = (MLA_NOPE + MLA_ROPE) ** -0.5 * LOG2_E
    moba_scale = MOBA_HEAD_DIM ** -0.5 * LOG2_E

    h_lat = _dot(xb, win_ref[:, 0:512])
    c_q = h_lat[:, 0:MLA_Q_LORA]
    c_kv = h_lat[:, MLA_Q_LORA:MLA_Q_LORA + MLA_KV_LORA]
    hq = _dot(xb, win_ref[:, 512:1024])
    hk = _dot(xb, win_ref[:, 1024:1536])
    vb_ref[...] = _dot_nt(wvb_ref[...], xb).astype(BF16)
    k_rope = _rope(h_lat[:, 384:512], cm, s1m, s2m, hm)

    q = _dot(_rms(c_q, qn_ref[...]).astype(BF16), wuq_ref[...])
    for h in range(MLA_HEADS):
        sl = slice(h * HEAD_PAD, (h + 1) * HEAD_PAD)
        qm_ref[:, sl] = (_rope(q[:, sl], cm, s1m, s2m, hm) * mla_scale).astype(BF16)

    ckv = _rms(c_kv, kvn_ref[...]).astype(BF16)
    k = _dot(ckv, wk_ref[...])
    for h in range(MLA_HEADS):
        sl = slice(h * HEAD_PAD, (h + 1) * HEAD_PAD)
        km_ref[:, sl] = (k[:, sl] + k_rope).astype(BF16)
    vm_ref[...] = _dot_nt(wv_ref[...], ckv).astype(BF16)

    for g in range(MOBA_WIDTH // LANES):
        sl = slice(g * LANES, (g + 1) * LANES)
        qb_ref[:, sl] = (_rope(hq[:, sl], cb, s1b, s2b, hb) * moba_scale).astype(BF16)
        kr = _rope(hk[:, sl], cb, s1b, s2b, hb)
        kb_ref[:, sl] = kr.astype(BF16)
        for r in range(ATT_PROJ_TM // MOBA_BLOCK):
            rows = slice(r * MOBA_BLOCK, (r + 1) * MOBA_BLOCK)
            kmean_ref[r, :, sl] = jnp.mean(kr[rows], axis=0, keepdims=True)


def _att_proj(x2, tabs, w_in, w_uq, w_k, w_v_t, w_vb_t, q_norm, kv_norm):
    t = x2.shape[0]
    tm = ATT_PROJ_TM
    c, s1, s2 = tabs
    row = lambda w: pl.BlockSpec((tm, w), lambda i: (i, 0))
    col = lambda h: pl.BlockSpec((h, tm), lambda i: (0, i))
    tab = pl.BlockSpec((2, tm, LANES), lambda i: (0, i, 0))
    bf = lambda w: jax.ShapeDtypeStruct((t, w), BF16)
    bf_t = lambda h: jax.ShapeDtypeStruct((h, t), BF16)
    nblk = tm // MOBA_BLOCK
    consts = [w_in, w_uq, w_k, w_v_t, w_vb_t, q_norm, kv_norm]
    return pl.pallas_call(
        _att_proj_kernel,
        grid=(t // tm,),
        in_specs=[row(D_MODEL)] + [_const_spec(w.shape) for w in consts] + [tab, tab, tab],
        out_specs=[row(1024), row(1024), col(512), row(512), row(512), col(512),
                   pl.BlockSpec((nblk, 1, MOBA_WIDTH), lambda i: (i, 0, 0))],
        out_shape=[bf(1024), bf(1024), bf_t(512), bf(512), bf(512), bf_t(512),
                   jax.ShapeDtypeStruct((t // MOBA_BLOCK, 1, MOBA_WIDTH), F32)],
        compiler_params=_params(("parallel",)),
        name="att_proj",
    )(x2, *consts, c, s1, s2)


SELECT_BLOCKS = 2


def _moba_select_kernel(q_ref, kab_ref, r_ref, lc_ref, qa_ref, *, nb):
    lane = lax.broadcasted_iota(jnp.int32, (1, LANES), 1)
    tie_first = lc_ref[1:2, :]
    low = lane < (LANES // 2)
    blocks = q_ref.shape[0] // MOBA_BLOCK
    gates = []
    for bk in range(blocks):
        rows = slice(bk * MOBA_BLOCK, (bk + 1) * MOBA_BLOCK)
        for h in range(MOBA_HEADS):
            qp = q_ref[rows, (h // 2) * LANES:(h // 2 + 1) * LANES]
            qm = jnp.where(low if h % 2 == 0 else jnp.logical_not(low), qp, jnp.zeros_like(qp))
            qa_ref[rows, 2 * h * LANES:(2 * h + 1) * LANES] = qm
            gates.append(_dot(qm, kab_ref[0, h]))
    beats = []
    for i, ab in enumerate(gates):
        own = pl.program_id(1) * blocks + i // MOBA_HEADS
        pair_valid = jnp.where(lc_ref[0:1, :] < own.astype(F32), 1.0, 0.0)
        a, b = ab[:, 0:LANES], ab[:, LANES:2 * LANES]
        beats.append((jnp.where(b > a, 1.0, jnp.where(b == a, tie_first, 0.0)) * pair_valid).astype(BF16))
    counts = [_dot(bt, r_ref[...]) for bt in beats]
    for i, cnt in enumerate(counts):
        bk, h = divmod(i, MOBA_HEADS)
        own = pl.program_id(1) * blocks + bk
        rest_pen = jnp.where((lane > own) & (lane < nb), NEG_BIG, 0.0)
        pen = jnp.where(lane < own, jnp.where(cnt < MOBA_TOPK, 0.0, NEG_BIG), rest_pen)
        qa_ref[bk * MOBA_BLOCK:(bk + 1) * MOBA_BLOCK, (2 * h + 1) * LANES:(2 * h + 2) * LANES] = pen.astype(BF16)


def _moba_select(qb, kmean, batch, seq):
    t = qb.shape[0]
    nb = seq // MOBA_BLOCK
    assert nb * nb <= LANES
    km = kmean.reshape(batch, nb, MOBA_HEADS, MOBA_HEAD_DIM).transpose(0, 2, 3, 1)
    a = jnp.repeat(km, nb, axis=-1)
    b = jnp.tile(km, (1, 1, 1, nb))
    zc = jnp.zeros(km.shape[:3] + (LANES - nb * nb,), F32)
    ab = jnp.concatenate([a, zc, b, zc], axis=-1)
    zr = jnp.zeros_like(ab)
    odd = (jnp.arange(MOBA_HEADS) % 2 == 1)[None, :, None, None]
    kab = jnp.where(odd, jnp.concatenate([zr, ab], axis=2), jnp.concatenate([ab, zr], axis=2)).astype(BF16)
    cidx = jnp.arange(LANES)
    rmat = ((cidx[:, None] // nb == cidx[None, :]) & (cidx[:, None] < nb * nb)).astype(BF16)
    used = cidx < nb * nb
    lane_consts = jnp.zeros((SUBLANES, LANES), F32)
    lane_consts = lane_consts.at[0].set(jnp.where(used, cidx % nb, nb).astype(F32))
    lane_consts = lane_consts.at[1].set((used & (cidx % nb < cidx // nb)).astype(F32))
    tq = min(SELECT_BLOCKS, nb) * MOBA_BLOCK
    return pl.pallas_call(
        functools.partial(_moba_select_kernel, nb=nb),
        grid=(batch, seq // tq),
        in_specs=[pl.BlockSpec((tq, MOBA_WIDTH), lambda bi, qi: (bi * (seq // tq) + qi, 0)),
                  pl.BlockSpec((1, MOBA_HEADS, LANES, 2 * LANES), lambda bi, qi: (bi, 0, 0, 0)),
                  _const_spec((LANES, LANES)), _const_spec((SUBLANES, LANES))],
        out_specs=pl.BlockSpec((tq, MOBA_HEADS * 2 * LANES), lambda bi, qi: (bi * (seq // tq) + qi, 0)),
        out_shape=jax.ShapeDtypeStruct((t, MOBA_HEADS * 2 * LANES), BF16),
        compiler_params=_params(("parallel", "parallel")),
        name="moba_select",
    )(qb, kab, rmat, lane_consts)


ATT_TILE = 256
ATT_PAIRS = 4


def _attn_kernel(*refs, moba):
    if moba:
        q_ref, k_ref, oh_ref, v_ref, o_ref, va_ref, vb_ref = refs
    else:
        q_ref, k_ref, v_ref, o_ref, va_ref, vb_ref = refs
        oh_ref = None
    seq = q_ref.shape[0]
    tq = ATT_TILE
    pairs = v_ref.shape[0] // LANES
    heads = 2 * pairs
    qw = q_ref.shape[1] // heads
    sub = lax.broadcasted_iota(jnp.int32, (LANES, 1), 0)
    half = LANES // 2
    for p in range(pairs):
        ps = slice(p * LANES, (p + 1) * LANES)
        v = v_ref[ps, :].astype(F32)
        va_ref[ps, :] = jnp.where(sub < half, v, jnp.where(sub == half, 1.0, 0.0)).astype(BF16)
        vb_ref[ps, :] = jnp.where(sub >= half, v, jnp.where(sub == 0, 1.0, 0.0)).astype(BF16)
    key = lax.broadcasted_iota(jnp.int32, (tq, tq), 0)
    qry = lax.broadcasted_iota(jnp.int32, (tq, tq), 1)
    causal = key <= qry

    def load_q(r0, h):
        return q_ref[pl.ds(r0, tq), h * qw:(h + 1) * qw]

    def load_k(c0, h):
        if moba:
            ps = slice((h // 2) * LANES, (h // 2 + 1) * LANES)
            return jnp.concatenate([k_ref[pl.ds(c0, tq), ps], oh_ref[pl.ds(c0, tq), :]], axis=1)
        return k_ref[pl.ds(c0, tq), h * HEAD_PAD:(h + 1) * HEAD_PAD]

    def load_v(c0, h):
        ps = slice((h // 2) * LANES, (h // 2 + 1) * LANES)
        return (va_ref if h % 2 == 0 else vb_ref)[ps, pl.ds(c0, tq)]

    def scores_of(r0, c0):
        return [_dot_nt(load_k(c0, h), load_q(r0, h)) for h in range(heads)]

    def value_update(c0, accs, alphas, ps):
        return tuple(alphas[h] * accs[h] + _dot(load_v(c0, h), ps[h]) for h in range(heads))

    def softmax_update(scores, ms):
        out = []
        for h in range(heads):
            m_new = jnp.maximum(ms[h], jnp.max(scores[h], axis=0, keepdims=True))
            alpha = jnp.exp2(ms[h] - m_new)
            p = jnp.exp2((scores[h] - m_new).astype(BF16))
            out.append((m_new, alpha, p))
        return tuple(zip(*out))

    def q_tile(qi):
        r0 = qi * tq

        rep = lambda a: (a,) * heads
        scores = [jnp.where(causal, s_t, -jnp.inf) for s_t in scores_of(r0, r0)]
        ms, alphas, ps = softmax_update(scores, rep(jnp.full((1, tq), -jnp.inf, F32)))

        def prev_tile(j):
            if isinstance(j, int):
                return (qi if j == 0 else j - 1) * tq
            return pl.multiple_of(jnp.where(j == 0, qi, j - 1) * tq, tq)

        def body(j, carry):
            ms, accs, alphas, ps = carry
            scores = scores_of(r0, pl.multiple_of(j * tq, tq))
            accs = value_update(prev_tile(j), accs, alphas, ps)
            ms, alphas, ps = softmax_update(scores, ms)
            return ms, accs, alphas, ps

        init = (ms, rep(jnp.zeros((LANES, tq), F32)), alphas, ps)
        ms, accs, alphas, ps = lax.fori_loop(0, qi, body, init, unroll=4)
        accs = value_update(prev_tile(qi), accs, alphas, ps)
        for p in range(pairs):
            a0, a1 = accs[2 * p], accs[2 * p + 1]
            o_t = jnp.where(sub < half, a0 * (1.0 / a0[half:half + 1, :]), a1 * (1.0 / a1[0:1, :]))
            o_ref[pl.ds(r0, tq), p * LANES:(p + 1) * LANES] = o_t.T.astype(o_ref.dtype)

    for qi in range(seq // tq):
        q_tile(qi)


def _attention(q, k, v_t, batch, seq, onehot=None):
    t = q.shape[0]
    moba = onehot is not None
    pairs = v_t.shape[0] // LANES
    steps = pairs // ATT_PAIRS
    blk = lambda a: pl.BlockSpec((seq, a.shape[1] // steps), lambda bi, p: (bi, p))
    vw = ATT_PAIRS * LANES
    in_specs = [blk(q), blk(k)]
    args = [q, k]
    if moba:
        in_specs.append(_const_spec((seq, LANES)))
        args.append(onehot)
    in_specs.append(pl.BlockSpec((vw, seq), lambda bi, p: (p, bi)))
    args.append(v_t)
    return pl.pallas_call(
        functools.partial(_attn_kernel, moba=moba),
        grid=(batch, steps),
        in_specs=in_specs,
        out_specs=pl.BlockSpec((seq, vw), lambda bi, p: (bi, p)),
        out_shape=jax.ShapeDtypeStruct((t, pairs * LANES), BF16),
        scratch_shapes=[pltpu.VMEM((vw, seq), BF16), pltpu.VMEM((vw, seq), BF16)],
        compiler_params=_params(("parallel", "parallel")),
        name="moba_attn" if moba else "mla_attn",
    )(*args)


FFN_TM = 512
FFN_SUB = 256
FFN_CHUNK = 256


def _ffn_kernel(*refs, n_act, tiles_per_seq):
    acts = refs[:n_act]
    wos = refs[n_act:2 * n_act]
    (x_ref, p_ref, gm_ref, bm_ref, wup_ref, cw_ref, cb_ref, wdn_ref, g_ref, b_ref, wg_ref, wp_ref,
     o_ref, gbuf_ref) = refs[2 * n_act:]
    hist = SUBLANES
    sub = FFN_SUB
    n_sub = FFN_TM // sub
    n_chunks = D_FF // FFN_CHUNK

    @pl.when(pl.program_id(0) % tiles_per_seq == 0)
    def _():
        gbuf_ref[0:hist, :] = jnp.zeros((hist, D_FF), F32)

    mixes = []
    for s in range(n_sub):
        rows = slice(s * sub, (s + 1) * sub)
        m = _dot(acts[0][rows, :], wos[0][...])
        for a_ref, w_ref in zip(acts[1:], wos[1:]):
            m = m + _dot(a_ref[rows, :], w_ref[...])
        mixes.append(m)

    for s in range(n_sub):
        r0 = s * sub
        x = _layer_norm(DEEPNORM_ALPHA * x_ref[r0:r0 + sub, :] + mixes[s], gm_ref[...], bm_ref[...])
        xb = x.astype(BF16)

        def up_proj(c):
            lo = c * FFN_CHUNK
            return (_dot(xb, wup_ref[:, lo:lo + FFN_CHUNK]),
                    _dot(xb, wup_ref[:, D_FF + lo:D_FF + lo + FFN_CHUNK]))

        f = jnp.zeros((sub, D_MODEL), F32)
        nxt = up_proj(0)
        for c in range(n_chunks):
            cs = slice(c * FFN_CHUNK, (c + 1) * FFN_CHUNK)
            gate, up = nxt
            if c + 1 < n_chunks:
                nxt = up_proj(c + 1)
            g0 = hist + r0
            gbuf_ref[g0:g0 + sub, cs] = gate
            conv = cw_ref[FFN_CONV - 1:FFN_CONV, cs] * gate + cb_ref[:, cs]
            for k in range(FFN_CONV - 1):
                d = FFN_CONV - 1 - k
                conv = conv + cw_ref[k:k + 1, cs] * gbuf_ref[g0 - d:g0 - d + sub, cs]
            hid = 0.5 * conv * (1.0 + lax.erf(conv * INV_SQRT2)) * up
            f = f + _dot(hid.astype(BF16), wdn_ref[cs, :])
        proj = _dot(p_ref[r0:r0 + sub, :].astype(BF16), wp_ref[...])
        half = sub // 2
        for r in range(2):
            rows = slice(r * half, (r + 1) * half)
            y = _layer_norm(DEEPNORM_ALPHA * x[rows] + f[rows], g_ref[...], b_ref[...])
            gate = jax.nn.sigmoid(_dot(y.astype(BF16), wg_ref[...]))
            o_ref[r0 + r * half:r0 + (r + 1) * half, :] = y + gate * proj[rows]
    gbuf_ref[0:hist, :] = gbuf_ref[FFN_TM:FFN_TM + hist, :]


def _mix_ffn_ple(acts, w_outs, x2, p_all, layer, g_mix, b_mix, w_up, conv_w, conv_b, w_down, g, b, w_gate, w_proj,
                 seq):
    t = x2.shape[0]
    tm = FFN_TM
    row = lambda w: pl.BlockSpec((tm, w), lambda i: (i, 0))
    p_spec = pl.BlockSpec((tm, PLE_DIM), lambda i: (layer * (t // tm) + i, 0))
    consts = [g_mix, b_mix, w_up, conv_w, conv_b, w_down, g, b, w_gate, w_proj]
    return pl.pallas_call(
        functools.partial(_ffn_kernel, n_act=len(acts), tiles_per_seq=seq // tm),
        grid=(t // tm,),
        in_specs=[row(a.shape[1]) for a in acts] + [_const_spec(w.shape) for w in w_outs]
        + [row(D_MODEL), p_spec] + [_const_spec(w.shape) for w in consts],
        out_specs=row(D_MODEL),
        out_shape=jax.ShapeDtypeStruct((t, D_MODEL), F32),
        scratch_shapes=[pltpu.VMEM((tm + SUBLANES, D_FF), F32)],
        compiler_params=_params(("arbitrary",)),
        name="mix_ffn_ple",
    )(*acts, *w_outs, x2, p_all, *consts)


SSD_PROJ_TM = 256
SSD_PROJ_SUB = 256
SSD_PROJ_CHUNK = 256
SSD_BC = SSD_GROUPS * SSD_STATE


def _ssd_proj_kernel(x_ref, win_ref, cw_ref, cb_ref, dtb_ref, a_ref, tri_ref,
                     z_ref, xs_ref, b_ref, c_ref, dt_ref, la_ref, lat_ref, cbuf_ref, xb_ref, *, tiles_per_seq):
    tm = SSD_PROJ_TM
    sub = SSD_PROJ_SUB
    hist = SUBLANES
    ck = SSD_PROJ_CHUNK
    n_conv = SSD_CONV_DIM // ck

    @pl.when(pl.program_id(0) % tiles_per_seq == 0)
    def _():
        cbuf_ref[0:hist, :] = jnp.zeros((hist, SSD_CONV_DIM), F32)

    xb_ref[...] = x_ref[...].astype(BF16)
    for s in range(tm // sub):
        rs = slice(s * sub, (s + 1) * sub)
        hd = _dot(xb_ref[rs, :], win_ref[:, SSD_INNER + SSD_CONV_DIM:]) + dtb_ref[...]
        dt = jnp.maximum(hd, 0.0) + jnp.log1p(jnp.exp(-jnp.abs(hd)))
        dt_ref[rs, :] = dt
        la = jnp.dot(tri_ref[...], dt * a_ref[...], preferred_element_type=F32,
                     precision=lax.Precision.HIGHEST) * LOG2_E
        la_ref[rs, :] = la
        lat_ref[:, rs] = la.T
        for c in range(SSD_INNER // ck):
            z_ref[rs, c * ck:(c + 1) * ck] = _dot(xb_ref[rs, :], win_ref[:, c * ck:(c + 1) * ck]).astype(z_ref.dtype)

        def conv_in(c):
            return _dot(xb_ref[rs, :], win_ref[:, SSD_INNER + c * ck:SSD_INNER + (c + 1) * ck])

        nxt = conv_in(0)
        for c in range(n_conv):
            cs = slice(c * ck, (c + 1) * ck)
            h = nxt
            if c + 1 < n_conv:
                nxt = conv_in(c + 1)
            g0 = hist + s * sub
            cbuf_ref[g0:g0 + sub, cs] = h
            conv = cw_ref[SSD_CONV - 1:SSD_CONV, cs] * h + cb_ref[:, cs]
            for k in range(SSD_CONV - 1):
                d = SSD_CONV - 1 - k
                conv = conv + cw_ref[k:k + 1, cs] * cbuf_ref[g0 - d:g0 - d + sub, cs]
            act = (conv * jax.nn.sigmoid(conv)).astype(xs_ref.dtype)
            lo = c * ck
            if lo < SSD_INNER:
                xs_ref[rs, cs] = act
            elif lo < SSD_INNER + SSD_BC:
                b_ref[rs, lo - SSD_INNER:lo - SSD_INNER + ck] = act
            else:
                c_ref[rs, lo - SSD_INNER - SSD_BC:lo - SSD_INNER - SSD_BC + ck] = act
    cbuf_ref[0:hist, :] = cbuf_ref[tm:tm + hist, :]


def _ssd_proj(x2, w_in, conv_w, conv_b, dt_bias, a_row, seq):
    t = x2.shape[0]
    tm = SSD_PROJ_TM
    assert SSD_BC % SSD_PROJ_CHUNK == 0 and SSD_INNER % SSD_PROJ_CHUNK == 0 and SSD_PROJ_SUB % SSD_CHUNK == 0
    row = lambda w: pl.BlockSpec((tm, w), lambda i: (i, 0))
    out = lambda w, dt: jax.ShapeDtypeStruct((t, w), dt)
    idx = jnp.arange(SSD_PROJ_SUB)
    tri = ((idx[:, None] // SSD_CHUNK == idx[None, :] // SSD_CHUNK) & (idx[None, :] <= idx[:, None])).astype(F32)
    consts = [w_in, conv_w, conv_b, dt_bias, a_row, tri]
    return pl.pallas_call(
        functools.partial(_ssd_proj_kernel, tiles_per_seq=seq // tm),
        grid=(t // tm,),
        in_specs=[row(D_MODEL)] + [_const_spec(w.shape) for w in consts],
        out_specs=[row(SSD_INNER), row(SSD_INNER), row(SSD_BC), row(SSD_BC), row(LANES), row(LANES),
                   pl.BlockSpec((LANES, tm), lambda i: (0, i))],
        out_shape=[out(SSD_INNER, BF16), out(SSD_INNER, BF16), out(SSD_BC, BF16), out(SSD_BC, BF16),
                   out(LANES, F32), out(LANES, F32), jax.ShapeDtypeStruct((LANES, t), F32)],
        scratch_shapes=[pltpu.VMEM((tm + SUBLANES, SSD_CONV_DIM), F32), pltpu.VMEM((tm, D_MODEL), BF16)],
        compiler_params=_params(("arbitrary",)),
        name="ssd_proj",
    )(x2, *consts)


SSD_SCAN_CHUNKS = 4


def _ssd_scan_kernel(xs_ref, z_ref, b_ref, c_ref, dt_ref, la_ref, lat_ref, d_ref, nw_ref, o_ref, state_ref):
    L = SSD_CHUNK

    @pl.when(pl.program_id(1) == 0)
    def _():
        state_ref[...] = jnp.zeros_like(state_ref)

    lane = lax.broadcasted_iota(jnp.int32, (1, LANES), 1)
    low = lane < LANES // 2
    row = lax.broadcasted_iota(jnp.int32, (L, L), 0)
    col = lax.broadcasted_iota(jnp.int32, (L, L), 1)
    causal = col <= row

    for ci in range(xs_ref.shape[0] // L):
        _ssd_chunk(slice(ci * L, (ci + 1) * L), xs_ref, z_ref, b_ref, c_ref, dt_ref, la_ref, lat_ref, d_ref, nw_ref,
                   o_ref, state_ref, low, causal)


def _ssd_chunk(rs, xs_ref, z_ref, b_ref, c_ref, dt_ref, la_ref, lat_ref, d_ref, nw_ref, o_ref, state_ref,
               low, causal):
    L = SSD_CHUNK
    hg = SSD_HEADS // SSD_GROUPS
    gw = hg * SSD_HEAD_DIM
    dt = dt_ref[rs, :]
    la = la_ref[rs, :]
    la_t = lat_ref[:, rs]

    for g in range(SSD_GROUPS):
        bg = b_ref[rs, g * SSD_STATE:(g + 1) * SSD_STATE]
        cg = c_ref[rs, g * SSD_STATE:(g + 1) * SSD_STATE]
        cb = _dot_nt(cg, bg)
        bg_t = bg.astype(F32).T.astype(BF16)
        la_bc = [jnp.broadcast_to(la[:, h:h + 1], (L, LANES)) for h in range(g * hg, (g + 1) * hg)]
        dt_bc = [jnp.broadcast_to(dt[:, h:h + 1], (L, LANES)) for h in range(g * hg, (g + 1) * hg)]
        y_parts = []
        for pp in range(hg // 2):
            h0 = g * hg + 2 * pp
            ps = slice(h0 * SSD_HEAD_DIM, (h0 + 2) * SSD_HEAD_DIM)
            xs = xs_ref[rs, ps].astype(F32)
            dt_pair = jnp.where(low, dt_bc[2 * pp], dt_bc[2 * pp + 1])
            la_pair = jnp.where(low, la_bc[2 * pp], la_bc[2 * pp + 1])
            xdt = xs * dt_pair
            xdt_b = xdt.astype(BF16)
            y = jnp.zeros((L, LANES), F32)
            for hh in range(2):
                seg = la_bc[2 * pp + hh] - la_t[h0 + hh:h0 + hh + 1, :]
                w = (cb * jnp.exp2(jnp.where(causal, seg, -jnp.inf))).astype(BF16)
                keep = low if hh == 0 else jnp.logical_not(low)
                y = y + _dot(w, jnp.where(keep, xdt_b, jnp.zeros_like(xdt_b)))
            st = state_ref[:, ps]
            y = y + _dot(cg, st.astype(BF16)) * jnp.exp2(la_pair)
            la_end = la_pair[L - 1:L, :]
            to_end = jnp.exp2(la_end - la_pair)
            state_ref[:, ps] = st * jnp.exp2(la_end) + _dot(bg_t, (xdt * to_end).astype(BF16))
            y = y + d_ref[:, ps] * xs
            zz = z_ref[rs, ps].astype(F32)
            y_parts.append(y * (zz * jax.nn.sigmoid(zz)))
        yg = jnp.concatenate(y_parts, axis=1)
        ms = jnp.mean(jnp.square(yg), axis=-1, keepdims=True)
        gs = slice(g * gw, (g + 1) * gw)
        o_ref[rs, gs] = (yg * lax.rsqrt(ms + RMS_EPS) * nw_ref[:, gs]).astype(o_ref.dtype)


def _ssd_scan(xs, z, bm, cm, dt, la, la_t, d_row, norm_w, batch, seq):
    t = xs.shape[0]
    rows = min(SSD_SCAN_CHUNKS * SSD_CHUNK, seq)
    nc = seq // rows
    assert SSD_INNER // SSD_GROUPS == SSD_BC
    row = lambda w: pl.BlockSpec((rows, w), lambda bi, ci: (bi * nc + ci, 0))
    consts = [d_row, norm_w]
    return pl.pallas_call(
        _ssd_scan_kernel,
        grid=(batch, nc),
        in_specs=[row(SSD_INNER), row(SSD_INNER), row(SSD_BC), row(SSD_BC), row(LANES), row(LANES),
                  pl.BlockSpec((LANES, rows), lambda bi, ci: (0, bi * nc + ci))]
        + [_const_spec(w.shape) for w in consts],
        out_specs=row(SSD_INNER),
        out_shape=jax.ShapeDtypeStruct((t, SSD_INNER), BF16),
        scratch_shapes=[pltpu.VMEM((SSD_STATE, SSD_INNER), F32)],
        compiler_params=_params(("parallel", "arbitrary")),
        name="ssd_scan",
    )(xs, z, bm, cm, dt, la, la_t, *consts)


def _pad_cols(w, n):
    return jnp.concatenate([w, jnp.zeros((w.shape[0], n), w.dtype)], axis=1)


def _attention_layer(x2, tabs, batch, seq, w_in, q_norm, w_uq, kv_norm, w_ukv, w_out):
    o1 = MLA_Q_LORA + MLA_KV_LORA
    zc = lambda n: jnp.zeros((D_MODEL, n), w_in.dtype)
    o2 = o1 + MLA_ROPE + 2 * MOBA_WIDTH
    w_in2 = jnp.concatenate([w_in[:, :o1], zc(MLA_NOPE), w_in[:, o1:o1 + MLA_ROPE],
                             zc(LANES - MLA_NOPE - MLA_ROPE), w_in[:, o1 + MLA_ROPE:o2]], axis=1).astype(BF16)
    w_vb_t = w_in[:, o2:].T.astype(BF16)
    dqk = MLA_NOPE + MLA_ROPE
    w_uq2 = jnp.pad(w_uq.reshape(MLA_Q_LORA, MLA_HEADS, dqk), ((0, 0), (0, 0), (0, HEAD_PAD - dqk)))
    w_uq2 = w_uq2.reshape(MLA_Q_LORA, MLA_HEADS * HEAD_PAD).astype(BF16)
    w_kv3 = w_ukv.reshape(MLA_KV_LORA, MLA_HEADS, MLA_NOPE + MLA_V)
    w_k2 = jnp.pad(w_kv3[:, :, :MLA_NOPE], ((0, 0), (0, 0), (0, HEAD_PAD - MLA_NOPE)))
    w_k2 = w_k2.reshape(MLA_KV_LORA, MLA_HEADS * HEAD_PAD).astype(BF16)
    w_v_t = w_kv3[:, :, MLA_NOPE:].reshape(MLA_KV_LORA, MLA_HEADS * MLA_V).T.astype(BF16)

    qm, km, vm, qb, kb, vb, kmean = _att_proj(x2, tabs, w_in2, w_uq2, w_k2, w_v_t, w_vb_t,
                                              q_norm.reshape(1, -1), kv_norm.reshape(1, -1))
    o_mla = _attention(qm, km, vm, batch, seq)
    qa = _moba_select(qb, kmean, batch, seq)
    pos = jnp.arange(seq)
    onehot = (pos[:, None] // MOBA_BLOCK == jnp.arange(LANES)[None, :]).astype(BF16)
    o_moba = _attention(qa, kb, vb, batch, seq, onehot=onehot)
    n_mla = MLA_HEADS * MLA_V
    w_o = w_out.astype(BF16)
    return [o_mla, o_moba], [w_o[:n_mla], w_o[n_mla:]]


def _ssd_layer(x2, batch, seq, w_in, conv_w, conv_b, dt_bias, a_log, d_skip, norm_w, w_out):
    w_in2 = _pad_cols(w_in, LANES - SSD_HEADS).astype(BF16)
    dtb = _pad_cols(dt_bias.reshape(1, -1), LANES - SSD_HEADS)
    a_row = _pad_cols((-jnp.exp(a_log.astype(F32))).reshape(1, -1), LANES - SSD_HEADS)
    z, xs, bm, cm, dt, la, la_t = _ssd_proj(x2, w_in2, conv_w, conv_b.reshape(1, -1), dtb, a_row, seq)
    d_row = jnp.repeat(d_skip.astype(F32), SSD_HEAD_DIM).reshape(1, -1)
    y = _ssd_scan(xs, z, bm, cm, dt, la, la_t, d_row, norm_w.reshape(1, -1), batch, seq)
    return [y], [w_out.astype(BF16)]


def kernel(x, p, positions, att_w_in, mla_q_norm, mla_w_uq, mla_kv_norm, mla_w_ukv, att_w_out, ssd_w_in, ssd_conv_w, ssd_conv_b, ssd_dt_bias, ssd_a_log, ssd_d, ssd_norm, ssd_w_out, ln_mix_g, ln_mix_b, ffn_w_up, ffn_conv_w, ffn_conv_b, ffn_w_down, ln_ffn_g, ln_ffn_b, ple_w_gate, ple_w_proj):
    batch, seq, _ = x.shape
    depth = p.shape[0]
    t = batch * seq
    x2 = x.reshape(t, D_MODEL)
    tabs = _rope_tables(positions)
    for i in range(depth):
        j = i // 2
        if i % 2 == 0:
            acts, ws = _attention_layer(x2, tabs, batch, seq, att_w_in[j], mla_q_norm[j], mla_w_uq[j],
                                        mla_kv_norm[j], mla_w_ukv[j], att_w_out[j])
        else:
            acts, ws = _ssd_layer(x2, batch, seq, ssd_w_in[j], ssd_conv_w[j], ssd_conv_b[j], ssd_dt_bias[j],
                                  ssd_a_log[j], ssd_d[j], ssd_norm[j], ssd_w_out[j])
        x2 = _mix_ffn_ple(acts, ws, x2, p.reshape(depth * t, PLE_DIM), i,
                          ln_mix_g[i].reshape(1, -1), ln_mix_b[i].reshape(1, -1),
                          ffn_w_up[i].astype(BF16), ffn_conv_w[i], ffn_conv_b[i].reshape(1, -1),
                          ffn_w_down[i].astype(BF16), ln_ffn_g[i].reshape(1, -1), ln_ffn_b[i].reshape(1, -1),
                          ple_w_gate[i].astype(BF16), ple_w_proj[i].astype(BF16), seq)
    return x2.reshape(batch, seq, D_MODEL)
```

```python
import functools

import jax
import jax.numpy as jnp
from jax import lax
from jax.experimental import pallas as pl
from jax.experimental.pallas import tpu as pltpu

D_MODEL = 1024
PLE_DIM = 256
ROPE_THETA = 500000.0
MLA_HEADS = 8
MLA_Q_LORA = 256
MLA_KV_LORA = 128
MLA_NOPE = 64
MLA_ROPE = 32
MLA_V = 64
MOBA_HEADS = 8
MOBA_HEAD_DIM = 64
MOBA_ROT = MOBA_HEAD_DIM // 4
MOBA_BLOCK = 256
MOBA_TOPK = 3
MOBA_WIDTH = MOBA_HEADS * MOBA_HEAD_DIM
SSD_INNER = 2 * D_MODEL
SSD_HEAD_DIM = 64
SSD_HEADS = SSD_INNER // SSD_HEAD_DIM
SSD_GROUPS = 4
SSD_STATE = 128
SSD_CONV = 4
SSD_CHUNK = 128
SSD_CONV_DIM = SSD_INNER + 2 * SSD_GROUPS * SSD_STATE
D_FF = 2816
FFN_CONV = 3
LN_EPS = 1e-5
RMS_EPS = 1e-6
DEPTH = 2
DEEPNORM_ALPHA = (2 * DEPTH) ** 0.25

LANES = 128
SUBLANES = 8
HEAD_PAD = LANES
VMEM_LIMIT = 56 * 1024 * 1024
NEG_BIG = -1e30
INV_SQRT2 = 0.7071067811865476
LOG2_E = 1.4426950408889634

F32 = jnp.float32
BF16 = jnp.bfloat16


def _dot(a, b):
    return jnp.dot(a, b, preferred_element_type=F32)


def _dot_nt(a, b):
    return lax.dot_general(a, b, (((1,), (1,)), ((), ())), preferred_element_type=F32)


def _const_spec(shape):
    return pl.BlockSpec(shape, lambda *_: (0,) * len(shape), pipeline_mode=pl.Buffered(1))


def _params(semantics):
    return pltpu.CompilerParams(dimension_semantics=semantics, vmem_limit_bytes=VMEM_LIMIT)


ROPE_PACK = 4


def _split3(a):
    hi = a.astype(BF16)
    r1 = a - hi.astype(F32)
    mid = r1.astype(BF16)
    lo = (r1 - mid.astype(F32)).astype(BF16)
    return jnp.concatenate([hi, mid, lo], axis=1)


def _rope_table_kernel(pos_ref, f_ref, ec_ref, e1_ref, e2_ref, bias_ref, c_ref, s1_ref, s2_ref):
    pos = pos_ref[...].astype(F32)
    ang = pos[:, 0:1] * f_ref[0:1, :]
    for u in range(1, ROPE_PACK):
        ang = ang + pos[:, u:u + 1] * f_ref[u:u + 1, :]
    cos3, sin3 = _split3(jnp.cos(ang)), _split3(jnp.sin(ang))
    rows = pos_ref.shape[0]
    for r in range(2):
        c = _dot(cos3, ec_ref[r]) + bias_ref[r:r + 1, :]
        s1 = _dot(sin3, e1_ref[r])
        s2 = _dot(sin3, e2_ref[r])
        for u in range(ROPE_PACK):
            ls = slice(u * LANES, (u + 1) * LANES)
            dst = pl.ds(u, rows, stride=ROPE_PACK)
            c_ref[r, dst, :] = c[:, ls]
            s1_ref[r, dst, :] = s1[:, ls]
            s2_ref[r, dst, :] = s2[:, ls]


def _rope_tables(positions):
    t = positions.size
    inv_m = ROPE_THETA ** (-jnp.arange(0, MLA_ROPE, 2, dtype=F32) / MLA_ROPE)
    inv_b = ROPE_THETA ** (-jnp.arange(0, MOBA_ROT, 2, dtype=F32) / MOBA_ROT)
    hm, hb = MLA_ROPE // 2, MOBA_ROT // 2
    slot = LANES // ROPE_PACK
    assert hm + hb <= slot and t % ROPE_PACK == 0
    f_tok = jnp.concatenate([inv_m, inv_b, jnp.zeros((slot - hm - hb,), F32)])
    f = jnp.kron(jnp.eye(ROPE_PACK, dtype=F32), f_tok[None, :])
    src = jnp.arange(slot)[:, None]
    dst = jnp.arange(LANES)[None, :]
    m_x1 = (src < hm) & (dst == MLA_NOPE + src)
    m_x2 = (src < hm) & (dst == MLA_NOPE + hm + src)
    i_b = src - hm
    in_b = (i_b >= 0) & (i_b < hb)
    b_x1 = in_b & (dst % MOBA_HEAD_DIM == i_b)
    b_x2 = in_b & (dst % MOBA_HEAD_DIM == hb + i_b)

    def expand(m, sign=1.0):
        e = jnp.kron(jnp.eye(ROPE_PACK, dtype=F32), sign * m.astype(F32))
        return jnp.concatenate([e, e, e], axis=0).astype(BF16)

    ec = jnp.stack([expand(m_x1 | m_x2), expand(b_x1 | b_x2)])
    e1 = jnp.stack([expand(m_x1, -1.0), expand(b_x1, -1.0)])
    e2 = jnp.stack([expand(m_x2), expand(b_x2)])
    bias = jnp.stack([jnp.tile(1.0 - jnp.sum((m_x1 | m_x2).astype(F32), axis=0), ROPE_PACK),
                      jnp.tile(1.0 - jnp.sum((b_x1 | b_x2).astype(F32), axis=0), ROPE_PACK)])
    rows = t // ROPE_PACK
    tr = min(rows, 512)
    out = jax.ShapeDtypeStruct((2, t, LANES), F32)
    tab_spec = pl.BlockSpec((2, tr * ROPE_PACK, LANES), lambda i: (0, i, 0))
    consts = [f, ec, e1, e2, bias]
    return pl.pallas_call(
        _rope_table_kernel,
        grid=(rows // tr,),
        in_specs=[pl.BlockSpec((tr, ROPE_PACK), lambda i: (i, 0))] + [_const_spec(w.shape) for w in consts],
        out_specs=[tab_spec, tab_spec, tab_spec],
        out_shape=[out, out, out],
        compiler_params=_params(("parallel",)),
        name="rope_tables",
    )(positions.reshape(rows, ROPE_PACK), *consts)


def _rope(t, c, s1, s2, half):
    w = t.shape[-1]
    return t * c + pltpu.roll(t, w - half, 1) * s1 + pltpu.roll(t, half, 1) * s2


def _rms(x, g):
    ms = jnp.mean(jnp.square(x), axis=-1, keepdims=True)
    return x * lax.rsqrt(ms + RMS_EPS) * g


def _layer_norm(x, g, b):
    mu = jnp.mean(x, axis=-1, keepdims=True)
    xc = x - mu
    var = jnp.mean(jnp.square(xc), axis=-1, keepdims=True)
    return xc * lax.rsqrt(var + LN_EPS) * g + b


ATT_PROJ_TM = 512


def _att_proj_kernel(x_ref, win_ref, wuq_ref, wk_ref, wv_ref, wvb_ref, qn_ref, kvn_ref, c_ref, s1_ref, s2_ref,
                     qm_ref, km_ref, vm_ref, qb_ref, kb_ref, vb_ref, kmean_ref):
    xb = x_ref[...].astype(BF16)
    cm, s1m, s2m = c_ref[0], s1_ref[0], s2_ref[0]
    cb, s1b, s2b = c_ref[1], s1_ref[1], s2_ref[1]
    hm, hb = MLA_ROPE // 2, MOBA_ROT // 2
    mla_scale = (MLA_NOPE + MLA_ROPE) ** -0.5 * LOG2_E
    moba_scale = MOBA_HEAD_DIM ** -0.5 * LOG2_E

    h_lat = _dot(xb, win_ref[:, 0:512])
    c_q = h_lat[:, 0:MLA_Q_LORA]
    c_kv = h_lat[:, MLA_Q_LORA:MLA_Q_LORA + MLA_KV_LORA]
    hq = _dot(xb, win_ref[:, 512:1024])
    hk = _dot(xb, win_ref[:, 1024:1536])
    vb_ref[...] = _dot_nt(wvb_ref[...], xb).astype(BF16)
    k_rope = _rope(h_lat[:, 384:512], cm, s1m, s2m, hm)

    q = _dot(_rms(c_q, qn_ref[...]).astype(BF16), wuq_ref[...])
    for h in range(MLA_HEADS):
        sl = slice(h * HEAD_PAD, (h + 1) * HEAD_PAD)
        qm_ref[:, sl] = (_rope(q[:, sl], cm, s1m, s2m, hm) * mla_scale).astype(BF16)

    ckv = _rms(c_kv, kvn_ref[...]).astype(BF16)
    k = _dot(ckv, wk_ref[...])
    for h in range(MLA_HEADS):
        sl = slice(h * HEAD_PAD, (h + 1) * HEAD_PAD)
        km_ref[:, sl] = (k[:, sl] + k_rope).astype(BF16)
    vm_ref[...] = _dot_nt(wv_ref[...], ckv).astype(BF16)

    for g in range(MOBA_WIDTH // LANES):
        sl = slice(g * LANES, (g + 1) * LANES)
        qb_ref[:, sl] = (_rope(hq[:, sl], cb, s1b, s2b, hb) * moba_scale).astype(BF16)
        kr = _rope(hk[:, sl], cb, s1b, s2b, hb)
        kb_ref[:, sl] = kr.astype(BF16)
        for r in range(ATT_PROJ_TM // MOBA_BLOCK):
            rows = slice(r * MOBA_BLOCK, (r + 1) * MOBA_BLOCK)
            kmean_ref[r, :, sl] = jnp.mean(kr[rows], axis=0, keepdims=True)


def _att_proj(x2, tabs, w_in, w_uq, w_k, w_v_t, w_vb_t, q_norm, kv_norm):
    t = x2.shape[0]
    tm = ATT_PROJ_TM
    c, s1, s2 = tabs
    row = lambda w: pl.BlockSpec((tm, w), lambda i: (i, 0))
    col = lambda h: pl.BlockSpec((h, tm), lambda i: (0, i))
    tab = pl.BlockSpec((2, tm, LANES), lambda i: (0, i, 0))
    bf = lambda w: jax.ShapeDtypeStruct((t, w), BF16)
    bf_t = lambda h: jax.ShapeDtypeStruct((h, t), BF16)
    nblk = tm // MOBA_BLOCK
    consts = [w_in, w_uq, w_k, w_v_t, w_vb_t, q_norm, kv_norm]
    return pl.pallas_call(
        _att_proj_kernel,
        grid=(t // tm,),
        in_specs=[row(D_MODEL)] + [_const_spec(w.shape) for w in consts] + [tab, tab, tab],
        out_specs=[row(1024), row(1024), col(512), row(512), row(512), col(512),
                   pl.BlockSpec((nblk, 1, MOBA_WIDTH), lambda i: (i, 0, 0))],
        out_shape=[bf(1024), bf(1024), bf_t(512), bf(512), bf(512), bf_t(512),
                   jax.ShapeDtypeStruct((t // MOBA_BLOCK, 1, MOBA_WIDTH), F32)],
        compiler_params=_params(("parallel",)),
        name="att_proj",
    )(x2, *consts, c, s1, s2)


SELECT_BLOCKS = 4


def _moba_select_kernel(q_ref, kab_ref, r_ref, lc_ref, qa_ref, *, nb):
    lane = lax.broadcasted_iota(jnp.int32, (1, LANES), 1)
    tie_first = lc_ref[1:2, :]
    low = lane < (LANES // 2)
    blocks = q_ref.shape[0] // MOBA_BLOCK
    gates = []
    for bk in range(blocks):
        rows = slice(bk * MOBA_BLOCK, (bk + 1) * MOBA_BLOCK)
        for h in range(MOBA_HEADS):
            qp = q_ref[rows, (h // 2) * LANES:(h // 2 + 1) * LANES]
            qm = jnp.where(low if h % 2 == 0 else jnp.logical_not(low), qp, jnp.zeros_like(qp))
            qa_ref[rows, 2 * h * LANES:(2 * h + 1) * LANES] = qm
            gates.append(_dot(qm, kab_ref[0, h]))
    beats = []
    for i, ab in enumerate(gates):
        own = pl.program_id(1) * blocks + i // MOBA_HEADS
        pair_valid = jnp.where(lc_ref[0:1, :] < own.astype(F32), 1.0, 0.0)
        a, b = ab[:, 0:LANES], ab[:, LANES:2 * LANES]
        beats.append((jnp.where(b > a, 1.0, jnp.where(b == a, tie_first, 0.0)) * pair_valid).astype(BF16))
    counts = [_dot(bt, r_ref[...]) for bt in beats]
    for i, cnt in enumerate(counts):
        bk, h = divmod(i, MOBA_HEADS)
        own = pl.program_id(1) * blocks + bk
        rest_pen = jnp.where((lane > own) & (lane < nb), NEG_BIG, 0.0)
        pen = jnp.where(lane < own, jnp.where(cnt < MOBA_TOPK, 0.0, NEG_BIG), rest_pen)
        qa_ref[bk * MOBA_BLOCK:(bk + 1) * MOBA_BLOCK, (2 * h + 1) * LANES:(2 * h + 2) * LANES] = pen.astype(BF16)


def _moba_select(qb, kmean, batch, seq):
    t = qb.shape[0]
    nb = seq // MOBA_BLOCK
    assert nb * nb <= LANES
    km = kmean.reshape(batch, nb, MOBA_HEADS, MOBA_HEAD_DIM).transpose(0, 2, 3, 1)
    a = jnp.repeat(km, nb, axis=-1)
    b = jnp.tile(km, (1, 1, 1, nb))
    zc = jnp.zeros(km.shape[:3] + (LANES - nb * nb,), F32)
    ab = jnp.concatenate([a, zc, b, zc], axis=-1)
    zr = jnp.zeros_like(ab)
    odd = (jnp.arange(MOBA_HEADS) % 2 == 1)[None, :, None, None]
    kab = jnp.where(odd, jnp.concatenate([zr, ab], axis=2), jnp.concatenate([ab, zr], axis=2)).astype(BF16)
    cidx = jnp.arange(LANES)
    rmat = ((cidx[:, None] // nb == cidx[None, :]) & (cidx[:, None] < nb * nb)).astype(BF16)
    used = cidx < nb * nb
    lane_consts = jnp.zeros((SUBLANES, LANES), F32)
    lane_consts = lane_consts.at[0].set(jnp.where(used, cidx % nb, nb).astype(F32))
    lane_consts = lane_consts.at[1].set((used & (cidx % nb < cidx // nb)).astype(F32))
    tq = min(SELECT_BLOCKS, nb) * MOBA_BLOCK
    return pl.pallas_call(
        functools.partial(_moba_select_kernel, nb=nb),
        grid=(batch, seq // tq),
        in_specs=[pl.BlockSpec((tq, MOBA_WIDTH), lambda bi, qi: (bi * (seq // tq) + qi, 0)),
                  pl.BlockSpec((1, MOBA_HEADS, LANES, 2 * LANES), lambda bi, qi: (bi, 0, 0, 0)),
                  _const_spec((LANES, LANES)), _const_spec((SUBLANES, LANES))],
        out_specs=pl.BlockSpec((tq, MOBA_HEADS * 2 * LANES), lambda bi, qi: (bi * (seq // tq) + qi, 0)),
        out_shape=jax.ShapeDtypeStruct((t, MOBA_HEADS * 2 * LANES), BF16),
        compiler_params=_params(("parallel", "parallel")),
        name="moba_select",
    )(qb, kab, rmat, lane_consts)


ATT_TILE = 256
ATT_PAIRS = 4


def _attn_kernel(*refs, moba):
    if moba:
        q_ref, k_ref, oh_ref, v_ref, o_ref, va_ref, vb_ref = refs
    else:
        q_ref, k_ref, v_ref, o_ref, va_ref, vb_ref = refs
        oh_ref = None
    seq = q_ref.shape[0]
    tq = ATT_TILE
    pairs = v_ref.shape[0] // LANES
    heads = 2 * pairs
    qw = q_ref.shape[1] // heads
    sub = lax.broadcasted_iota(jnp.int32, (LANES, 1), 0)
    half = LANES // 2
    for p in range(pairs):
        ps = slice(p * LANES, (p + 1) * LANES)
        v = v_ref[ps, :].astype(F32)
        va_ref[ps, :] = jnp.where(sub < half, v, jnp.where(sub == half, 1.0, 0.0)).astype(BF16)
        vb_ref[ps, :] = jnp.where(sub >= half, v, jnp.where(sub == 0, 1.0, 0.0)).astype(BF16)
    key = lax.broadcasted_iota(jnp.int32, (tq, tq), 0)
    qry = lax.broadcasted_iota(jnp.int32, (tq, tq), 1)
    causal = key <= qry

    def load_q(r0, h):
        return q_ref[pl.ds(r0, tq), h * qw:(h + 1) * qw]

    def load_k(c0, h):
        if moba:
            ps = slice((h // 2) * LANES, (h // 2 + 1) * LANES)
            return jnp.concatenate([k_ref[pl.ds(c0, tq), ps], oh_ref[pl.ds(c0, tq), :]], axis=1)
        return k_ref[pl.ds(c0, tq), h * HEAD_PAD:(h + 1) * HEAD_PAD]

    def load_v(c0, h):
        ps = slice((h // 2) * LANES, (h // 2 + 1) * LANES)
        return (va_ref if h % 2 == 0 else vb_ref)[ps, pl.ds(c0, tq)]

    def scores_of(r0, c0):
        return [_dot_nt(load_k(c0, h), load_q(r0, h)) for h in range(heads)]

    def value_update(c0, accs, alphas, ps):
        return tuple(alphas[h] * accs[h] + _dot(load_v(c0, h), ps[h]) for h in range(heads))

    def softmax_update(scores, ms):
        out = []
        for h in range(heads):
            m_new = jnp.maximum(ms[h], jnp.max(scores[h], axis=0, keepdims=True))
            alpha = jnp.exp2(ms[h] - m_new)
            p = jnp.exp2((scores[h] - m_new).astype(BF16))
            out.append((m_new, alpha, p))
        return tuple(zip(*out))

    def q_tile(qi):
        r0 = qi * tq

        rep = lambda a: (a,) * heads
        scores = [jnp.where(causal, s_t, -jnp.inf) for s_t in scores_of(r0, r0)]
        ms, alphas, ps = softmax_update(scores, rep(jnp.full((1, tq), -jnp.inf, F32)))

        def prev_tile(j):
            if isinstance(j, int):
                return (qi if j == 0 else j - 1) * tq
            return pl.multiple_of(jnp.where(j == 0, qi, j - 1) * tq, tq)

        def body(j, carry):
            ms, accs, alphas, ps = carry
            scores = scores_of(r0, pl.multiple_of(j * tq, tq))
            accs = value_update(prev_tile(j), accs, alphas, ps)
            ms, alphas, ps = softmax_update(scores, ms)
            return ms, accs, alphas, ps

        init = (ms, rep(jnp.zeros((LANES, tq), F32)), alphas, ps)
        ms, accs, alphas, ps = lax.fori_loop(0, qi, body, init, unroll=4)
        accs = value_update(prev_tile(qi), accs, alphas, ps)
        for p in range(pairs):
            a0, a1 = accs[2 * p], accs[2 * p + 1]
            o_t = jnp.where(sub < half, a0 * (1.0 / a0[half:half + 1, :]), a1 * (1.0 / a1[0:1, :]))
            o_ref[pl.ds(r0, tq), p * LANES:(p + 1) * LANES] = o_t.T.astype(o_ref.dtype)

    for qi in range(seq // tq):
        q_tile(qi)


def _attention(q, k, v_t, batch, seq, onehot=None):
    t = q.shape[0]
    moba = onehot is not None
    pairs = v_t.shape[0] // LANES
    steps = pairs // ATT_PAIRS
    blk = lambda a: pl.BlockSpec((seq, a.shape[1] // steps), lambda bi, p: (bi, p))
    vw = ATT_PAIRS * LANES
    in_specs = [blk(q), blk(k)]
    args = [q, k]
    if moba:
        in_specs.append(_const_spec((seq, LANES)))
        args.append(onehot)
    in_specs.append(pl.BlockSpec((vw, seq), lambda bi, p: (p, bi)))
    args.append(v_t)
    return pl.pallas_call(
        functools.partial(_attn_kernel, moba=moba),
        grid=(batch, steps),
        in_specs=in_specs,
        out_specs=pl.BlockSpec((seq, vw), lambda bi, p: (bi, p)),
        out_shape=jax.ShapeDtypeStruct((t, pairs * LANES), BF16),
        scratch_shapes=[pltpu.VMEM((vw, seq), BF16), pltpu.VMEM((vw, seq), BF16)],
        compiler_params=_params(("parallel", "parallel")),
        name="moba_attn" if moba else "mla_attn",
    )(*args)


FFN_TM = 512
FFN_SUB = 256
FFN_CHUNK = 256


def _ffn_kernel(*refs, n_act, tiles_per_seq):
    acts = refs[:n_act]
    wos = refs[n_act:2 * n_act]
    (x_ref, p_ref, gm_ref, bm_ref, wup_ref, cw_ref, cb_ref, wdn_ref, g_ref, b_ref, wg_ref, wp_ref,
     o_ref, gbuf_ref) = refs[2 * n_act:]
    hist = SUBLANES
    sub = FFN_SUB
    n_sub = FFN_TM // sub
    n_chunks = D_FF // FFN_CHUNK

    @pl.when(pl.program_id(0) % tiles_per_seq == 0)
    def _():
        gbuf_ref[0:hist, :] = jnp.zeros((hist, D_FF), F32)

    mixes = []
    for s in range(n_sub):
        rows = slice(s * sub, (s + 1) * sub)
        m = _dot(acts[0][rows, :], wos[0][...])
        for a_ref, w_ref in zip(acts[1:], wos[1:]):
            m = m + _dot(a_ref[rows, :], w_ref[...])
        mixes.append(m)

    for s in range(n_sub):
        r0 = s * sub
        x = _layer_norm(DEEPNORM_ALPHA * x_ref[r0:r0 + sub, :] + mixes[s], gm_ref[...], bm_ref[...])
        xb = x.astype(BF16)

        def up_proj(c):
            lo = c * FFN_CHUNK
            return (_dot(xb, wup_ref[:, lo:lo + FFN_CHUNK]),
                    _dot(xb, wup_ref[:, D_FF + lo:D_FF + lo + FFN_CHUNK]))

        f = jnp.zeros((sub, D_MODEL), F32)
        nxt = up_proj(0)
        for c in range(n_chunks):
            cs = slice(c * FFN_CHUNK, (c + 1) * FFN_CHUNK)
            gate, up = nxt
            if c + 1 < n_chunks:
                nxt = up_proj(c + 1)
            g0 = hist + r0
            gbuf_ref[g0:g0 + sub, cs] = gate
            conv = cw_ref[FFN_CONV - 1:FFN_CONV, cs] * gate + cb_ref[:, cs]
            for k in range(FFN_CONV - 1):
                d = FFN_CONV - 1 - k
                conv = conv + cw_ref[k:k + 1, cs] * gbuf_ref[g0 - d:g0 - d + sub, cs]
            hid = 0.5 * conv * (1.0 + lax.erf(conv * INV_SQRT2)) * up
            f = f + _dot(hid.astype(BF16), wdn_ref[cs, :])
        proj = _dot(p_ref[r0:r0 + sub, :].astype(BF16), wp_ref[...])
        half = sub // 2
        for r in range(2):
            rows = slice(r * half, (r + 1) * half)
            y = _layer_norm(DEEPNORM_ALPHA * x[rows] + f[rows], g_ref[...], b_ref[...])
            gate = jax.nn.sigmoid(_dot(y.astype(BF16), wg_ref[...]))
            o_ref[r0 + r * half:r0 + (r + 1) * half, :] = y + gate * proj[rows]
    gbuf_ref[0:hist, :] = gbuf_ref[FFN_TM:FFN_TM + hist, :]


def _mix_ffn_ple(acts, w_outs, x2, p_all, layer, g_mix, b_mix, w_up, conv_w, conv_b, w_down, g, b, w_gate, w_proj,
                 seq):
    t = x2.shape[0]
    tm = FFN_TM
    row = lambda w: pl.BlockSpec((tm, w), lambda i: (i, 0))
    p_spec = pl.BlockSpec((tm, PLE_DIM), lambda i: (layer * (t // tm) + i, 0))
    consts = [g_mix, b_mix, w_up, conv_w, conv_b, w_down, g, b, w_gate, w_proj]
    return pl.pallas_call(
        functools.partial(_ffn_kernel, n_act=len(acts), tiles_per_seq=seq // tm),
        grid=(t // tm,),
        in_specs=[row(a.shape[1]) for a in acts] + [_const_spec(w.shape) for w in w_outs]
        + [row(D_MODEL), p_spec] + [_const_spec(w.shape) for w in consts],
        out_specs=row(D_MODEL),
        out_shape=jax.ShapeDtypeStruct((t, D_MODEL), F32),
        scratch_shapes=[pltpu.VMEM((tm + SUBLANES, D_FF), F32)],
        compiler_params=_params(("arbitrary",)),
        name="mix_ffn_ple",
    )(*acts, *w_outs, x2, p_all, *consts)


SSD_PROJ_TM = 256
SSD_PROJ_SUB = 256
SSD_PROJ_CHUNK = 256
SSD_BC = SSD_GROUPS * SSD_STATE


def _ssd_proj_kernel(x_ref, win_ref, cw_ref, cb_ref, dtb_ref, a_ref, tri_ref,
                     z_ref, xs_ref, b_ref, c_ref, dt_ref, la_ref, lat_ref, cbuf_ref, xb_ref, *, tiles_per_seq):
    tm = SSD_PROJ_TM
    sub = SSD_PROJ_SUB
    hist = SUBLANES
    ck = SSD_PROJ_CHUNK
    n_conv = SSD_CONV_DIM // ck

    @pl.when(pl.program_id(0) % tiles_per_seq == 0)
    def _():
        cbuf_ref[0:hist, :] = jnp.zeros((hist, SSD_CONV_DIM), F32)

    xb_ref[...] = x_ref[...].astype(BF16)
    for s in range(tm // sub):
        rs = slice(s * sub, (s + 1) * sub)
        hd = _dot(xb_ref[rs, :], win_ref[:, SSD_INNER + SSD_CONV_DIM:]) + dtb_ref[...]
        dt = jnp.maximum(hd, 0.0) + jnp.log1p(jnp.exp(-jnp.abs(hd)))
        dt_ref[rs, :] = dt
        la = jnp.dot(tri_ref[...], dt * a_ref[...], preferred_element_type=F32,
                     precision=lax.Precision.HIGHEST) * LOG2_E
        la_ref[rs, :] = la
        lat_ref[:, rs] = la.T
        for c in range(SSD_INNER // ck):
            z_ref[rs, c * ck:(c + 1) * ck] = _dot(xb_ref[rs, :], win_ref[:, c * ck:(c + 1) * ck]).astype(z_ref.dtype)

        def conv_in(c):
            return _dot(xb_ref[rs, :], win_ref[:, SSD_INNER + c * ck:SSD_INNER + (c + 1) * ck])

        nxt = conv_in(0)
        for c in range(n_conv):
            cs = slice(c * ck, (c + 1) * ck)
            h = nxt
            if c + 1 < n_conv:
                nxt = conv_in(c + 1)
            g0 = hist + s * sub
            cbuf_ref[g0:g0 + sub, cs] = h
            conv = cw_ref[SSD_CONV - 1:SSD_CONV, cs] * h + cb_ref[:, cs]
            for k in range(SSD_CONV - 1):
                d = SSD_CONV - 1 - k
                conv = conv + cw_ref[k:k + 1, cs] * cbuf_ref[g0 - d:g0 - d + sub, cs]
            act = (conv * jax.nn.sigmoid(conv)).astype(xs_ref.dtype)
            lo = c * ck
            if lo < SSD_INNER:
                xs_ref[rs, cs] = act
            elif lo < SSD_INNER + SSD_BC:
                b_ref[rs, lo - SSD_INNER:lo - SSD_INNER + ck] = act
            else:
                c_ref[rs, lo - SSD_INNER - SSD_BC:lo - SSD_INNER - SSD_BC + ck] = act
    cbuf_ref[0:hist, :] = cbuf_ref[tm:tm + hist, :]


def _ssd_proj(x2, w_in, conv_w, conv_b, dt_bias, a_row, seq):
    t = x2.shape[0]
    tm = SSD_PROJ_TM
    assert SSD_BC % SSD_PROJ_CHUNK == 0 and SSD_INNER % SSD_PROJ_CHUNK == 0 and SSD_PROJ_SUB % SSD_CHUNK == 0
    row = lambda w: pl.BlockSpec((tm, w), lambda i: (i, 0))
    out = lambda w, dt: jax.ShapeDtypeStruct((t, w), dt)
    idx = jnp.arange(SSD_PROJ_SUB)
    tri = ((idx[:, None] // SSD_CHUNK == idx[None, :] // SSD_CHUNK) & (idx[None, :] <= idx[:, None])).astype(F32)
    consts = [w_in, conv_w, conv_b, dt_bias, a_row, tri]
    return pl.pallas_call(
        functools.partial(_ssd_proj_kernel, tiles_per_seq=seq // tm),
        grid=(t // tm,),
        in_specs=[row(D_MODEL)] + [_const_spec(w.shape) for w in consts],
        out_specs=[row(SSD_INNER), row(SSD_INNER), row(SSD_BC), row(SSD_BC), row(LANES), row(LANES),
                   pl.BlockSpec((LANES, tm), lambda i: (0, i))],
        out_shape=[out(SSD_INNER, BF16), out(SSD_INNER, BF16), out(SSD_BC, BF16), out(SSD_BC, BF16),
                   out(LANES, F32), out(LANES, F32), jax.ShapeDtypeStruct((LANES, t), F32)],
        scratch_shapes=[pltpu.VMEM((tm + SUBLANES, SSD_CONV_DIM), F32), pltpu.VMEM((tm, D_MODEL), BF16)],
        compiler_params=_params(("arbitrary",)),
        name="ssd_proj",
    )(x2, *consts)


SSD_SCAN_CHUNKS = 8


def _ssd_scan_kernel(xs_ref, z_ref, b_ref, c_ref, dt_ref, la_ref, lat_ref, d_ref, nw_ref, o_ref, state_ref):
    L = SSD_CHUNK

    @pl.when(pl.program_id(1) == 0)
    def _():
        state_ref[...] = jnp.zeros_like(state_ref)

    lane = lax.broadcasted_iota(jnp.int32, (1, LANES), 1)
    low = lane < LANES // 2
    row = lax.broadcasted_iota(jnp.int32, (L, L), 0)
    col = lax.broadcasted_iota(jnp.int32, (L, L), 1)
    causal = col <= row

    for ci in range(xs_ref.shape[0] // L):
        _ssd_chunk(slice(ci * L, (ci + 1) * L), xs_ref, z_ref, b_ref, c_ref, dt_ref, la_ref, lat_ref, d_ref, nw_ref,
                   o_ref, state_ref, low, causal)


def _ssd_chunk(rs, xs_ref, z_ref, b_ref, c_ref, dt_ref, la_ref, lat_ref, d_ref, nw_ref, o_ref, state_ref,
               low, causal):
    L = SSD_CHUNK
    hg = SSD_HEADS // SSD_GROUPS
    gw = hg * SSD_HEAD_DIM
    dt = dt_ref[rs, :]
    la = la_ref[rs, :]
    la_t = lat_ref[:, rs]

    for g in range(SSD_GROUPS):
        bg = b_ref[rs, g * SSD_STATE:(g + 1) * SSD_STATE]
        cg = c_ref[rs, g * SSD_STATE:(g + 1) * SSD_STATE]
        cb = _dot_nt(cg, bg)
        bg_t = bg.astype(F32).T.astype(BF16)
        la_bc = [jnp.broadcast_to(la[:, h:h + 1], (L, LANES)) for h in range(g * hg, (g + 1) * hg)]
        dt_bc = [jnp.broadcast_to(dt[:, h:h + 1], (L, LANES)) for h in range(g * hg, (g + 1) * hg)]
        y_parts = []
        for pp in range(hg // 2):
            h0 = g * hg + 2 * pp
            ps = slice(h0 * SSD_HEAD_DIM, (h0 + 2) * SSD_HEAD_DIM)
            xs = xs_ref[rs, ps].astype(F32)
            dt_pair = jnp.where(low, dt_bc[2 * pp], dt_bc[2 * pp + 1])
            la_pair = jnp.where(low, la_bc[2 * pp], la_bc[2 * pp + 1])
            xdt = xs * dt_pair
            xdt_b = xdt.astype(BF16)
            y = jnp.zeros((L, LANES), F32)
            for hh in range(2):
                seg = la_bc[2 * pp + hh] - la_t[h0 + hh:h0 + hh + 1, :]
                w = (cb * jnp.exp2(jnp.where(causal, seg, -jnp.inf))).astype(BF16)
                keep = low if hh == 0 else jnp.logical_not(low)
                y = y + _dot(w, jnp.where(keep, xdt_b, jnp.zeros_like(xdt_b)))
            st = state_ref[:, ps]
            y = y + _dot(cg, st.astype(BF16)) * jnp.exp2(la_pair)
            la_end = la_pair[L - 1:L, :]
            to_end = jnp.exp2(la_end - la_pair)
            state_ref[:, ps] = st * jnp.exp2(la_end) + _dot(bg_t, (xdt * to_end).astype(BF16))
            y = y + d_ref[:, ps] * xs
            zz = z_ref[rs, ps].astype(F32)
            y_parts.append(y * (zz * jax.nn.sigmoid(zz)))
        yg = jnp.concatenate(y_parts, axis=1)
        ms = jnp.mean(jnp.square(yg), axis=-1, keepdims=True)
        gs = slice(g * gw, (g + 1) * gw)
        o_ref[rs, gs] = (yg * lax.rsqrt(ms + RMS_EPS) * nw_ref[:, gs]).astype(o_ref.dtype)


def _ssd_scan(xs, z, bm, cm, dt, la, la_t, d_row, norm_w, batch, seq):
    t = xs.shape[0]
    rows = min(SSD_SCAN_CHUNKS * SSD_CHUNK, seq)
    nc = seq // rows
    assert SSD_INNER // SSD_GROUPS == SSD_BC
    row = lambda w: pl.BlockSpec((rows, w), lambda bi, ci: (bi * nc + ci, 0))
    consts = [d_row, norm_w]
    return pl.pallas_call(
        _ssd_scan_kernel,
        grid=(batch, nc),
        in_specs=[row(SSD_INNER), row(SSD_INNER), row(SSD_BC), row(SSD_BC), row(LANES), row(LANES),
                  pl.BlockSpec((LANES, rows), lambda bi, ci: (0, bi * nc + ci))]
        + [_const_spec(w.shape) for w in consts],
        out_specs=row(SSD_INNER),
        out_shape=jax.ShapeDtypeStruct((t, SSD_INNER), BF16),
        scratch_shapes=[pltpu.VMEM((SSD_STATE, SSD_INNER), F32)],
        compiler_params=_params(("parallel", "arbitrary")),
        name="ssd_scan",
    )(xs, z, bm, cm, dt, la, la_t, *consts)


def _pad_cols(w, n):
    return jnp.concatenate([w, jnp.zeros((w.shape[0], n), w.dtype)], axis=1)


def _attention_layer(x2, tabs, batch, seq, w_in, q_norm, w_uq, kv_norm, w_ukv, w_out):
    o1 = MLA_Q_LORA + MLA_KV_LORA
    zc = lambda n: jnp.zeros((D_MODEL, n), w_in.dtype)
    o2 = o1 + MLA_ROPE + 2 * MOBA_WIDTH
    w_in2 = jnp.concatenate([w_in[:, :o1], zc(MLA_NOPE), w_in[:, o1:o1 + MLA_ROPE],
                             zc(LANES - MLA_NOPE - MLA_ROPE), w_in[:, o1 + MLA_ROPE:o2]], axis=1).astype(BF16)
    w_vb_t = w_in[:, o2:].T.astype(BF16)
    dqk = MLA_NOPE + MLA_ROPE
    w_uq2 = jnp.pad(w_uq.reshape(MLA_Q_LORA, MLA_HEADS, dqk), ((0, 0), (0, 0), (0, HEAD_PAD - dqk)))
    w_uq2 = w_uq2.reshape(MLA_Q_LORA, MLA_HEADS * HEAD_PAD).astype(BF16)
    w_kv3 = w_ukv.reshape(MLA_KV_LORA, MLA_HEADS, MLA_NOPE + MLA_V)
    w_k2 = jnp.pad(w_kv3[:, :, :MLA_NOPE], ((0, 0), (0, 0), (0, HEAD_PAD - MLA_NOPE)))
    w_k2 = w_k2.reshape(MLA_KV_LORA, MLA_HEADS * HEAD_PAD).astype(BF16)
    w_v_t = w_kv3[:, :, MLA_NOPE:].reshape(MLA_KV_LORA, MLA_HEADS * MLA_V).T.astype(BF16)

    qm, km, vm, qb, kb, vb, kmean = _att_proj(x2, tabs, w_in2, w_uq2, w_k2, w_v_t, w_vb_t,
                                              q_norm.reshape(1, -1), kv_norm.reshape(1, -1))
    o_mla = _attention(qm, km, vm, batch, seq)
    qa = _moba_select(qb, kmean, batch, seq)
    pos = jnp.arange(seq)
    onehot = (pos[:, None] // MOBA_BLOCK == jnp.arange(LANES)[None, :]).astype(BF16)
    o_moba = _attention(qa, kb, vb, batch, seq, onehot=onehot)
    n_mla = MLA_HEADS * MLA_V
    w_o = w_out.astype(BF16)
    return [o_mla, o_moba], [w_o[:n_mla], w_o[n_mla:]]


def _ssd_layer(x2, batch, seq, w_in, conv_w, conv_b, dt_bias, a_log, d_skip, norm_w, w_out):
    w_in2 = _pad_cols(w_in, LANES - SSD_HEADS).astype(BF16)
    dtb = _pad_cols(dt_bias.reshape(1, -1), LANES - SSD_HEADS)
    a_row = _pad_cols((-jnp.exp(a_log.astype(F32))).reshape(1, -1), LANES - SSD_HEADS)
    z, xs, bm, cm, dt, la, la_t = _ssd_proj(x2, w_in2, conv_w, conv_b.reshape(1, -1), dtb, a_row, seq)
    d_row = jnp.repeat(d_skip.astype(F32), SSD_HEAD_DIM).reshape(1, -1)
    y = _ssd_scan(xs, z, bm, cm, dt, la, la_t, d_row, norm_w.reshape(1, -1), batch, seq)
    return [y], [w_out.astype(BF16)]


def kernel(x, p, positions, att_w_in, mla_q_norm, mla_w_uq, mla_kv_norm, mla_w_ukv, att_w_out, ssd_w_in, ssd_conv_w, ssd_conv_b, ssd_dt_bias, ssd_a_log, ssd_d, ssd_norm, ssd_w_out, ln_mix_g, ln_mix_b, ffn_w_up, ffn_conv_w, ffn_conv_b, ffn_w_down, ln_ffn_g, ln_ffn_b, ple_w_gate, ple_w_proj):
    batch, seq, _ = x.shape
    depth = p.shape[0]
    t = batch * seq
    x2 = x.reshape(t, D_MODEL)
    tabs = _rope_tables(positions)
    for i in range(depth):
        j = i // 2
        if i % 2 == 0:
            acts, ws = _attention_layer(x2, tabs, batch, seq, att_w_in[j], mla_q_norm[j], mla_w_uq[j],
                                        mla_kv_norm[j], mla_w_ukv[j], att_w_out[j])
        else:
            acts, ws = _ssd_layer(x2, batch, seq, ssd_w_in[j], ssd_conv_w[j], ssd_conv_b[j], ssd_dt_bias[j],
                                  ssd_a_log[j], ssd_d[j], ssd_norm[j], ssd_w_out[j])
        x2 = _mix_ffn_ple(acts, ws, x2, p.reshape(depth * t, PLE_DIM), i,
                          ln_mix_g[i].reshape(1, -1), ln_mix_b[i].reshape(1, -1),
                          ffn_w_up[i].astype(BF16), ffn_conv_w[i], ffn_conv_b[i].reshape(1, -1),
                          ffn_w_down[i].astype(BF16), ln_ffn_g[i].reshape(1, -1), ln_ffn_b[i].reshape(1, -1),
                          ple_w_gate[i].astype(BF16), ple_w_proj[i].astype(BF16), seq)
    return x2.reshape(batch, seq, D_MODEL)
```
